```python
import jax, jax.numpy as jnp
from jax import lax
import numpy as np

D_MODEL = 2048
BATCH = 4
SEQ = 2048
DEPTH = 4
DEC_BATCH = 2
DEC_SEQ = 8192
PAST_LEN = 128

HEAD_DIM = 128
N_MEM = 256
A_GROUPS = ((128, 1), (512, 4), (2048, 16))
A_HEADS_PER_GROUP = 4
A_WIDTH = len(A_GROUPS) * A_HEADS_PER_GROUP * HEAD_DIM
A_OUT = A_HEADS_PER_GROUP * HEAD_DIM
D_Q_HEADS = 8
D_KV_HEADS = 2
D_RADIUS = 128
D_WIDTH = D_Q_HEADS * HEAD_DIM
B_WIDTH = 1024
CONV_WIDTH = 3
C_GROUP_DIM = 128
C_GROUPS = 8
C_WIDTH = C_GROUPS * C_GROUP_DIM
M_HEADS = 4
M_WIDTH = M_HEADS * HEAD_DIM
D_FF = 5632
ROPE_THETA = 10000.0
EPS = 1e-6
N_BRANCH = 4
IN_SPLITS = (A_WIDTH, A_WIDTH, A_WIDTH,
             D_WIDTH, D_KV_HEADS * HEAD_DIM, D_KV_HEADS * HEAD_DIM,
             B_WIDTH, B_WIDTH, B_WIDTH,
             C_WIDTH)
IN_WIDTH = 10240

kernel_name = "hybrid_dilated_window_conv_fourier_encoder"


def rms_norm(x, g):
    xf = x.astype(jnp.float32)
    y = xf * lax.rsqrt(jnp.mean(xf * xf, axis=-1, keepdims=True) + EPS)
    return (y * g.astype(jnp.float32)).astype(x.dtype)


def rotary(x, pos):
    half = x.shape[-1] // 2
    inv = ROPE_THETA ** (-jnp.arange(half, dtype=jnp.float32) / half)
    ang = pos.astype(jnp.float32)[:, None] * inv[None, :]
    cos = jnp.cos(ang)[None, :, None, :]
    sin = jnp.sin(ang)[None, :, None, :]
    xf = x.astype(jnp.float32)
    x1, x2 = xf[..., :half], xf[..., half:]
    return jnp.concatenate([x1 * cos - x2 * sin, x2 * cos + x1 * sin], axis=-1).astype(x.dtype)


def banded_attention(q, k, v, radius, sink=None):
    n, L, h, hd = q.shape
    g = k.shape[2]
    rep = h // g
    blk = radius
    nb = -(-L // blk)
    lp = nb * blk
    qp = jnp.pad(q, ((0, 0), (0, lp - L), (0, 0), (0, 0))).reshape(n, nb, blk, g, rep, hd)
    pad_kv = ((0, 0), (blk, lp - L + blk), (0, 0), (0, 0))
    kp = jnp.pad(k, pad_kv).reshape(n, nb + 2, blk, g, hd)
    vp = jnp.pad(v, pad_kv).reshape(n, nb + 2, blk, g, hd)
    kw = jnp.concatenate([kp[:, :-2], kp[:, 1:-1], kp[:, 2:]], axis=2)
    vw = jnp.concatenate([vp[:, :-2], vp[:, 1:-1], vp[:, 2:]], axis=2)
    qpos = jnp.arange(lp).reshape(nb, blk)
    kpos = jnp.arange(nb)[:, None] * blk + jnp.arange(3 * blk)[None, :] - blk
    mask = ((jnp.abs(qpos[:, :, None] - kpos[:, None, :]) <= radius)
            & (kpos[:, None, :] >= 0) & (kpos[:, None, :] < L))
    s = jnp.einsum('nbqgrd,nbkgd->nbgrqk', qp, kw,
                   preferred_element_type=jnp.float32) * (hd ** -0.5)
    s = jnp.where(mask[None, :, None, None], s, -jnp.inf)
    m = jnp.max(s, axis=-1)
    if sink is not None:
        sk = sink.astype(jnp.float32).reshape(g, rep)[None, None, :, :, None]
        m = jnp.maximum(m, sk)
    p = jnp.exp(s - m[..., None])
    den = jnp.sum(p, axis=-1)
    if sink is not None:
        den = den + jnp.exp(sk - m)
    o = jnp.einsum('nbgrqk,nbkgd->nbqgrd', p.astype(v.dtype), vw,
                   preferred_element_type=jnp.float32)
    den_t = jnp.transpose(den, (0, 1, 4, 2, 3))
    o = (o / den_t[..., None]).reshape(n, lp, h, hd)[:, :L]
    lse = jnp.transpose(m + jnp.log(den), (0, 1, 4, 2, 3)).reshape(n, lp, h)[:, :L]
    return o.astype(q.dtype), lse


def dilated_attention(qa, ka, va, pos):
    b, s, _ = qa.shape
    hpg = A_HEADS_PER_GROUP
    q = rotary(qa.reshape(b, s, -1, HEAD_DIM), pos)
    k = rotary(ka.reshape(b, s, -1, HEAD_DIM), pos)
    v = va.reshape(b, s, -1, HEAD_DIM)
    outs, lses = [], []
    for gi, (window, dil) in enumerate(A_GROUPS):
        sl = slice(gi * hpg, (gi + 1) * hpg)

        def to_sub(t):
            return (t[:, :, sl].reshape(b, s // dil, dil, hpg, HEAD_DIM)
                    .transpose(0, 2, 1, 3, 4).reshape(b * dil, s // dil, hpg, HEAD_DIM))

        o, lse = banded_attention(to_sub(q), to_sub(k), to_sub(v), window // (2 * dil))
        outs.append(o.reshape(b, dil, s // dil, hpg, HEAD_DIM)
                    .transpose(0, 2, 1, 3, 4).reshape(b, s, hpg, HEAD_DIM))
        lses.append(lse.reshape(b, dil, s // dil, hpg).transpose(0, 2, 1, 3).reshape(b, s, hpg))
    wts = jax.nn.softmax(jnp.stack(lses, axis=0), axis=0)
    o = jnp.sum(wts[..., None] * jnp.stack(outs, axis=0).astype(jnp.float32), axis=0)
    return o.reshape(b, s, A_OUT).astype(qa.dtype)


def short_conv(bg, cg, xin, w_conv):
    s = xin.shape[1]
    u = cg * xin
    half = CONV_WIDTH // 2
    up = jnp.pad(u, ((0, 0), (half, half), (0, 0)))
    y = sum(up[:, j:j + s] * w_conv[j] for j in range(CONV_WIDTH))
    return bg * y


def fourier_mix(u):
    b, s, _ = u.shape
    uf = u.astype(jnp.float32).reshape(b, s, C_GROUPS, C_GROUP_DIM)
    y = jnp.real(jnp.fft.fft2(uf, axes=(1, 3), norm="ortho"))
    return y.reshape(b, s, C_WIDTH).astype(u.dtype)


def hybrid_mixer(h, pos, w_in, w_conv, sink, w_gate, b_gate, w_pa, w_pd, w_pb, w_pc, w_o):
    b, s, _ = h.shape
    z = h @ w_in
    offsets = np.cumsum(IN_SPLITS)[:-1].tolist()
    qa, ka, va, qd, kd, vd, bg, cg, xb, uc = jnp.split(z, offsets, axis=-1)
    ya = dilated_attention(qa, ka, va, pos)
    qd = rotary(qd.reshape(b, s, D_Q_HEADS, HEAD_DIM), pos)
    kd = rotary(kd.reshape(b, s, D_KV_HEADS, HEAD_DIM), pos)
    vd = vd.reshape(b, s, D_KV_HEADS, HEAD_DIM)
    yd = banded_attention(qd, kd, vd, D_RADIUS, sink)[0].reshape(b, s, D_WIDTH)
    yb = short_conv(bg, cg, xb, w_conv)
    yc = fourier_mix(uc)
    branches = (ya, yd, yb, yc)
    projs = (w_pa, w_pd, w_pb, w_pc)
    merged = sum(jax.nn.sigmoid(h @ w_gate[i] + b_gate[i]) * (branches[i] @ projs[i])
                 for i in range(N_BRANCH))
    return merged @ w_o


def memory_attention(h, mem_n, w_mq, w_mkv, w_mo):
    b, s, _ = h.shape
    nm = mem_n.shape[1]
    q = (h @ w_mq).reshape(b, s, M_HEADS, HEAD_DIM)
    k, v = jnp.split(mem_n @ w_mkv, 2, axis=-1)
    k = k.reshape(b, nm, M_HEADS, HEAD_DIM)
    v = v.reshape(b, nm, M_HEADS, HEAD_DIM)
    sc = jnp.einsum('bshd,bmhd->bhsm', q, k, preferred_element_type=jnp.float32) * (HEAD_DIM ** -0.5)
    p = jax.nn.softmax(sc, axis=-1)
    o = jnp.einsum('bhsm,bmhd->bshd', p.astype(v.dtype), v)
    return o.reshape(b, s, M_WIDTH) @ w_mo


def swiglu(h, wg, wu, wd):
    return (jax.nn.silu(h @ wg) * (h @ wu)) @ wd


def trunk(x, mem, params):
    (g_ffn1, w_ffn1_gate, w_ffn1_up, w_ffn1_down, g_mix, w_in, w_conv, sink, w_gate, b_gate,
     w_pa, w_pd, w_pb, w_pc, w_o, g_memq, g_memkv, w_mq, w_mkv, w_mo,
     g_ffn2, w_ffn2_gate, w_ffn2_up, w_ffn2_down, g_final) = params
    pos = jnp.arange(x.shape[1], dtype=jnp.int32)
    for l in range(DEPTH):
        h = rms_norm(x, g_ffn1[l])
        x = x + 0.5 * swiglu(h, w_ffn1_gate[l], w_ffn1_up[l], w_ffn1_down[l])
        h = rms_norm(x, g_mix[l])
        x = x + hybrid_mixer(h, pos, w_in[l], w_conv[l], sink[l], w_gate[l], b_gate[l],
                             w_pa[l], w_pd[l], w_pb[l], w_pc[l], w_o[l])
        h = rms_norm(x, g_memq[l])
        mn = rms_norm(mem, g_memkv[l])
        x = x + memory_attention(h, mn, w_mq[l], w_mkv[l], w_mo[l])
        h = rms_norm(x, g_ffn2[l])
        x = x + 0.5 * swiglu(h, w_ffn2_gate[l], w_ffn2_up[l], w_ffn2_down[l])
    return rms_norm(x, g_final)


def setup_inputs(seed: int = 0) -> dict:
    key = jax.random.key(seed)
    kit = iter(jax.random.split(key, 40))
    f32 = jnp.float32

    def w(shape, fan_in):
        return jax.random.normal(next(kit), shape, f32) * (fan_in ** -0.5)

    def gain(shape):
        return 1.0 + 0.02 * jax.random.normal(next(kit), shape, f32)

    L, D, F = DEPTH, D_MODEL, D_FF
    return {
        "x_prompt": jax.random.normal(next(kit), (BATCH, SEQ, D), f32),
        "x_sample": jax.random.normal(next(kit), (DEC_BATCH, DEC_SEQ, D), f32),
        "mem_prompt": jax.random.normal(next(kit), (BATCH, N_MEM, D), f32),
        "mem_sample": jax.random.normal(next(kit), (DEC_BATCH, N_MEM, D), f32),
        "g_ffn1": gain((L, D)),
        "w_ffn1_gate": w((L, D, F), D),
        "w_ffn1_up": w((L, D, F), D),
        "w_ffn1_down": w((L, F, D), F),
        "g_mix": gain((L, D)),
        "w_in": w((L, D, IN_WIDTH), D),
        "w_conv": w((L, CONV_WIDTH, B_WIDTH), CONV_WIDTH),
        "sink": 0.5 * jax.random.normal(next(kit), (L, D_Q_HEADS), f32),
        "w_gate": w((L, N_BRANCH, D, D), D),
        "b_gate": 0.02 * jax.random.normal(next(kit), (L, N_BRANCH, D), f32),
        "w_pa": w((L, A_OUT, D), A_OUT),
        "w_pd": w((L, D_WIDTH, D), D_WIDTH),
        "w_pb": w((L, B_WIDTH, D), B_WIDTH),
        "w_pc": w((L, C_WIDTH, D), C_WIDTH),
        "w_o": w((L, D, D), D),
        "g_memq": gain((L, D)),
        "g_memkv": gain((L, D)),
        "w_mq": w((L, D, M_WIDTH), D),
        "w_mkv": w((L, D, 2 * M_WIDTH), D),
        "w_mo": w((L, M_WIDTH, D), M_WIDTH),
        "g_ffn2": gain((L, D)),
        "w_ffn2_gate": w((L, D, F), D),
        "w_ffn2_up": w((L, D, F), D),
        "w_ffn2_down": w((L, F, D), F),
        "g_final": gain((D,)),
    }


def reference(x_prompt, x_sample, mem_prompt, mem_sample,
              g_ffn1, w_ffn1_gate, w_ffn1_up, w_ffn1_down, g_mix, w_in, w_conv, sink, w_gate, b_gate,
              w_pa, w_pd, w_pb, w_pc, w_o, g_memq, g_memkv, w_mq, w_mkv, w_mo,
              g_ffn2, w_ffn2_gate, w_ffn2_up, w_ffn2_down, g_final):
    params = (g_ffn1, w_ffn1_gate, w_ffn1_up, w_ffn1_down, g_mix, w_in, w_conv, sink, w_gate, b_gate,
              w_pa, w_pd, w_pb, w_pc, w_o, g_memq, g_memkv, w_mq, w_mkv, w_mo,
              g_ffn2, w_ffn2_gate, w_ffn2_up, w_ffn2_down, g_final)
    y_prompt = trunk(x_prompt, mem_prompt, params)
    y_sample = trunk(x_sample, mem_sample, params)
    return (y_prompt, y_sample)
```

```python
import functools

import jax
import jax.numpy as jnp
from jax import lax
from jax.experimental import pallas as pl
from jax.experimental.pallas import tpu as pltpu

BF = jnp.bfloat16
F32 = jnp.float32

HEAD_DIM = 128
EPS = 1e-6
ROPE_THETA = 10000.0
A_DILATIONS = (1, 4, 16)
A_RADIUS = 64
A_HEADS_PER_GROUP = 4
A_WIDTH = 1536
D_Q_HEADS = 8
D_KV_HEADS = 2
D_RADIUS = 128
B_WIDTH = 1024
C_WIDTH = 1024
C_GROUP_DIM = 128
M_HEADS = 4

QA_COL, KA_COL, QD_COL, KD_COL, VD_COL, VA_COL, BG_COL, CG_COL, XB_COL, Z_WIDTH = (
    0, 1536, 3072, 4096, 4352, 4608, 6144, 7168, 8192, 9216)

ROW_CHUNK = 128
NEG_BIG = -1e30
VMEM_LIMIT_BYTES = 56 * 1024 * 1024


def _cparams(ndim):
    return pltpu.CompilerParams(dimension_semantics=("arbitrary",) * ndim,
                                vmem_limit_bytes=VMEM_LIMIT_BYTES)


def _grid_spec(grid, in_specs, out_specs, scratch=()):
    return pltpu.PrefetchScalarGridSpec(num_scalar_prefetch=1, grid=grid, in_specs=in_specs,
                                        out_specs=out_specs, scratch_shapes=list(scratch))


def _rms(x, g):
    return x * lax.rsqrt(jnp.mean(x * x, axis=-1, keepdims=True) + EPS) * g


def _dot(a, b):
    return jnp.dot(a, b, preferred_element_type=F32)


def _dot_t(a, b):
    return lax.dot_general(a, b, (((1,), (1,)), ((), ())), preferred_element_type=F32)


def _ffn_body(l_ref, x_ref, g_ref, wg_ref, wu_ref, wd_ref, o_ref, h_ref, acc_ref):
    j = pl.program_id(1)

    @pl.when(j == 0)
    def _():
        h_ref[...] = _rms(x_ref[...], g_ref[...]).astype(BF)
        acc_ref[...] = jnp.zeros_like(acc_ref)

    h = h_ref[...]
    gate = _dot(h, wg_ref[...])
    up = _dot(h, wu_ref[...])
    a = (gate * jax.nn.sigmoid(gate) * up).astype(BF)
    acc_ref[...] += _dot(a, wd_ref[...])

    @pl.when(j == pl.num_programs(1) - 1)
    def _():
        o_ref[...] = x_ref[...] + 0.5 * acc_ref[...]


def _ffn(lidx, x, g, wg, wu, wd):
    m, d = x.shape
    f = wg.shape[-1]
    tm, tf = min(512, m), min(512, f)
    return pl.pallas_call(
        _ffn_body,
        grid_spec=_grid_spec(
            (m // tm, f // tf),
            [pl.BlockSpec((tm, d), lambda i, j, l: (i, 0)),
             pl.BlockSpec((None, 1, d), lambda i, j, l: (l[0], 0, 0)),
             pl.BlockSpec((None, d, tf), lambda i, j, l: (l[0], 0, j)),
             pl.BlockSpec((None, d, tf), lambda i, j, l: (l[0], 0, j)),
             pl.BlockSpec((None, tf, d), lambda i, j, l: (l[0], j, 0))],
            pl.BlockSpec((tm, d), lambda i, j, l: (i, 0)),
            [pltpu.VMEM((tm, d), BF), pltpu.VMEM((tm, d), F32)]),
        out_shape=jax.ShapeDtypeStruct((m, d), F32),
        compiler_params=_cparams(2),
    )(lidx, x, g, wg, wu, wd)


def _proj_body(l_ref, x_ref, g_ref, w_ref, o_ref, h_ref):
    @pl.when(pl.program_id(1) == 0)
    def _():
        h_ref[...] = _rms(x_ref[...], g_ref[...]).astype(BF)

    o_ref[...] = _dot(h_ref[...], w_ref[...]).astype(o_ref.dtype)


def _proj_rot_body(l_ref, x_ref, g_ref, w_ref, cos_ref, sin_ref, o_ref, h_ref, *, tn):
    j = pl.program_id(1)

    @pl.when(j == 0)
    def _():
        h_ref[...] = _rms(x_ref[...], g_ref[...]).astype(BF)

    acc = _dot(h_ref[...], w_ref[...])
    chunks = tn // HEAD_DIM

    def rot(xc):
        return xc * cos_ref[...] + pltpu.roll(xc, HEAD_DIM // 2, 1) * sin_ref[...]

    is_q = (j < KA_COL // tn) | ((j >= QD_COL // tn) & (j < KD_COL // tn))
    scale = jnp.where(is_q, HEAD_DIM ** -0.5, 1.0).astype(F32)
    kd_tile = KD_COL // tn
    kd_chunks = (VD_COL - KD_COL) // HEAD_DIM

    @pl.when(j < kd_tile)
    def _():
        for c in range(chunks):
            sl = slice(c * HEAD_DIM, (c + 1) * HEAD_DIM)
            o_ref[:, sl] = (rot(acc[:, sl]) * scale).astype(BF)

    @pl.when(j == kd_tile)
    def _():
        for c in range(chunks):
            sl = slice(c * HEAD_DIM, (c + 1) * HEAD_DIM)
            o_ref[:, sl] = (rot(acc[:, sl]) if c < kd_chunks else acc[:, sl]).astype(BF)

    @pl.when(j > kd_tile)
    def _():
        o_ref[...] = acc.astype(BF)


def _proj(lidx, x, g, w, out_dtype=BF, tn=512):
    m, d = x.shape
    n = w.shape[-1]
    tm, tn = min(512, m), min(tn, n)
    return pl.pallas_call(
        _proj_body,
        grid_spec=_grid_spec(
            (m // tm, n // tn),
            [pl.BlockSpec((tm, d), lambda i, j, l: (i, 0)),
             pl.BlockSpec((None, 1, d), lambda i, j, l: (l[0], 0, 0)),
             pl.BlockSpec((None, d, tn), lambda i, j, l: (l[0], 0, j))],
            pl.BlockSpec((tm, tn), lambda i, j, l: (i, j)),
            [pltpu.VMEM((tm, d), BF)]),
        out_shape=jax.ShapeDtypeStruct((m, n), out_dtype),
        compiler_params=_cparams(2),
    )(lidx, x, g, w)


def _proj_rot(lidx, x, g, w, cos_t, sin_t, seq):
    m, d = x.shape
    n = w.shape[-1]
    tm, tn = min(512, m), 512
    assert n == Z_WIDTH and KD_COL % tn == 0 and seq % tm == 0
    pos_blocks = seq // tm
    return pl.pallas_call(
        functools.partial(_proj_rot_body, tn=tn),
        grid_spec=_grid_spec(
            (m // tm, n // tn),
            [pl.BlockSpec((tm, d), lambda i, j, l: (i, 0)),
             pl.BlockSpec((None, 1, d), lambda i, j, l: (l[0], 0, 0)),
             pl.BlockSpec((None, d, tn), lambda i, j, l: (l[0], 0, j)),
             pl.BlockSpec((tm, HEAD_DIM), lambda i, j, l: (i % pos_blocks, 0)),
             pl.BlockSpec((tm, HEAD_DIM), lambda i, j, l: (i % pos_blocks, 0))],
            pl.BlockSpec((tm, tn), lambda i, j, l: (i, j)),
            [pltpu.VMEM((tm, d), BF)]),
        out_shape=jax.ShapeDtypeStruct((m, n), BF),
        compiler_params=_cparams(2),
    )(lidx, x, g, w, cos_t, sin_t)


def _band_attn_body(l_ref, q_ref, kp_ref, kc_ref, kn_ref, vp_ref, vc_ref, vn_ref, *rest,
                    tq, length, radius, n_heads, rep, has_sink, want_lse):
    rest = list(rest)
    sink_ref = rest.pop(0) if has_sink else None
    o_ref = rest.pop(0)
    lse_ref = rest.pop(0) if want_lse else None
    i = pl.program_id(2)
    rc = ROW_CHUNK
    nc = tq // rc

    def window(prev_ref, cur_ref, next_ref, c, cols):
        parts = []
        for cc in (c - 1, c, c + 1):
            if cc < 0:
                parts.append(prev_ref[:, cols])
            elif cc >= nc:
                parts.append(next_ref[:, cols])
            else:
                parts.append(cur_ref[cc * rc:(cc + 1) * rc, cols])
        return jnp.concatenate(parts, axis=0)

    row = lax.broadcasted_iota(jnp.int32, (rc, 3 * rc), 0)
    col = lax.broadcasted_iota(jnp.int32, (rc, 3 * rc), 1)
    in_band = jnp.abs(col - rc - row) <= radius
    for c in range(nc):
        kpos = i * tq + (c - 1) * rc + col
        mask = in_band & (kpos >= 0) & (kpos < length)
        rows = slice(c * rc, (c + 1) * rc)
        for g in range(n_heads // rep):
            gcols = slice(g * HEAD_DIM, (g + 1) * HEAD_DIM)
            k = window(kp_ref, kc_ref, kn_ref, c, gcols)
            v = window(vp_ref, vc_ref, vn_ref, c, gcols)
            for r in range(rep):
                h = g * rep + r
                hcols = slice(h * HEAD_DIM, (h + 1) * HEAD_DIM)
                s = jnp.where(mask, _dot_t(q_ref[rows, hcols], k), NEG_BIG)
                m = jnp.max(s, axis=-1, keepdims=True)
                if has_sink:
                    sk = sink_ref[l_ref[0], h]
                    m = jnp.maximum(m, sk)
                p = jnp.exp(s - m)
                den = jnp.sum(p, axis=-1, keepdims=True)
                if has_sink:
                    den = den + jnp.exp(sk - m)
                o = _dot(p.astype(BF), v) / den
                o_ref[rows, hcols] = o.astype(o_ref.dtype)
                if want_lse:
                    lse_ref[rows, hcols] = jnp.broadcast_to(m + jnp.log(den), (rc, HEAD_DIM))


def _band_attention(lidx, z, batch, seq, dil, q_col, k_col, v_col, n_heads, rep, radius,
                    sink=None, want_lse=False):
    width = z.shape[1]
    length = seq // dil
    tq = min(512, length)
    nc = tq // ROW_CHUNK
    qw, kw = n_heads * HEAD_DIM, (n_heads // rep) * HEAD_DIM
    assert length % tq == 0 and tq % ROW_CHUNK == 0 and width % qw == 0 and width % kw == 0
    assert q_col % qw == 0 and k_col % kw == 0 and v_col % kw == 0
    last_chunk = length // ROW_CHUNK - 1
    zv = z.reshape(batch, length, dil * width)

    def kv_specs(col):
        cb, per = col // kw, width // kw
        return [
            pl.BlockSpec((None, ROW_CHUNK, kw),
                         lambda b, r, i, l: (b, jnp.maximum(i * nc - 1, 0), r * per + cb)),
            pl.BlockSpec((None, tq, kw), lambda b, r, i, l: (b, i, r * per + cb)),
            pl.BlockSpec((None, ROW_CHUNK, kw),
                         lambda b, r, i, l: (b, jnp.minimum((i + 1) * nc, last_chunk), r * per + cb)),
        ]

    in_specs = [pl.BlockSpec((None, tq, qw), lambda b, r, i, l: (b, i, r * (width // qw) + q_col // qw))]
    in_specs += kv_specs(k_col) + kv_specs(v_col)
    args = [lidx, zv] + [zv] * 6
    if sink is not None:
        in_specs.append(pl.BlockSpec(memory_space=pltpu.SMEM))
        args.append(sink)
    out_spec = pl.BlockSpec((None, tq, qw), lambda b, r, i, l: (b, i, r))
    out_shape = [jax.ShapeDtypeStruct((batch, length, dil * qw), BF)]
    out_specs = [out_spec]
    if want_lse:
        out_shape.append(jax.ShapeDtypeStruct((batch, length, dil * qw), F32))
        out_specs.append(out_spec)
    outs = pl.pallas_call(
        functools.partial(_band_attn_body, tq=tq, length=length, radius=radius, n_heads=n_heads,
                          rep=rep, has_sink=sink is not None, want_lse=want_lse),
        grid_spec=_grid_spec((batch, dil, length // tq), in_specs, out_specs),
        out_shape=out_shape,
        compiler_params=_cparams(3),
    )(*args)
    return [o.reshape(batch * seq, qw) for o in outs]


def _group_merge_body(l_ref, o0_ref, o1_ref, o2_ref, l0_ref, l1_ref, l2_ref, y_ref):
    l0, l1, l2 = l0_ref[...], l1_ref[...], l2_ref[...]
    m = jnp.maximum(jnp.maximum(l0, l1), l2)
    e0, e1, e2 = jnp.exp(l0 - m), jnp.exp(l1 - m), jnp.exp(l2 - m)
    num = e0 * o0_ref[...].astype(F32) + e1 * o1_ref[...].astype(F32) + e2 * o2_ref[...].astype(F32)
    y_ref[...] = (num / (e0 + e1 + e2)).astype(y_ref.dtype)


def _group_merge(lidx, outs, lses):
    m, w = outs[0].shape
    tm = min(1024, m)
    spec = pl.BlockSpec((tm, w), lambda i, l: (i, 0))
    return pl.pallas_call(
        _group_merge_body,
        grid_spec=_grid_spec((m // tm,), [spec] * 6, spec),
        out_shape=jax.ShapeDtypeStruct((m, w), BF),
        compiler_params=_cparams(1),
    )(lidx, *outs, *lses)


def _conv_body(l_ref, bg_ref, cg_ref, xb_ref, cgp_ref, xbp_ref, cgn_ref, xbn_ref, w_ref, o_ref,
               *, tm, seq):
    i = pl.program_id(0)
    u = cg_ref[...].astype(F32) * xb_ref[...].astype(F32)
    not_first = ((i * tm) % seq != 0).astype(F32)
    not_last = (((i + 1) * tm) % seq != 0).astype(F32)
    u_prev = cgp_ref[7:8, :].astype(F32) * xbp_ref[7:8, :].astype(F32) * not_first
    u_next = cgn_ref[0:1, :].astype(F32) * xbn_ref[0:1, :].astype(F32) * not_last
    row = lax.broadcasted_iota(jnp.int32, u.shape, 0)
    below = jnp.where(row == 0, u_prev, pltpu.roll(u, 1, 0))
    above = jnp.where(row == tm - 1, u_next, pltpu.roll(u, tm - 1, 0))
    y = below * w_ref[0:1, :] + u * w_ref[1:2, :] + above * w_ref[2:3, :]
    o_ref[...] = (bg_ref[...].astype(F32) * y).astype(o_ref.dtype)


def _short_conv(lidx, z, w_conv, seq):
    m = z.shape[0]
    tm, tc = min(512, m), 512
    halo = 8
    assert seq % tm == 0 and B_WIDTH % tc == 0
    last_halo = m // halo - 1

    def tile(col):
        return pl.BlockSpec((tm, tc), lambda i, j, l: (i, col // tc + j))

    def prev(col):
        return pl.BlockSpec((halo, tc), lambda i, j, l: (jnp.maximum(i * (tm // halo) - 1, 0), col // tc + j))

    def nxt(col):
        return pl.BlockSpec((halo, tc),
                            lambda i, j, l: (jnp.minimum((i + 1) * (tm // halo), last_halo), col // tc + j))

    return pl.pallas_call(
        functools.partial(_conv_body, tm=tm, seq=seq),
        grid_spec=_grid_spec(
            (m // tm, B_WIDTH // tc),
            [tile(BG_COL), tile(CG_COL), tile(XB_COL), prev(CG_COL), prev(XB_COL), nxt(CG_COL), nxt(XB_COL),
             pl.BlockSpec((None, 3, tc), lambda i, j, l: (l[0], 0, j))],
            pl.BlockSpec((tm, tc), lambda i, j, l: (i, j))),
        out_shape=jax.ShapeDtypeStruct((m, B_WIDTH), BF),
        compiler_params=_cparams(2),
    )(lidx, z, z, z, z, z, z, z, w_conv)


def _dft1_body(l_ref, u_ref, g_ref, ar_ref, ai_ref, *, tn1, s2, width):
    for jj in range(tn1):
        cols = slice(jj * width, (jj + 1) * width)
        r = _dot(g_ref[jj], u_ref[:, cols])
        ar_ref[:, cols] = r[:s2].astype(BF)
        ai_ref[:, cols] = r[s2:].astype(BF)


def _dft2_body(l_ref, ar_ref, ai_ref, m2_ref, cc_ref, sc_ref, y_ref, xr_ref, xi_ref,
               *, tk2, s1, width, scale):
    for j in range(tk2):
        a = jnp.concatenate([ar_ref[j], ai_ref[j]], axis=0)
        x = _dot(m2_ref[...], a)
        xr_ref[j * s1:(j + 1) * s1, :] = x[:s1].astype(BF)
        xi_ref[j * s1:(j + 1) * s1, :] = x[s1:].astype(BF)
    for g in range(width // C_GROUP_DIM):
        gc = slice(g * C_GROUP_DIM, (g + 1) * C_GROUP_DIM)
        y = (_dot(xr_ref[:, gc], cc_ref[...]) + _dot(xi_ref[:, gc], sc_ref[...])) * scale
        for j in range(tk2):
            y_ref[:, j * width + g * C_GROUP_DIM:j * width + (g + 1) * C_GROUP_DIM] = (
                y[j * s1:(j + 1) * s1].astype(y_ref.dtype))


def _fourier_mix(lidx, u, batch, seq, tables):
    g1, m2, cc, sc = tables
    width = u.shape[1]
    s2 = g1.shape[2]
    s1 = seq // s2
    tn1, tk2 = min(8, s1), 8
    uv = u.reshape(batch, s2, s1 * width)
    a_spec = pl.BlockSpec((None, s2, tn1 * width), lambda b, t, l: (b, 0, t))
    ar, ai = pl.pallas_call(
        functools.partial(_dft1_body, tn1=tn1, s2=s2, width=width),
        grid_spec=_grid_spec(
            (batch, s1 // tn1),
            [a_spec, pl.BlockSpec((tn1, 2 * s2, s2), lambda b, t, l: (t, 0, 0))],
            [a_spec, a_spec]),
        out_shape=[jax.ShapeDtypeStruct((batch, s2, s1 * width), BF)] * 2,
        compiler_params=_cparams(2),
    )(lidx, uv, g1)
    ar = ar.reshape(batch, s2, s1, width)
    ai = ai.reshape(batch, s2, s1, width)
    b_spec = pl.BlockSpec((None, tk2, s1, width), lambda b, t, l: (b, t, 0, 0))
    y = pl.pallas_call(
        functools.partial(_dft2_body, tk2=tk2, s1=s1, width=width,
                          scale=float((seq * C_GROUP_DIM) ** -0.5)),
        grid_spec=_grid_spec(
            (batch, s2 // tk2),
            [b_spec, b_spec,
             pl.BlockSpec((2 * s1, 2 * s1), lambda b, t, l: (0, 0)),
             pl.BlockSpec((C_GROUP_DIM, C_GROUP_DIM), lambda b, t, l: (0, 0)),
             pl.BlockSpec((C_GROUP_DIM, C_GROUP_DIM), lambda b, t, l: (0, 0))],
            pl.BlockSpec((None, s1, tk2 * width), lambda b, t, l: (b, 0, t)),
            [pltpu.VMEM((tk2 * s1, width), BF), pltpu.VMEM((tk2 * s1, width), BF)]),
        out_shape=jax.ShapeDtypeStruct((batch, s1, s2 * width), BF),
        compiler_params=_cparams(2),
    )(lidx, ar, ai, m2, cc, sc)
    return y.reshape(batch * seq, width)


def _dft_tables(seq, s2=128):
    s1 = seq // s2
    two_pi = 2.0 * jnp.pi

    def cs(idx, period):
        ang = (idx % period).astype(F32) * (two_pi / period)
        return jnp.cos(ang), jnp.sin(ang)

    n1 = jnp.arange(s1, dtype=jnp.int32)[:, None, None]
    k2 = jnp.arange(s2, dtype=jnp.int32)[None, :, None]
    n2 = jnp.arange(s2, dtype=jnp.int32)[None, None, :]
    ca, sa = cs(k2 * (n1 + s1 * n2), seq)
    g1 = jnp.concatenate([ca, -sa], axis=1).astype(BF)
    a = jnp.arange(s1, dtype=jnp.int32)
    cb, sb = cs(a[:, None] * a[None, :], s1)
    m2 = jnp.concatenate([jnp.concatenate([cb, sb], axis=1),
                          jnp.concatenate([-sb, cb], axis=1)], axis=0).astype(BF)
    c = jnp.arange(C_GROUP_DIM, dtype=jnp.int32)
    cc, sc = cs(c[:, None] * c[None, :], C_GROUP_DIM)
    return g1, m2, cc.astype(BF), sc.astype(BF)


def _merge_body(l_ref, x_ref, g_ref, wgate_ref, bgate_ref, ya_ref, yd_ref, yb_ref, yc_ref,
                wpa_ref, wpd_ref, wpb_ref, wpc_ref, wo_ref, o_ref, h_ref, acc_ref):
    n = pl.program_id(1)

    @pl.when(n == 0)
    def _():
        h_ref[...] = _rms(x_ref[...], g_ref[...]).astype(BF)
        acc_ref[...] = jnp.zeros_like(acc_ref)

    h = h_ref[...]
    merged = None
    for i, (br_ref, wp_ref) in enumerate(((ya_ref, wpa_ref), (yd_ref, wpd_ref),
                                          (yb_ref, wpb_ref), (yc_ref, wpc_ref))):
        gate = jax.nn.sigmoid(_dot(h, wgate_ref[i]) + bgate_ref[i])
        term = gate * _dot(br_ref[...], wp_ref[...])
        merged = term if merged is None else merged + term
    acc_ref[...] += _dot(merged.astype(BF), wo_ref[...])

    @pl.when(n == pl.num_programs(1) - 1)
    def _():
        o_ref[...] = x_ref[...] + acc_ref[...]


def _merge(lidx, x, g, w_gate, b_gate, branches, projs, w_o):
    m, d = x.shape
    tm, tn = min(512, m), min(256, d)

    def row_tile(a):
        return pl.BlockSpec((tm, a.shape[1]), lambda i, n, l: (i, 0))

    def proj_tile(w):
        return pl.BlockSpec((None, w.shape[1], tn), lambda i, n, l: (l[0], 0, n))

    return pl.pallas_call(
        _merge_body,
        grid_spec=_grid_spec(
            (m // tm, d // tn),
            [row_tile(x),
             pl.BlockSpec((None, 1, d), lambda i, n, l: (l[0], 0, 0)),
             pl.BlockSpec((None, 4, d, tn), lambda i, n, l: (l[0], 0, 0, n)),
             pl.BlockSpec((None, 4, 1, tn), lambda i, n, l: (l[0], 0, 0, n))]
            + [row_tile(b) for b in branches] + [proj_tile(w) for w in projs]
            + [pl.BlockSpec((None, tn, d), lambda i, n, l: (l[0], n, 0))],
            row_tile(x),
            [pltpu.VMEM((tm, d), BF), pltpu.VMEM((tm, d), F32)]),
        out_shape=jax.ShapeDtypeStruct((m, d), F32),
        compiler_params=_cparams(2),
    )(lidx, x, g, w_gate, b_gate, *branches, *projs, w_o)


def _mem_body(l_ref, x_ref, g_ref, wq_ref, kv_ref, wo_ref, o_ref, att_ref):
    x = x_ref[...]
    h = _rms(x, g_ref[...]).astype(BF)
    q = (_dot(h, wq_ref[...]) * HEAD_DIM ** -0.5).astype(BF)
    mw = M_HEADS * HEAD_DIM
    for hh in range(M_HEADS):
        cols = slice(hh * HEAD_DIM, (hh + 1) * HEAD_DIM)
        k = kv_ref[:, cols]
        v = kv_ref[:, mw + hh * HEAD_DIM:mw + (hh + 1) * HEAD_DIM]
        s = _dot_t(q[:, cols], k)
        p = jnp.exp(s - jnp.max(s, axis=-1, keepdims=True))
        o = _dot(p.astype(BF), v) / jnp.sum(p, axis=-1, keepdims=True)
        att_ref[:, cols] = o.astype(BF)
    o_ref[...] = x + _dot(att_ref[...], wo_ref[...])


def _memory_attention(lidx, x, g, w_mq, kv, w_mo, seq):
    m, d = x.shape
    tm = min(512, m)
    n_mem, kvw = kv.shape[1], kv.shape[2]
    mw = w_mq.shape[-1]
    assert seq % tm == 0
    return pl.pallas_call(
        _mem_body,
        grid_spec=_grid_spec(
            (m // tm,),
            [pl.BlockSpec((tm, d), lambda i, l: (i, 0)),
             pl.BlockSpec((None, 1, d), lambda i, l: (l[0], 0, 0)),
             pl.BlockSpec((None, d, mw), lambda i, l: (l[0], 0, 0)),
             pl.BlockSpec((None, n_mem, kvw), lambda i, l: ((i * tm) // seq, 0, 0)),
             pl.BlockSpec((None, mw, d), lambda i, l: (l[0], 0, 0))],
            pl.BlockSpec((tm, d), lambda i, l: (i, 0)),
            [pltpu.VMEM((tm, mw), BF)]),
        out_shape=jax.ShapeDtypeStruct((m, d), F32),
        compiler_params=_cparams(1),
    )(lidx, x, g, w_mq, kv, w_mo)


def _final_norm_body(l_ref, x_ref, g_ref, o_ref):
    o_ref[...] = _rms(x_ref[...], g_ref[...])


def _final_norm(lidx, x, g):
    m, d = x.shape
    tm = min(512, m)
    return pl.pallas_call(
        _final_norm_body,
        grid_spec=_grid_spec((m // tm,),
                             [pl.BlockSpec((tm, d), lambda i, l: (i, 0)),
                              pl.BlockSpec((1, d), lambda i, l: (0, 0))],
                             pl.BlockSpec((tm, d), lambda i, l: (i, 0))),
        out_shape=jax.ShapeDtypeStruct((m, d), F32),
        compiler_params=_cparams(1),
    )(lidx, x, g)


def _rope_tables(seq):
    half = HEAD_DIM // 2
    inv = ROPE_THETA ** (-jnp.arange(half, dtype=F32) / half)
    ang = jnp.arange(seq, dtype=jnp.int32).astype(F32)[:, None] * inv[None, :]
    cos, sin = jnp.cos(ang), jnp.sin(ang)
    return jnp.concatenate([cos, cos], axis=1), jnp.concatenate([-sin, sin], axis=1)


def _layer(lidx, x, mem, p, batch, seq, tabs):
    cos_t, sin_t, dft = tabs
    x = _ffn(lidx, x, p["g_ffn1"], p["w_ffn1_gate"], p["w_ffn1_up"], p["w_ffn1_down"])

    z = _proj_rot(lidx, x, p["g_mix"], p["w_z"], cos_t, sin_t, seq)
    uc = _proj(lidx, x, p["g_mix"], p["w_uc"])
    outs, lses = [], []
    for gi, dil in enumerate(A_DILATIONS):
        gw = A_HEADS_PER_GROUP * HEAD_DIM
        o, lse = _band_attention(lidx, z, batch, seq, dil, QA_COL + gi * gw, KA_COL + gi * gw,
                                 VA_COL + gi * gw, A_HEADS_PER_GROUP, 1, A_RADIUS, want_lse=True)
        outs.append(o)
        lses.append(lse)
    ya = _group_merge(lidx, outs, lses)
    (yd,) = _band_attention(lidx, z, batch, seq, 1, QD_COL, KD_COL, VD_COL, D_Q_HEADS,
                            D_Q_HEADS // D_KV_HEADS, D_RADIUS, sink=p["sink"])
    yb = _short_conv(lidx, z, p["w_conv"], seq)
    yc = _fourier_mix(lidx, uc, batch, seq, dft)
    x = _merge(lidx, x, p["g_mix"], p["w_gate"], p["b_gate"], (ya, yd, yb, yc),
               (p["w_pa"], p["w_pd"], p["w_pb"], p["w_pc"]), p["w_o"])

    kv = _proj(lidx, mem, p["g_memkv"], p["w_mkv"])
    kv = kv.reshape(batch, mem.shape[0] // batch, kv.shape[1])
    x = _memory_attention(lidx, x, p["g_memq"], p["w_mq"], kv, p["w_mo"], seq)

    x = _ffn(lidx, x, p["g_ffn2"], p["w_ffn2_gate"], p["w_ffn2_up"], p["w_ffn2_down"])
    return x


def _forward(x_prompt, x_sample, mem_prompt, mem_sample, p, g_final):
    depth = p["w_o"].shape[0]
    d = x_prompt.shape[-1]
    trunks = []
    for x, mem in ((x_prompt, mem_prompt), (x_sample, mem_sample)):
        batch, seq = x.shape[0], x.shape[1]
        tabs = _rope_tables(seq) + (_dft_tables(seq),)
        trunks.append((batch, seq, tabs, mem.reshape(-1, d)))

    def body(xs, l):
        lidx = l.reshape(1)
        return tuple(_layer(lidx, x, mem, p, batch, seq, tabs)
                     for x, (batch, seq, tabs, mem) in zip(xs, trunks)), None

    xs = (x_prompt.reshape(-1, d), x_sample.reshape(-1, d))
    xs, _ = lax.scan(body, xs, jnp.arange(depth, dtype=jnp.int32))
    zero = jnp.zeros((1,), jnp.int32)
    gf = g_final.reshape(1, d)
    return tuple(_final_norm(zero, x, gf).reshape(orig.shape)
                 for x, orig in zip(xs, (x_prompt, x_sample)))


def kernel(x_prompt, x_sample, mem_prompt, mem_sample, g_ffn1, w_ffn1_gate, w_ffn1_up, w_ffn1_down, g_mix, w_in, w_conv, sink, w_gate, b_gate, w_pa, w_pd, w_pb, w_pc, w_o, g_memq, g_memkv, w_mq, w_mkv, w_mo, g_ffn2, w_ffn2_gate, w_ffn2_up, w_ffn2_down, g_final):
    depth, d = g_ffn1.shape

    def gain(g):
        return g.reshape(depth, 1, d)

    w_z = jnp.concatenate([w_in[..., 0:3072], w_in[..., 4608:6144], w_in[..., 3072:4608],
                           w_in[..., 6144:9216]], axis=-1).astype(BF)
    p = {
        "g_ffn1": gain(g_ffn1), "w_ffn1_gate": w_ffn1_gate.astype(BF), "w_ffn1_up": w_ffn1_up.astype(BF),
        "w_ffn1_down": w_ffn1_down.astype(BF),
        "g_mix": gain(g_mix), "w_z": w_z, "w_uc": w_in[..., 9216:].astype(BF),
        "w_conv": w_conv, "sink": sink,
        "w_gate": w_gate.astype(BF), "b_gate": b_gate.reshape(depth, 4, 1, d),
        "w_pa": w_pa.astype(BF), "w_pd": w_pd.astype(BF), "w_pb": w_pb.astype(BF), "w_pc": w_pc.astype(BF),
        "w_o": w_o.astype(BF),
        "g_memq": gain(g_memq), "g_memkv": gain(g_memkv), "w_mq": w_mq.astype(BF),
        "w_mkv": w_mkv.astype(BF), "w_mo": w_mo.astype(BF),
        "g_ffn2": gain(g_ffn2), "w_ffn2_gate": w_ffn2_gate.astype(BF), "w_ffn2_up": w_ffn2_up.astype(BF),
        "w_ffn2_down": w_ffn2_down.astype(BF),
    }
    return _forward(x_prompt, x_sample, mem_prompt, mem_sample, p, g_final)
```

```python
import functools

import jax
import jax.numpy as jnp
from jax import lax
from jax.experimental import pallas as pl
from jax.experimental.pallas import tpu as pltpu

BF = jnp.bfloat16
F32 = jnp.float32

HEAD_DIM = 128
EPS = 1e-6
ROPE_THETA = 10000.0
A_DILATIONS = (1, 4, 16)
A_RADIUS = 64
A_HEADS_PER_GROUP = 4
A_WIDTH = 1536
D_Q_HEADS = 8
D_KV_HEADS = 2
D_RADIUS = 128
B_WIDTH = 1024
C_WIDTH = 1024
C_GROUP_DIM = 128
M_HEADS = 4

QA0_COL, KA0_COL, QD_COL, KD_COL, VD_COL, VA0_COL, BG_COL, CG_COL, XB_COL, Z_WIDTH = (
    0, 512, 1024, 2048, 2304, 2560, 3072, 4096, 5120, 6144)
PROJ_TILE = 512
GROUP_WIDTH = 512
PLAIN, ROT_Q, ROT_K, ROT_HALF = 0, 1, 2, 3
TILE_KINDS = ((ROT_Q, ROT_K, ROT_Q, ROT_Q, ROT_HALF) + (PLAIN,) * 7
              + (ROT_Q, ROT_K, PLAIN) * 2
              + (PLAIN,) * 2)
Z_TILES, G_TILES, U_TILES = 12, 3, 2

ROW_CHUNK = 128
NEG_BIG = -1e30
VMEM_LIMIT_BYTES = 56 * 1024 * 1024


def _cparams(ndim):
    return pltpu.CompilerParams(dimension_semantics=("arbitrary",) * ndim,
                                vmem_limit_bytes=VMEM_LIMIT_BYTES)


def _grid_spec(grid, in_specs, out_specs, scratch=()):
    return pltpu.PrefetchScalarGridSpec(num_scalar_prefetch=1, grid=grid, in_specs=in_specs,
                                        out_specs=out_specs, scratch_shapes=list(scratch))


def _rms(x, g):
    return x * lax.rsqrt(jnp.mean(x * x, axis=-1, keepdims=True) + EPS) * g


def _dot(a, b):
    return jnp.dot(a, b, preferred_element_type=F32)


def _dot_t(a, b):
    return lax.dot_general(a, b, (((1,), (1,)), ((), ())), preferred_element_type=F32)


def _ffn_body(l_ref, x_ref, g_ref, wg_ref, wu_ref, wd_ref, o_ref, h_ref, acc_ref):
    j = pl.program_id(1)

    @pl.when(j == 0)
    def _():
        h_ref[...] = _rms(x_ref[...], g_ref[...]).astype(BF)
        acc_ref[...] = jnp.zeros_like(acc_ref)

    h = h_ref[...]
    gate = _dot(h, wg_ref[...])
    up = _dot(h, wu_ref[...])
    a = (gate * jax.nn.sigmoid(gate) * up).astype(BF)
    acc_ref[...] += _dot(a, wd_ref[...])

    @pl.when(j == pl.num_programs(1) - 1)
    def _():
        o_ref[...] = x_ref[...] + 0.5 * acc_ref[...]


def _ffn(lidx, x, g, wg, wu, wd):
    m, d = x.shape
    f = wg.shape[-1]
    tm, tf = min(512, m), min(512, f)
    return pl.pallas_call(
        _ffn_body,
        grid_spec=_grid_spec(
            (m // tm, f // tf),
            [pl.BlockSpec((tm, d), lambda i, j, l: (i, 0)),
             pl.BlockSpec((None, 1, d), lambda i, j, l: (l[0], 0, 0)),
             pl.BlockSpec((None, d, tf), lambda i, j, l: (l[0], 0, j)),
             pl.BlockSpec((None, d, tf), lambda i, j, l: (l[0], 0, j)),
             pl.BlockSpec((None, tf, d), lambda i, j, l: (l[0], j, 0))],
            pl.BlockSpec((tm, d), lambda i, j, l: (i, 0)),
            [pltpu.VMEM((tm, d), BF), pltpu.VMEM((tm, d), F32)]),
        out_shape=jax.ShapeDtypeStruct((m, d), F32),
        compiler_params=_cparams(2),
    )(lidx, x, g, wg, wu, wd)


def _proj_body(l_ref, x_ref, g_ref, w_ref, o_ref, h_ref):
    @pl.when(pl.program_id(1) == 0)
    def _():
        h_ref[...] = _rms(x_ref[...], g_ref[...]).astype(BF)

    o_ref[...] = _dot(h_ref[...], w_ref[...]).astype(o_ref.dtype)


def _proj_in_body(l_ref, kind_ref, x_ref, g_ref, w_ref, cos_ref, sin_ref,
                  z_ref, g1_ref, g2_ref, uc_ref, h_ref, res_ref, *, tm, s1):
    j = pl.program_id(1)

    @pl.when(j == 0)
    def _():
        h_ref[...] = _rms(x_ref[...], g_ref[...]).astype(BF)

    acc = _dot(h_ref[...], w_ref[...])
    kind = kind_ref[j]
    chunks = PROJ_TILE // HEAD_DIM
    half_chunks = (VD_COL - KD_COL) // HEAD_DIM

    def rot(xc):
        return xc * cos_ref[...] + pltpu.roll(xc, HEAD_DIM // 2, 1) * sin_ref[...]

    def lanes(c):
        return slice(c * HEAD_DIM, (c + 1) * HEAD_DIM)

    def rotate_into_res(n_rot, scale):
        for c in range(chunks):
            if c < n_rot:
                res_ref[c] = rot(acc[:, lanes(c)]) * scale if scale != 1.0 else rot(acc[:, lanes(c)])
            else:
                res_ref[c] = acc[:, lanes(c)]

    @pl.when(kind == PLAIN)
    def _():
        rotate_into_res(0, 1.0)

    @pl.when(kind == ROT_Q)
    def _():
        rotate_into_res(chunks, HEAD_DIM ** -0.5)

    @pl.when(kind == ROT_K)
    def _():
        rotate_into_res(chunks, 1.0)

    @pl.when(kind == ROT_HALF)
    def _():
        rotate_into_res(half_chunks, 1.0)

    def store_residue_major(dst_ref, dil):
        rows = tm // dil
        for r in range(dil):
            for c in range(chunks):
                dst_ref[r, :, lanes(c)] = res_ref[c, pl.ds(r, rows, stride=dil), :].astype(BF)

    g1_0, g2_0, u_0 = Z_TILES, Z_TILES + G_TILES, Z_TILES + 2 * G_TILES

    @pl.when(j < g1_0)
    def _():
        for c in range(chunks):
            z_ref[:, lanes(c)] = res_ref[c].astype(BF)

    @pl.when((j >= g1_0) & (j < g2_0))
    def _():
        store_residue_major(g1_ref, A_DILATIONS[1])

    @pl.when((j >= g2_0) & (j < u_0))
    def _():
        store_residue_major(g2_ref, A_DILATIONS[2])

    @pl.when(j >= u_0)
    def _():
        store_residue_major(uc_ref, s1)


def _proj(lidx, x, g, w, out_dtype=BF, tn=512):
    m, d = x.shape
    n = w.shape[-1]
    tm, tn = min(512, m), min(tn, n)
    return pl.pallas_call(
        _proj_body,
        grid_spec=_grid_spec(
            (m // tm, n // tn),
            [pl.BlockSpec((tm, d), lambda i, j, l: (i, 0)),
             pl.BlockSpec((None, 1, d), lambda i, j, l: (l[0], 0, 0)),
             pl.BlockSpec((None, d, tn), lambda i, j, l: (l[0], 0, j))],
            pl.BlockSpec((tm, tn), lambda i, j, l: (i, j)),
            [pltpu.VMEM((tm, d), BF)]),
        out_shape=jax.ShapeDtypeStruct((m, n), out_dtype),
        compiler_params=_cparams(2),
    )(lidx, x, g, w)


def _proj_in(lidx, x, g, w, cos_t, sin_t, batch, seq, s1):
    m, d = x.shape
    tm, tn = min(512, m), PROJ_TILE
    n_tiles = len(TILE_KINDS)
    d1, d2 = A_DILATIONS[1], A_DILATIONS[2]
    assert w.shape[-1] == n_tiles * tn and seq % tm == 0 and tm % (16 * max(d2, s1)) == 0
    nsb = seq // tm
    g1_0, g2_0, u_0 = Z_TILES, Z_TILES + G_TILES, Z_TILES + 2 * G_TILES

    def residue_major(dil, first_tile, tiles):
        return pl.BlockSpec((None, dil, tm // dil, tn),
                            lambda i, j, l, k: (i // nsb, 0, i % nsb, jnp.clip(j - first_tile, 0, tiles - 1)))

    return pl.pallas_call(
        functools.partial(_proj_in_body, tm=tm, s1=s1),
        grid_spec=pltpu.PrefetchScalarGridSpec(
            num_scalar_prefetch=2,
            grid=(m // tm, n_tiles),
            in_specs=[pl.BlockSpec((tm, d), lambda i, j, l, k: (i, 0)),
                      pl.BlockSpec((None, 1, d), lambda i, j, l, k: (l[0], 0, 0)),
                      pl.BlockSpec((None, d, tn), lambda i, j, l, k: (l[0], 0, j)),
                      pl.BlockSpec((tm, HEAD_DIM), lambda i, j, l, k: (i % nsb, 0)),
                      pl.BlockSpec((tm, HEAD_DIM), lambda i, j, l, k: (i % nsb, 0))],
            out_specs=[pl.BlockSpec((tm, tn), lambda i, j, l, k: (i, jnp.minimum(j, Z_TILES - 1))),
                       residue_major(d1, g1_0, G_TILES),
                       residue_major(d2, g2_0, G_TILES),
                       residue_major(s1, u_0, U_TILES)],
            scratch_shapes=[pltpu.VMEM((tm, d), BF), pltpu.VMEM((tn // HEAD_DIM, tm, HEAD_DIM), F32)]),
        out_shape=[jax.ShapeDtypeStruct((m, Z_WIDTH), BF),
                   jax.ShapeDtypeStruct((batch, d1, seq // d1, G_TILES * tn), BF),
                   jax.ShapeDtypeStruct((batch, d2, seq // d2, G_TILES * tn), BF),
                   jax.ShapeDtypeStruct((batch, s1, seq // s1, U_TILES * tn), BF)],
        compiler_params=_cparams(2),
    )(lidx, jnp.asarray(TILE_KINDS, jnp.int32), x, g, w, cos_t, sin_t)


def _band_attn_body(l_ref, q_ref, kp_ref, kc_ref, kn_ref, vp_ref, vc_ref, vn_ref, *rest,
                    tq, length, radius, n_heads, rep, has_sink, want_lse):
    rest = list(rest)
    sink_ref = rest.pop(0) if has_sink else None
    o_ref = rest.pop(0)
    lse_ref = rest.pop(0) if want_lse else None
    i = pl.program_id(2)
    rc = ROW_CHUNK
    nc = tq // rc

    def window(prev_ref, cur_ref, next_ref, c, cols):
        parts = []
        for cc in (c - 1, c, c + 1):
            if cc < 0:
                parts.append(prev_ref[:, cols])
            elif cc >= nc:
                parts.append(next_ref[:, cols])
            else:
                parts.append(cur_ref[cc * rc:(cc + 1) * rc, cols])
        return jnp.concatenate(parts, axis=0)

    row = lax.broadcasted_iota(jnp.int32, (rc, 3 * rc), 0)
    col = lax.broadcasted_iota(jnp.int32, (rc, 3 * rc), 1)
    in_band = jnp.abs(col - rc - row) <= radius
    for c in range(nc):
        kpos = i * tq + (c - 1) * rc + col
        mask = in_band & (kpos >= 0) & (kpos < length)
        rows = slice(c * rc, (c + 1) * rc)
        for g in range(n_heads // rep):
            gcols = slice(g * HEAD_DIM, (g + 1) * HEAD_DIM)
            k = window(kp_ref, kc_ref, kn_ref, c, gcols)
            v = window(vp_ref, vc_ref, vn_ref, c, gcols)
            for r in range(rep):
                h = g * rep + r
                hcols = slice(h * HEAD_DIM, (h + 1) * HEAD_DIM)
                s = jnp.where(mask, _dot_t(q_ref[rows, hcols], k), NEG_BIG)
                m = jnp.max(s, axis=-1, keepdims=True)
                if has_sink:
                    sk = sink_ref[l_ref[0], h]
                    m = jnp.maximum(m, sk)
                p = jnp.exp(s - m)
                den = jnp.sum(p, axis=-1, keepdims=True)
                if has_sink:
                    den = den + jnp.exp(sk - m)
                o = _dot(p.astype(BF), v) / den
                o_ref[rows, hcols] = o.astype(o_ref.dtype)
                if want_lse:
                    lse_ref[rows, hcols] = jnp.broadcast_to(m + jnp.log(den), (rc, HEAD_DIM))


def _band_attention(lidx, src, q_col, k_col, v_col, n_heads, rep, radius, sink=None, want_lse=False):
    batch, dil, length, width = src.shape
    tq = min(512, length)
    nc = tq // ROW_CHUNK
    qw, kw = n_heads * HEAD_DIM, (n_heads // rep) * HEAD_DIM
    assert length % tq == 0 and tq % ROW_CHUNK == 0
    assert q_col % qw == 0 and k_col % kw == 0 and v_col % kw == 0
    last_chunk = length // ROW_CHUNK - 1

    def kv_specs(col):
        cb = col // kw
        return [
            pl.BlockSpec((None, None, ROW_CHUNK, kw),
                         lambda b, r, i, l: (b, r, jnp.maximum(i * nc - 1, 0), cb)),
            pl.BlockSpec((None, None, tq, kw), lambda b, r, i, l: (b, r, i, cb)),
            pl.BlockSpec((None, None, ROW_CHUNK, kw),
                         lambda b, r, i, l: (b, r, jnp.minimum((i + 1) * nc, last_chunk), cb)),
        ]

    in_specs = [pl.BlockSpec((None, None, tq, qw), lambda b, r, i, l: (b, r, i, q_col // qw))]
    in_specs += kv_specs(k_col) + kv_specs(v_col)
    args = [lidx] + [src] * 7
    if sink is not None:
        in_specs.append(pl.BlockSpec(memory_space=pltpu.SMEM))
        args.append(sink)
    out_spec = pl.BlockSpec((None, None, tq, qw), lambda b, r, i, l: (b, r, i, 0))
    out_shape = [jax.ShapeDtypeStruct((batch, dil, length, qw), BF)]
    out_specs = [out_spec]
    if want_lse:
        out_shape.append(jax.ShapeDtypeStruct((batch, dil, length, qw), F32))
        out_specs.append(out_spec)
    return pl.pallas_call(
        functools.partial(_band_attn_body, tq=tq, length=length, radius=radius, n_heads=n_heads,
                          rep=rep, has_sink=sink is not None, want_lse=want_lse),
        grid_spec=_grid_spec((batch, dil, length // tq), in_specs, out_specs),
        out_shape=out_shape,
        compiler_params=_cparams(3),
    )(*args)


def _group_merge_body(l_ref, o0_ref, l0_ref, o1_ref, l1_ref, o2_ref, l2_ref, y_ref, nat_ref, *, tm):
    heads = A_HEADS_PER_GROUP
    for slot, (src_ref, dil) in enumerate(((o1_ref, A_DILATIONS[1]), (l1_ref, A_DILATIONS[1]),
                                           (o2_ref, A_DILATIONS[2]), (l2_ref, A_DILATIONS[2]))):
        rows = tm // dil
        for r in range(dil):
            for h in range(heads):
                nat_ref[slot * heads + h, pl.ds(r, rows, stride=dil), :] = (
                    src_ref[r, :, h * HEAD_DIM:(h + 1) * HEAD_DIM].astype(F32))
    for h in range(heads):
        hc = slice(h * HEAD_DIM, (h + 1) * HEAD_DIM)
        o1, l1, o2, l2 = (nat_ref[slot * heads + h] for slot in range(4))
        l0 = l0_ref[:, hc]
        m = jnp.maximum(jnp.maximum(l0, l1), l2)
        e0, e1, e2 = jnp.exp(l0 - m), jnp.exp(l1 - m), jnp.exp(l2 - m)
        num = e0 * o0_ref[:, hc].astype(F32) + e1 * o1 + e2 * o2
        y_ref[:, hc] = (num / (e0 + e1 + e2)).astype(y_ref.dtype)


def _group_merge(lidx, outs, lses, seq):
    batch, _, _, w = outs[0].shape
    m = batch * seq
    tm = min(512, seq)
    nsb = seq // tm
    assert tm % (16 * A_DILATIONS[2]) == 0

    def spec(dil):
        return pl.BlockSpec((None, dil, tm // dil, w), lambda i, l: (i // nsb, 0, i % nsb, 0))

    args = []
    in_specs = []
    for o, lse, dil in zip(outs, lses, A_DILATIONS):
        args += [o, lse]
        in_specs += [spec(dil)] * 2
    in_specs[0] = in_specs[1] = pl.BlockSpec((None, None, tm, w), lambda i, l: (i // nsb, 0, i % nsb, 0))
    return pl.pallas_call(
        functools.partial(_group_merge_body, tm=tm),
        grid_spec=_grid_spec((m // tm,), in_specs, pl.BlockSpec((tm, w), lambda i, l: (i, 0)),
                             [pltpu.VMEM((4 * A_HEADS_PER_GROUP, tm, HEAD_DIM), F32)]),
        out_shape=jax.ShapeDtypeStruct((m, w), BF),
        compiler_params=_cparams(1),
    )(lidx, *args)


def _conv_body(l_ref, bg_ref, cg_ref, xb_ref, cgp_ref, xbp_ref, cgn_ref, xbn_ref, w_ref, o_ref,
               *, tm, seq):
    i = pl.program_id(0)
    u = cg_ref[...].astype(F32) * xb_ref[...].astype(F32)
    not_first = ((i * tm) % seq != 0).astype(F32)
    not_last = (((i + 1) * tm) % seq != 0).astype(F32)
    u_prev = cgp_ref[7:8, :].astype(F32) * xbp_ref[7:8, :].astype(F32) * not_first
    u_next = cgn_ref[0:1, :].astype(F32) * xbn_ref[0:1, :].astype(F32) * not_last
    row = lax.broadcasted_iota(jnp.int32, u.shape, 0)
    below = jnp.where(row == 0, u_prev, pltpu.roll(u, 1, 0))
    above = jnp.where(row == tm - 1, u_next, pltpu.roll(u, tm - 1, 0))
    y = below * w_ref[0:1, :] + u * w_ref[1:2, :] + above * w_ref[2:3, :]
    o_ref[...] = (bg_ref[...].astype(F32) * y).astype(o_ref.dtype)


def _short_conv(lidx, z, w_conv, seq):
    m = z.shape[0]
    tm, tc = min(512, m), 512
    halo = 8
    assert seq % tm == 0 and B_WIDTH % tc == 0
    last_halo = m // halo - 1

    def tile(col):
        return pl.BlockSpec((tm, tc), lambda i, j, l: (i, col // tc + j))

    def prev(col):
        return pl.BlockSpec((halo, tc), lambda i, j, l: (jnp.maximum(i * (tm // halo) - 1, 0), col // tc + j))

    def nxt(col):
        return pl.BlockSpec((halo, tc),
                            lambda i, j, l: (jnp.minimum((i + 1) * (tm // halo), last_halo), col // tc + j))

    return pl.pallas_call(
        functools.partial(_conv_body, tm=tm, seq=seq),
        grid_spec=_grid_spec(
            (m // tm, B_WIDTH // tc),
            [tile(BG_COL), tile(CG_COL), tile(XB_COL), prev(CG_COL), prev(XB_COL), nxt(CG_COL), nxt(XB_COL),
             pl.BlockSpec((None, 3, tc), lambda i, j, l: (l[0], 0, j))],
            pl.BlockSpec((tm, tc), lambda i, j, l: (i, j))),
        out_shape=jax.ShapeDtypeStruct((m, B_WIDTH), BF),
        compiler_params=_cparams(2),
    )(lidx, z, z, z, z, z, z, z, w_conv)


def _dft1_body(l_ref, u_ref, g_ref, ar_ref, ai_ref, *, tn1, tk2, s2, groups):
    for jj in range(tn1):
        r = _dot(g_ref[jj], u_ref[jj])
        for g in range(groups):
            gc = slice(g * C_GROUP_DIM, (g + 1) * C_GROUP_DIM)
            for t in range(s2 // tk2):
                dst = slice(jj * tk2, (jj + 1) * tk2)
                ar_ref[g, t, dst, :] = r[t * tk2:(t + 1) * tk2, gc]
                ai_ref[g, t, dst, :] = r[s2 + t * tk2:s2 + (t + 1) * tk2, gc]


def _dft2_body(l_ref, ar_ref, ai_ref, m2_ref, cc_ref, sc_ref, y_ref, xr_ref, xi_ref,
               *, tk2, s1, groups, scale):
    for j in range(tk2):
        rows = pl.ds(j, s1, stride=tk2)
        a_re = jnp.concatenate([ar_ref[g, rows, :] for g in range(groups)], axis=1)
        a_im = jnp.concatenate([ai_ref[g, rows, :] for g in range(groups)], axis=1)
        x = _dot(m2_ref[...], jnp.concatenate([a_re, a_im], axis=0).astype(BF))
        for g in range(groups):
            gc = slice(g * C_GROUP_DIM, (g + 1) * C_GROUP_DIM)
            xr_ref[g, rows, :] = x[:s1, gc]
            xi_ref[g, rows, :] = x[s1:, gc]
    for g in range(groups):
        gc = slice(g * C_GROUP_DIM, (g + 1) * C_GROUP_DIM)
        yg = (_dot(xr_ref[g].astype(BF), cc_ref[...]) + _dot(xi_ref[g].astype(BF), sc_ref[...])) * scale
        for k1 in range(s1):
            y_ref[k1, :, gc] = yg[k1 * tk2:(k1 + 1) * tk2].astype(y_ref.dtype)


def _fourier_mix(lidx, u, tables):
    g1, m2, cc, sc = tables
    batch, s1, s2, width = u.shape
    groups = width // C_GROUP_DIM
    tn1, tk2 = 4, 16
    a_shape = jax.ShapeDtypeStruct((batch, groups, s2 // tk2, s1 * tk2, C_GROUP_DIM), F32)
    a_out = pl.BlockSpec((None, groups, s2 // tk2, tn1 * tk2, C_GROUP_DIM), lambda b, t, l: (b, 0, 0, t, 0))
    ar, ai = pl.pallas_call(
        functools.partial(_dft1_body, tn1=tn1, tk2=tk2, s2=s2, groups=groups),
        grid_spec=_grid_spec(
            (batch, s1 // tn1),
            [pl.BlockSpec((None, tn1, s2, width), lambda b, t, l: (b, t, 0, 0)),
             pl.BlockSpec((tn1, 2 * s2, s2), lambda b, t, l: (t, 0, 0))],
            [a_out, a_out]),
        out_shape=[a_shape, a_shape],
        compiler_params=_cparams(2),
    )(lidx, u, g1)
    a_in = pl.BlockSpec((None, groups, None, s1 * tk2, C_GROUP_DIM), lambda b, t, l: (b, 0, t, 0, 0))
    y = pl.pallas_call(
        functools.partial(_dft2_body, tk2=tk2, s1=s1, groups=groups,
                          scale=float((s1 * s2 * C_GROUP_DIM) ** -0.5)),
        grid_spec=_grid_spec(
            (batch, s2 // tk2),
            [a_in, a_in,
             pl.BlockSpec((2 * s1, 2 * s1), lambda b, t, l: (0, 0)),
             pl.BlockSpec((C_GROUP_DIM, C_GROUP_DIM), lambda b, t, l: (0, 0)),
             pl.BlockSpec((C_GROUP_DIM, C_GROUP_DIM), lambda b, t, l: (0, 0))],
            pl.BlockSpec((None, s1, tk2, width), lambda b, t, l: (b, 0, t, 0)),
            [pltpu.VMEM((groups, tk2 * s1, C_GROUP_DIM), F32)] * 2),
        out_shape=jax.ShapeDtypeStruct((batch, s1, s2, width), BF),
        compiler_params=_cparams(2),
    )(lidx, ar, ai, m2, cc, sc)
    return y.reshape(batch * s1 * s2, width)


def _dft_split(seq):
    s2 = 256 if seq >= 8192 else 128
    return seq // s2, s2


def _dft_tables(seq):
    s1, s2 = _dft_split(seq)
    two_pi = 2.0 * jnp.pi

    def cs(idx, period):
        ang = (idx % period).astype(F32) * (two_pi / period)
        return jnp.cos(ang), jnp.sin(ang)

    n1 = jnp.arange(s1, dtype=jnp.int32)[:, None, None]
    k2 = jnp.arange(s2, dtype=jnp.int32)[None, :, None]
    n2 = jnp.arange(s2, dtype=jnp.int32)[None, None, :]
    ca, sa = cs(k2 * (n1 + s1 * n2), seq)
    g1 = jnp.concatenate([ca, -sa], axis=1).astype(BF)
    a = jnp.arange(s1, dtype=jnp.int32)
    cb, sb = cs(a[:, None] * a[None, :], s1)
    m2 = jnp.concatenate([jnp.concatenate([cb, sb], axis=1),
                          jnp.concatenate([-sb, cb], axis=1)], axis=0).astype(BF)
    c = jnp.arange(C_GROUP_DIM, dtype=jnp.int32)
    cc, sc = cs(c[:, None] * c[None, :], C_GROUP_DIM)
    return g1, m2, cc.astype(BF), sc.astype(BF)


def _merge_body(l_ref, x_ref, g_ref, wgate_ref, bgate_ref, ya_ref, yd_ref, yb_ref, yc_ref,
                wpa_ref, wpd_ref, wpb_ref, wpc_ref, wo_ref, o_ref, h_ref, acc_ref):
    n = pl.program_id(1)

    @pl.when(n == 0)
    def _():
        h_ref[...] = _rms(x_ref[...], g_ref[...]).astype(BF)
        acc_ref[...] = jnp.zeros_like(acc_ref)

    h = h_ref[...]
    merged = None
    for i, (br_ref, wp_ref) in enumerate(((ya_ref, wpa_ref), (yd_ref, wpd_ref),
                                          (yb_ref, wpb_ref), (yc_ref, wpc_ref))):
        gate = jax.nn.sigmoid(_dot(h, wgate_ref[i]) + bgate_ref[i])
        term = gate * _dot(br_ref[...], wp_ref[...])
        merged = term if merged is None else merged + term
    acc_ref[...] += _dot(merged.astype(BF), wo_ref[...])

    @pl.when(n == pl.num_programs(1) - 1)
    def _():
        o_ref[...] = x_ref[...] + acc_ref[...]


def _merge(lidx, x, g, w_gate, b_gate, branches, projs, w_o):
    m, d = x.shape
    tm, tn = min(512, m), min(256, d)

    def row_tile(a):
        return pl.BlockSpec((tm, a.shape[1]), lambda i, n, l: (i, 0))

    def proj_tile(w):
        return pl.BlockSpec((None, w.shape[1], tn), lambda i, n, l: (l[0], 0, n))

    return pl.pallas_call(
        _merge_body,
        grid_spec=_grid_spec(
            (m // tm, d // tn),
            [row_tile(x),
             pl.BlockSpec((None, 1, d), lambda i, n, l: (l[0], 0, 0)),
             pl.BlockSpec((None, 4, d, tn), lambda i, n, l: (l[0], 0, 0, n)),
             pl.BlockSpec((None, 4, 1, tn), lambda i, n, l: (l[0], 0, 0, n))]
            + [row_tile(b) for b in branches] + [proj_tile(w) for w in projs]
            + [pl.BlockSpec((None, tn, d), lambda i, n, l: (l[0], n, 0))],
            row_tile(x),
            [pltpu.VMEM((tm, d), BF), pltpu.VMEM((tm, d), F32)]),
        out_shape=jax.ShapeDtypeStruct((m, d), F32),
        compiler_params=_cparams(2),
    )(lidx, x, g, w_gate, b_gate, *branches, *projs, w_o)


def _mem_body(l_ref, x_ref, g_ref, wq_ref, kv_ref, wo_ref, o_ref, att_ref):
    x = x_ref[...]
    h = _rms(x, g_ref[...]).astype(BF)
    q = (_dot(h, wq_ref[...]) * HEAD_DIM ** -0.5).astype(BF)
    mw = M_HEADS * HEAD_DIM
    for hh in range(M_HEADS):
        cols = slice(hh * HEAD_DIM, (hh + 1) * HEAD_DIM)
        k = kv_ref[:, cols]
        v = kv_ref[:, mw + hh * HEAD_DIM:mw + (hh + 1) * HEAD_DIM]
        s = _dot_t(q[:, cols], k)
        p = jnp.exp(s - jnp.max(s, axis=-1, keepdims=True))
        o = _dot(p.astype(BF), v) / jnp.sum(p, axis=-1, keepdims=True)
        att_ref[:, cols] = o.astype(BF)
    o_ref[...] = x + _dot(att_ref[...], wo_ref[...])


def _memory_attention(lidx, x, g, w_mq, kv, w_mo, seq):
    m, d = x.shape
    tm = min(512, m)
    n_mem, kvw = kv.shape[1], kv.shape[2]
    mw = w_mq.shape[-1]
    assert seq % tm == 0
    return pl.pallas_call(
        _mem_body,
        grid_spec=_grid_spec(
            (m // tm,),
            [pl.BlockSpec((tm, d), lambda i, l: (i, 0)),
             pl.BlockSpec((None, 1, d), lambda i, l: (l[0], 0, 0)),
             pl.BlockSpec((None, d, mw), lambda i, l: (l[0], 0, 0)),
             pl.BlockSpec((None, n_mem, kvw), lambda i, l: ((i * tm) // seq, 0, 0)),
             pl.BlockSpec((None, mw, d), lambda i, l: (l[0], 0, 0))],
            pl.BlockSpec((tm, d), lambda i, l: (i, 0)),
            [pltpu.VMEM((tm, mw), BF)]),
        out_shape=jax.ShapeDtypeStruct((m, d), F32),
        compiler_params=_cparams(1),
    )(lidx, x, g, w_mq, kv, w_mo)


def _final_norm_body(l_ref, x_ref, g_ref, o_ref):
    o_ref[...] = _rms(x_ref[...], g_ref[...])


def _final_norm(lidx, x, g):
    m, d = x.shape
    tm = min(512, m)
    return pl.pallas_call(
        _final_norm_body,
        grid_spec=_grid_spec((m // tm,),
                             [pl.BlockSpec((tm, d), lambda i, l: (i, 0)),
                              pl.BlockSpec((1, d), lambda i, l: (0, 0))],
                             pl.BlockSpec((tm, d), lambda i, l: (i, 0))),
        out_shape=jax.ShapeDtypeStruct((m, d), F32),
        compiler_params=_cparams(1),
    )(lidx, x, g)


def _rope_tables(seq):
    half = HEAD_DIM // 2
    inv = ROPE_THETA ** (-jnp.arange(half, dtype=F32) / half)
    ang = jnp.arange(seq, dtype=jnp.int32).astype(F32)[:, None] * inv[None, :]
    cos, sin = jnp.cos(ang), jnp.sin(ang)
    return jnp.concatenate([cos, cos], axis=1), jnp.concatenate([-sin, sin], axis=1)


def _layer(lidx, x, mem, p, batch, seq, tabs):
    cos_t, sin_t, dft = tabs
    x = _ffn(lidx, x, p["g_ffn1"], p["w_ffn1_gate"], p["w_ffn1_up"], p["w_ffn1_down"])

    s1, _ = _dft_split(seq)
    z, zg1, zg2, uc = _proj_in(lidx, x, p["g_mix"], p["w_in"], cos_t, sin_t, batch, seq, s1)
    z4 = z.reshape(batch, 1, seq, Z_WIDTH)
    outs, lses = [], []
    for src, cols in ((z4, (QA0_COL, KA0_COL, VA0_COL)),
                      (zg1, (0, GROUP_WIDTH, 2 * GROUP_WIDTH)),
                      (zg2, (0, GROUP_WIDTH, 2 * GROUP_WIDTH))):
        o, lse = _band_attention(lidx, src, *cols, A_HEADS_PER_GROUP, 1, A_RADIUS, want_lse=True)
        outs.append(o)
        lses.append(lse)
    ya = _group_merge(lidx, outs, lses, seq)
    (yd,) = _band_attention(lidx, z4, QD_COL, KD_COL, VD_COL, D_Q_HEADS,
                            D_Q_HEADS // D_KV_HEADS, D_RADIUS, sink=p["sink"])
    yd = yd.reshape(batch * seq, D_Q_HEADS * HEAD_DIM)
    yb = _short_conv(lidx, z, p["w_conv"], seq)
    yc = _fourier_mix(lidx, uc, dft)
    x = _merge(lidx, x, p["g_mix"], p["w_gate"], p["b_gate"], (ya, yd, yb, yc),
               (p["w_pa"], p["w_pd"], p["w_pb"], p["w_pc"]), p["w_o"])

    kv = _proj(lidx, mem, p["g_memkv"], p["w_mkv"])
    kv = kv.reshape(batch, mem.shape[0] // batch, kv.shape[1])
    x = _memory_attention(lidx, x, p["g_memq"], p["w_mq"], kv, p["w_mo"], seq)

    x = _ffn(lidx, x, p["g_ffn2"], p["w_ffn2_gate"], p["w_ffn2_up"], p["w_ffn2_down"])
    return x


def _forward(x_prompt, x_sample, mem_prompt, mem_sample, p, g_final):
    depth = p["w_o"].shape[0]
    d = x_prompt.shape[-1]
    trunks = []
    for x, mem in ((x_prompt, mem_prompt), (x_sample, mem_sample)):
        batch, seq = x.shape[0], x.shape[1]
        tabs = _rope_tables(seq) + (_dft_tables(seq),)
        trunks.append((batch, seq, tabs, mem.reshape(-1, d)))

    def body(xs, l):
        lidx = l.reshape(1)
        return tuple(_layer(lidx, x, mem, p, batch, seq, tabs)
                     for x, (batch, seq, tabs, mem) in zip(xs, trunks)), None

    xs = (x_prompt.reshape(-1, d), x_sample.reshape(-1, d))
    xs, _ = lax.scan(body, xs, jnp.arange(depth, dtype=jnp.int32))
    zero = jnp.zeros((1,), jnp.int32)
    gf = g_final.reshape(1, d)
    return tuple(_final_norm(zero, x, gf).reshape(orig.shape)
                 for x, orig in zip(xs, (x_prompt, x_sample)))


def kernel(x_prompt, x_sample, mem_prompt, mem_sample, g_ffn1, w_ffn1_gate, w_ffn1_up, w_ffn1_down, g_mix, w_in, w_conv, sink, w_gate, b_gate, w_pa, w_pd, w_pb, w_pc, w_o, g_memq, g_memkv, w_mq, w_mkv, w_mo, g_ffn2, w_ffn2_gate, w_ffn2_up, w_ffn2_down, g_final):
    depth, d = g_ffn1.shape

    def gain(g):
        return g.reshape(depth, 1, d)

    def cols(lo, hi):
        return w_in[..., lo:hi]

    def head_group(base, gi):
        return cols(base + gi * GROUP_WIDTH, base + (gi + 1) * GROUP_WIDTH)

    qa, ka, va = 0, A_WIDTH, 2 * A_WIDTH
    w_in_perm = jnp.concatenate(
        [head_group(qa, 0), head_group(ka, 0), cols(4608, 6144), head_group(va, 0), cols(6144, 9216),
         head_group(qa, 1), head_group(ka, 1), head_group(va, 1),
         head_group(qa, 2), head_group(ka, 2), head_group(va, 2), cols(9216, 10240)], axis=-1).astype(BF)
    p = {
        "g_ffn1": gain(g_ffn1), "w_ffn1_gate": w_ffn1_gate.astype(BF), "w_ffn1_up": w_ffn1_up.astype(BF),
        "w_ffn1_down": w_ffn1_down.astype(BF),
        "g_mix": gain(g_mix), "w_in": w_in_perm,
        "w_conv": w_conv, "sink": sink,
        "w_gate": w_gate.astype(BF), "b_gate": b_gate.reshape(depth, 4, 1, d),
        "w_pa": w_pa.astype(BF), "w_pd": w_pd.astype(BF), "w_pb": w_pb.astype(BF), "w_pc": w_pc.astype(BF),
        "w_o": w_o.astype(BF),
        "g_memq": gain(g_memq), "g_memkv": gain(g_memkv), "w_mq": w_mq.astype(BF),
        "w_mkv": w_mkv.astype(BF), "w_mo": w_mo.astype(BF),
        "g_ffn2": gain(g_ffn2), "w_ffn2_gate": w_ffn2_gate.astype(BF), "w_ffn2_up": w_ffn2_up.astype(BF),
        "w_ffn2_down": w_ffn2_down.astype(BF),
    }
    return _forward(x_prompt, x_sample, mem_prompt, mem_sample, p, g_final)
```

```python
import functools

import jax
import jax.numpy as jnp
from jax import lax
from jax.experimental import pallas as pl
from jax.experimental.pallas import tpu as pltpu

BF = jnp.bfloat16
F32 = jnp.float32

HEAD_DIM = 128
EPS = 1e-6
ROPE_THETA = 10000.0
A_DILATIONS = (1, 4, 16)
A_RADIUS = 64
A_HEADS_PER_GROUP = 4
A_WIDTH = 1536
D_Q_HEADS = 8
D_KV_HEADS = 2
D_RADIUS = 128
B_WIDTH = 1024
C_WIDTH = 1024
C_GROUP_DIM = 128
M_HEADS = 4

QA0_COL, KA0_COL, QD_COL, KD_COL, VD_COL, VA0_COL, BG_COL, CG_COL, XB_COL, Z_WIDTH = (
    0, 512, 1024, 2048, 2304, 2560, 3072, 4096, 5120, 6144)
PROJ_TILE = 512
GROUP_WIDTH = 512
PLAIN, ROT_Q, ROT_K, ROT_HALF = 0, 1, 2, 3
TILE_KINDS = ((ROT_Q, ROT_K, ROT_Q, ROT_Q, ROT_HALF) + (PLAIN,) * 7
              + (ROT_Q, ROT_K, PLAIN) * 2
              + (PLAIN,) * 2)
Z_TILES, G_TILES, U_TILES = 12, 3, 2

ROW_CHUNK = 128
NEG_BIG = -1e30
VMEM_LIMIT_BYTES = 60 * 1024 * 1024
FFN_ROWS = 1024
PROJ_ROWS = 1024
MERGE_COLS = 256


def _cparams(ndim):
    return pltpu.CompilerParams(dimension_semantics=("arbitrary",) * ndim,
                                vmem_limit_bytes=VMEM_LIMIT_BYTES)


def _grid_spec(grid, in_specs, out_specs, scratch=()):
    return pltpu.PrefetchScalarGridSpec(num_scalar_prefetch=1, grid=grid, in_specs=in_specs,
                                        out_specs=out_specs, scratch_shapes=list(scratch))


def _rms(x, g):
    return x * lax.rsqrt(jnp.mean(x * x, axis=-1, keepdims=True) + EPS) * g


def _dot(a, b):
    return jnp.dot(a, b, preferred_element_type=F32)


def _dot_t(a, b):
    return lax.dot_general(a, b, (((1,), (1,)), ((), ())), preferred_element_type=F32)


def _ffn_body(l_ref, x_ref, g_ref, wg_ref, wu_ref, wd_ref, o_ref, h_ref):
    j = pl.program_id(1)

    @pl.when(j == 0)
    def _():
        x = x_ref[...]
        h_ref[...] = _rms(x, g_ref[...]).astype(BF)
        o_ref[...] = x

    h = h_ref[...]
    gate = _dot(h, wg_ref[...])
    up = _dot(h, wu_ref[...])
    a = (0.5 * gate * jax.nn.sigmoid(gate) * up).astype(BF)
    o_ref[...] += _dot(a, wd_ref[...])


def _ffn(lidx, x, g, wg, wu, wd):
    m, d = x.shape
    f = wg.shape[-1]
    tm, tf = min(FFN_ROWS, m), min(512, f)
    return pl.pallas_call(
        _ffn_body,
        grid_spec=_grid_spec(
            (m // tm, f // tf),
            [pl.BlockSpec((tm, d), lambda i, j, l: (i, 0)),
             pl.BlockSpec((None, 1, d), lambda i, j, l: (l[0], 0, 0)),
             pl.BlockSpec((None, d, tf), lambda i, j, l: (l[0], 0, j)),
             pl.BlockSpec((None, d, tf), lambda i, j, l: (l[0], 0, j)),
             pl.BlockSpec((None, tf, d), lambda i, j, l: (l[0], j, 0))],
            pl.BlockSpec((tm, d), lambda i, j, l: (i, 0)),
            [pltpu.VMEM((tm, d), BF)]),
        out_shape=jax.ShapeDtypeStruct((m, d), F32),
        compiler_params=_cparams(2),
    )(lidx, x, g, wg, wu, wd)


def _proj_body(l_ref, x_ref, g_ref, w_ref, o_ref, h_ref):
    @pl.when(pl.program_id(1) == 0)
    def _():
        h_ref[...] = _rms(x_ref[...], g_ref[...]).astype(BF)

    o_ref[...] = _dot(h_ref[...], w_ref[...]).astype(o_ref.dtype)


def _proj_in_body(l_ref, kind_ref, x_ref, g_ref, w_ref, cos_ref, sin_ref,
                  z_ref, g1_ref, g2_ref, uc_ref, h_ref, res_ref, *, tm, s1):
    j = pl.program_id(1)

    @pl.when(j == 0)
    def _():
        h_ref[...] = _rms(x_ref[...], g_ref[...]).astype(BF)

    acc = _dot(h_ref[...], w_ref[...])
    kind = kind_ref[j]
    chunks = PROJ_TILE // HEAD_DIM
    half_chunks = (VD_COL - KD_COL) // HEAD_DIM

    def rot(xc):
        return xc * cos_ref[...] + pltpu.roll(xc, HEAD_DIM // 2, 1) * sin_ref[...]

    def lanes(c):
        return slice(c * HEAD_DIM, (c + 1) * HEAD_DIM)

    g1_0, g2_0, u_0 = Z_TILES, Z_TILES + G_TILES, Z_TILES + 2 * G_TILES
    n_rot = {PLAIN: 0, ROT_Q: chunks, ROT_K: chunks, ROT_HALF: half_chunks}

    def finished(c, tile_kind):
        xc = acc[:, lanes(c)]
        if c >= n_rot[tile_kind]:
            return xc
        return rot(xc) * HEAD_DIM ** -0.5 if tile_kind == ROT_Q else rot(xc)

    for tile_kind in (PLAIN, ROT_Q, ROT_K, ROT_HALF):
        @pl.when((kind == tile_kind) & (j < g1_0))
        def _(tile_kind=tile_kind):
            for c in range(chunks):
                z_ref[:, lanes(c)] = finished(c, tile_kind).astype(BF)

        if tile_kind != ROT_HALF:
            @pl.when((kind == tile_kind) & (j >= g1_0))
            def _(tile_kind=tile_kind):
                for c in range(chunks):
                    res_ref[c] = finished(c, tile_kind)

    def store_residue_major(dst_ref, dil):
        rows = tm // dil
        for r in range(dil):
            for c in range(chunks):
                dst_ref[r, :, lanes(c)] = res_ref[c, pl.ds(r, rows, stride=dil), :].astype(BF)

    @pl.when((j >= g1_0) & (j < g2_0))
    def _():
        store_residue_major(g1_ref, A_DILATIONS[1])

    @pl.when((j >= g2_0) & (j < u_0))
    def _():
        store_residue_major(g2_ref, A_DILATIONS[2])

    @pl.when(j >= u_0)
    def _():
        store_residue_major(uc_ref, s1)


def _proj(lidx, x, g, w, out_dtype=BF, tn=512):
    m, d = x.shape
    n = w.shape[-1]
    tm, tn = min(512, m), min(tn, n)
    return pl.pallas_call(
        _proj_body,
        grid_spec=_grid_spec(
            (m // tm, n // tn),
            [pl.BlockSpec((tm, d), lambda i, j, l: (i, 0)),
             pl.BlockSpec((None, 1, d), lambda i, j, l: (l[0], 0, 0)),
             pl.BlockSpec((None, d, tn), lambda i, j, l: (l[0], 0, j))],
            pl.BlockSpec((tm, tn), lambda i, j, l: (i, j)),
            [pltpu.VMEM((tm, d), BF)]),
        out_shape=jax.ShapeDtypeStruct((m, n), out_dtype),
        compiler_params=_cparams(2),
    )(lidx, x, g, w)


def _proj_in(lidx, x, g, w, cos_t, sin_t, batch, seq, s1):
    m, d = x.shape
    tm, tn = min(PROJ_ROWS, m), PROJ_TILE
    n_tiles = len(TILE_KINDS)
    d1, d2 = A_DILATIONS[1], A_DILATIONS[2]
    assert w.shape[-1] == n_tiles * tn and seq % tm == 0 and tm % (16 * max(d2, s1)) == 0
    nsb = seq // tm
    g1_0, g2_0, u_0 = Z_TILES, Z_TILES + G_TILES, Z_TILES + 2 * G_TILES

    def residue_major(dil, first_tile, tiles):
        return pl.BlockSpec((None, dil, tm // dil, tn),
                            lambda i, j, l, k: (i // nsb, 0, i % nsb, jnp.clip(j - first_tile, 0, tiles - 1)))

    return pl.pallas_call(
        functools.partial(_proj_in_body, tm=tm, s1=s1),
        grid_spec=pltpu.PrefetchScalarGridSpec(
            num_scalar_prefetch=2,
            grid=(m // tm, n_tiles),
            in_specs=[pl.BlockSpec((tm, d), lambda i, j, l, k: (i, 0)),
                      pl.BlockSpec((None, 1, d), lambda i, j, l, k: (l[0], 0, 0)),
                      pl.BlockSpec((None, d, tn), lambda i, j, l, k: (l[0], 0, j)),
                      pl.BlockSpec((tm, HEAD_DIM), lambda i, j, l, k: (i % nsb, 0)),
                      pl.BlockSpec((tm, HEAD_DIM), lambda i, j, l, k: (i % nsb, 0))],
            out_specs=[pl.BlockSpec((tm, tn), lambda i, j, l, k: (i, jnp.minimum(j, Z_TILES - 1))),
                       residue_major(d1, g1_0, G_TILES),
                       residue_major(d2, g2_0, G_TILES),
                       residue_major(s1, u_0, U_TILES)],
            scratch_shapes=[pltpu.VMEM((tm, d), BF), pltpu.VMEM((tn // HEAD_DIM, tm, HEAD_DIM), F32)]),
        out_shape=[jax.ShapeDtypeStruct((m, Z_WIDTH), BF),
                   jax.ShapeDtypeStruct((batch, d1, seq // d1, G_TILES * tn), BF),
                   jax.ShapeDtypeStruct((batch, d2, seq // d2, G_TILES * tn), BF),
                   jax.ShapeDtypeStruct((batch, s1, seq // s1, U_TILES * tn), BF)],
        compiler_params=_cparams(2),
    )(lidx, jnp.asarray(TILE_KINDS, jnp.int32), x, g, w, cos_t, sin_t)


def _band_attn_body(l_ref, q_ref, kp_ref, kc_ref, kn_ref, vp_ref, vc_ref, vn_ref, *rest,
                    tq, length, radius, n_heads, rep, has_sink, want_lse):
    rest = list(rest)
    sink_ref = rest.pop(0) if has_sink else None
    o_ref = rest.pop(0)
    lse_ref = rest.pop(0) if want_lse else None
    i = pl.program_id(2)
    rc = ROW_CHUNK
    nc = tq // rc
    win = rc + 2 * radius
    groups = n_heads // rep
    assert radius <= rc and radius % 16 == 0

    def window(prev_ref, cur_ref, next_ref, c, cols):
        lo, hi = c * rc - radius, (c + 1) * rc + radius
        parts = []
        if lo < 0:
            parts.append(prev_ref[rc - radius:rc, cols])
        parts.append(cur_ref[max(lo, 0):min(hi, tq), cols])
        if hi > tq:
            parts.append(next_ref[0:radius, cols])
        return parts[0] if len(parts) == 1 else jnp.concatenate(parts, axis=0)

    row = lax.broadcasted_iota(jnp.int32, (n_heads * rc, win), 0) & (rc - 1)
    col = lax.broadcasted_iota(jnp.int32, (n_heads * rc, win), 1)
    in_band = jnp.abs(col - radius - row) <= radius
    if has_sink:
        sink_col = jnp.concatenate(
            [jnp.full((rc, 1), sink_ref[l_ref[0], h], F32) for h in range(n_heads)], axis=0)
    for c in range(nc):
        kpos = i * tq + c * rc - radius + col
        mask = in_band & (kpos >= 0) & (kpos < length)
        rows = slice(c * rc, (c + 1) * rc)
        scores = []
        for g in range(groups):
            k = window(kp_ref, kc_ref, kn_ref, c, slice(g * HEAD_DIM, (g + 1) * HEAD_DIM))
            qcols = slice(g * rep * HEAD_DIM, (g + 1) * rep * HEAD_DIM)
            q = q_ref[rows, qcols]
            if rep > 1:
                q = jnp.concatenate([q[:, r * HEAD_DIM:(r + 1) * HEAD_DIM] for r in range(rep)], axis=0)
            scores.append(_dot_t(q, k))
        s = jnp.where(mask, jnp.concatenate(scores, axis=0), NEG_BIG)
        m = jnp.max(s, axis=-1, keepdims=True)
        if has_sink:
            m = jnp.maximum(m, sink_col)
        p = jnp.exp(s - m)
        den = jnp.sum(p, axis=-1, keepdims=True)
        if has_sink:
            den = den + jnp.exp(sink_col - m)
        p = p.astype(BF)
        inv_den = 1.0 / den
        lse = m + jnp.log(den) if want_lse else None
        for g in range(groups):
            v = window(vp_ref, vc_ref, vn_ref, c, slice(g * HEAD_DIM, (g + 1) * HEAD_DIM))
            grows = slice(g * rep * rc, (g + 1) * rep * rc)
            o = _dot(p[grows], v) * inv_den[grows]
            for r in range(rep):
                h = g * rep + r
                hcols = slice(h * HEAD_DIM, (h + 1) * HEAD_DIM)
                o_ref[rows, hcols] = o[r * rc:(r + 1) * rc].astype(o_ref.dtype)
                if want_lse:
                    lse_ref[rows, hcols] = jnp.broadcast_to(lse[h * rc:(h + 1) * rc], (rc, HEAD_DIM))


def _band_attention(lidx, src, q_col, k_col, v_col, n_heads, rep, radius, sink=None, want_lse=False):
    batch, dil, length, width = src.shape
    tq = min(512, length)
    nc = tq // ROW_CHUNK
    qw, kw = n_heads * HEAD_DIM, (n_heads // rep) * HEAD_DIM
    assert length % tq == 0 and tq % ROW_CHUNK == 0
    assert q_col % qw == 0 and k_col % kw == 0 and v_col % kw == 0
    last_chunk = length // ROW_CHUNK - 1

    def kv_specs(col):
        cb = col // kw
        return [
            pl.BlockSpec((None, None, ROW_CHUNK, kw),
                         lambda b, r, i, l: (b, r, jnp.maximum(i * nc - 1, 0), cb)),
            pl.BlockSpec((None, None, tq, kw), lambda b, r, i, l: (b, r, i, cb)),
            pl.BlockSpec((None, None, ROW_CHUNK, kw),
                         lambda b, r, i, l: (b, r, jnp.minimum((i + 1) * nc, last_chunk), cb)),
        ]

    in_specs = [pl.BlockSpec((None, None, tq, qw), lambda b, r, i, l: (b, r, i, q_col // qw))]
    in_specs += kv_specs(k_col) + kv_specs(v_col)
    args = [lidx] + [src] * 7
    if sink is not None:
        in_specs.append(pl.BlockSpec(memory_space=pltpu.SMEM))
        args.append(sink)
    out_spec = pl.BlockSpec((None, None, tq, qw), lambda b, r, i, l: (b, r, i, 0))
    out_shape = [jax.ShapeDtypeStruct((batch, dil, length, qw), BF)]
    out_specs = [out_spec]
    if want_lse:
        out_shape.append(jax.ShapeDtypeStruct((batch, dil, length, qw), F32))
        out_specs.append(out_spec)
    return pl.pallas_call(
        functools.partial(_band_attn_body, tq=tq, length=length, radius=radius, n_heads=n_heads,
                          rep=rep, has_sink=sink is not None, want_lse=want_lse),
        grid_spec=_grid_spec((batch, dil, length // tq), in_specs, out_specs),
        out_shape=out_shape,
        compiler_params=_cparams(3),
    )(*args)


def _group_merge_body(l_ref, o0_ref, l0_ref, o1_ref, l1_ref, o2_ref, l2_ref, y_ref, nat_ref, *, tm):
    heads = A_HEADS_PER_GROUP
    for slot, (src_ref, dil) in enumerate(((o1_ref, A_DILATIONS[1]), (l1_ref, A_DILATIONS[1]),
                                           (o2_ref, A_DILATIONS[2]), (l2_ref, A_DILATIONS[2]))):
        rows = tm // dil
        for r in range(dil):
            for h in range(heads):
                nat_ref[slot * heads + h, pl.ds(r, rows, stride=dil), :] = (
                    src_ref[r, :, h * HEAD_DIM:(h + 1) * HEAD_DIM].astype(F32))
    for h in range(heads):
        hc = slice(h * HEAD_DIM, (h + 1) * HEAD_DIM)
        o1, l1, o2, l2 = (nat_ref[slot * heads + h] for slot in range(4))
        l0 = l0_ref[:, hc]
        m = jnp.maximum(jnp.maximum(l0, l1), l2)
        e0, e1, e2 = jnp.exp(l0 - m), jnp.exp(l1 - m), jnp.exp(l2 - m)
        num = e0 * o0_ref[:, hc].astype(F32) + e1 * o1 + e2 * o2
        y_ref[:, hc] = (num / (e0 + e1 + e2)).astype(y_ref.dtype)


def _group_merge(lidx, outs, lses, seq):
    batch, _, _, w = outs[0].shape
    m = batch * seq
    tm = min(512, seq)
    nsb = seq // tm
    assert tm % (16 * A_DILATIONS[2]) == 0

    def spec(dil):
        return pl.BlockSpec((None, dil, tm // dil, w), lambda i, l: (i // nsb, 0, i % nsb, 0))

    args = []
    in_specs = []
    for o, lse, dil in zip(outs, lses, A_DILATIONS):
        args += [o, lse]
        in_specs += [spec(dil)] * 2
    in_specs[0] = in_specs[1] = pl.BlockSpec((None, None, tm, w), lambda i, l: (i // nsb, 0, i % nsb, 0))
    return pl.pallas_call(
        functools.partial(_group_merge_body, tm=tm),
        grid_spec=_grid_spec((m // tm,), in_specs, pl.BlockSpec((tm, w), lambda i, l: (i, 0)),
                             [pltpu.VMEM((4 * A_HEADS_PER_GROUP, tm, HEAD_DIM), F32)]),
        out_shape=jax.ShapeDtypeStruct((m, w), BF),
        compiler_params=_cparams(1),
    )(lidx, *args)


def _conv_body(l_ref, bg_ref, cg_ref, xb_ref, cgp_ref, xbp_ref, cgn_ref, xbn_ref, w_ref, o_ref,
               *, tm, seq):
    i = pl.program_id(0)
    u = cg_ref[...].astype(F32) * xb_ref[...].astype(F32)
    not_first = ((i * tm) % seq != 0).astype(F32)
    not_last = (((i + 1) * tm) % seq != 0).astype(F32)
    u_prev = cgp_ref[7:8, :].astype(F32) * xbp_ref[7:8, :].astype(F32) * not_first
    u_next = cgn_ref[0:1, :].astype(F32) * xbn_ref[0:1, :].astype(F32) * not_last
    row = lax.broadcasted_iota(jnp.int32, u.shape, 0)
    below = jnp.where(row == 0, u_prev, pltpu.roll(u, 1, 0))
    above = jnp.where(row == tm - 1, u_next, pltpu.roll(u, tm - 1, 0))
    y = below * w_ref[0:1, :] + u * w_ref[1:2, :] + above * w_ref[2:3, :]
    o_ref[...] = (bg_ref[...].astype(F32) * y).astype(o_ref.dtype)


def _short_conv(lidx, z, w_conv, seq):
    m = z.shape[0]
    tm, tc = min(512, m), 512
    halo = 8
    assert seq % tm == 0 and B_WIDTH % tc == 0
    last_halo = m // halo - 1

    def tile(col):
        return pl.BlockSpec((tm, tc), lambda i, j, l: (i, col // tc + j))

    def prev(col):
        return pl.BlockSpec((halo, tc), lambda i, j, l: (jnp.maximum(i * (tm // halo) - 1, 0), col // tc + j))

    def nxt(col):
        return pl.BlockSpec((halo, tc),
                            lambda i, j, l: (jnp.minimum((i + 1) * (tm // halo), last_halo), col // tc + j))

    return pl.pallas_call(
        functools.partial(_conv_body, tm=tm, seq=seq),
        grid_spec=_grid_spec(
            (m // tm, B_WIDTH // tc),
            [tile(BG_COL), tile(CG_COL), tile(XB_COL), prev(CG_COL), prev(XB_COL), nxt(CG_COL), nxt(XB_COL),
             pl.BlockSpec((None, 3, tc), lambda i, j, l: (l[0], 0, j))],
            pl.BlockSpec((tm, tc), lambda i, j, l: (i, j))),
        out_shape=jax.ShapeDtypeStruct((m, B_WIDTH), BF),
        compiler_params=_cparams(2),
    )(lidx, z, z, z, z, z, z, z, w_conv)


def _dft1_body(l_ref, u_ref, g_ref, ar_ref, ai_ref, *, tn1, tk2, s2, groups):
    for jj in range(tn1):
        r = _dot(g_ref[jj], u_ref[jj])
        for g in range(groups):
            gc = slice(g * C_GROUP_DIM, (g + 1) * C_GROUP_DIM)
            for t in range(s2 // tk2):
                dst = slice(jj * tk2, (jj + 1) * tk2)
                ar_ref[g, t, dst, :] = r[t * tk2:(t + 1) * tk2, gc]
                ai_ref[g, t, dst, :] = r[s2 + t * tk2:s2 + (t + 1) * tk2, gc]


def _dft2_body(l_ref, ar_ref, ai_ref, m2_ref, cc_ref, sc_ref, y_ref, xr_ref, xi_ref,
               *, tk2, s1, groups, scale):
    for j in range(tk2):
        rows = pl.ds(j, s1, stride=tk2)
        a_re = jnp.concatenate([ar_ref[g, rows, :] for g in range(groups)], axis=1)
        a_im = jnp.concatenate([ai_ref[g, rows, :] for g in range(groups)], axis=1)
        x = _dot(m2_ref[...], jnp.concatenate([a_re, a_im], axis=0).astype(BF))
        for g in range(groups):
            gc = slice(g * C_GROUP_DIM, (g + 1) * C_GROUP_DIM)
            xr_ref[g, rows, :] = x[:s1, gc]
            xi_ref[g, rows, :] = x[s1:, gc]
    for g in range(groups):
        gc = slice(g * C_GROUP_DIM, (g + 1) * C_GROUP_DIM)
        yg = (_dot(xr_ref[g].astype(BF), cc_ref[...]) + _dot(xi_ref[g].astype(BF), sc_ref[...])) * scale
        for k1 in range(s1):
            y_ref[k1, :, gc] = yg[k1 * tk2:(k1 + 1) * tk2].astype(y_ref.dtype)


def _fourier_mix(lidx, u, tables):
    g1, m2, cc, sc = tables
    batch, s1, s2, width = u.shape
    groups = width // C_GROUP_DIM
    tn1, tk2 = 4, 16
    a_shape = jax.ShapeDtypeStruct((batch, groups, s2 // tk2, s1 * tk2, C_GROUP_DIM), F32)
    a_out = pl.BlockSpec((None, groups, s2 // tk2, tn1 * tk2, C_GROUP_DIM), lambda b, t, l: (b, 0, 0, t, 0))
    ar, ai = pl.pallas_call(
        functools.partial(_dft1_body, tn1=tn1, tk2=tk2, s2=s2, groups=groups),
        grid_spec=_grid_spec(
            (batch, s1 // tn1),
            [pl.BlockSpec((None, tn1, s2, width), lambda b, t, l: (b, t, 0, 0)),
             pl.BlockSpec((tn1, 2 * s2, s2), lambda b, t, l: (t, 0, 0))],
            [a_out, a_out]),
        out_shape=[a_shape, a_shape],
        compiler_params=_cparams(2),
    )(lidx, u, g1)
    a_in = pl.BlockSpec((None, groups, None, s1 * tk2, C_GROUP_DIM), lambda b, t, l: (b, 0, t, 0, 0))
    y = pl.pallas_call(
        functools.partial(_dft2_body, tk2=tk2, s1=s1, groups=groups,
                          scale=float((s1 * s2 * C_GROUP_DIM) ** -0.5)),
        grid_spec=_grid_spec(
            (batch, s2 // tk2),
            [a_in, a_in,
             pl.BlockSpec((2 * s1, 2 * s1), lambda b, t, l: (0, 0)),
             pl.BlockSpec((C_GROUP_DIM, C_GROUP_DIM), lambda b, t, l: (0, 0)),
             pl.BlockSpec((C_GROUP_DIM, C_GROUP_DIM), lambda b, t, l: (0, 0))],
            pl.BlockSpec((None, s1, tk2, width), lambda b, t, l: (b, 0, t, 0)),
            [pltpu.VMEM((groups, tk2 * s1, C_GROUP_DIM), F32)] * 2),
        out_shape=jax.ShapeDtypeStruct((batch, s1, s2, width), BF),
        compiler_params=_cparams(2),
    )(lidx, ar, ai, m2, cc, sc)
    return y.reshape(batch * s1 * s2, width)


def _dft_split(seq):
    s2 = 256 if seq >= 8192 else 128
    return seq // s2, s2


def _dft_tables(seq):
    s1, s2 = _dft_split(seq)
    two_pi = 2.0 * jnp.pi

    def cs(idx, period):
        ang = (idx % period).astype(F32) * (two_pi / period)
        return jnp.cos(ang), jnp.sin(ang)

    n1 = jnp.arange(s1, dtype=jnp.int32)[:, None, None]
    k2 = jnp.arange(s2, dtype=jnp.int32)[None, :, None]
    n2 = jnp.arange(s2, dtype=jnp.int32)[None, None, :]
    ca, sa = cs(k2 * (n1 + s1 * n2), seq)
    g1 = jnp.concatenate([ca, -sa], axis=1).astype(BF)
    a = jnp.arange(s1, dtype=jnp.int32)
    cb, sb = cs(a[:, None] * a[None, :], s1)
    m2 = jnp.concatenate([jnp.concatenate([cb, sb], axis=1),
                          jnp.concatenate([-sb, cb], axis=1)], axis=0).astype(BF)
    c = jnp.arange(C_GROUP_DIM, dtype=jnp.int32)
    cc, sc = cs(c[:, None] * c[None, :], C_GROUP_DIM)
    return g1, m2, cc.astype(BF), sc.astype(BF)


def _memory_update(x, g_ref, wq_ref, kv_ref, wo_ref, att_ref):
    h = _rms(x, g_ref[...]).astype(BF)
    q = (_dot(h, wq_ref[...]) * HEAD_DIM ** -0.5).astype(BF)
    mw = M_HEADS * HEAD_DIM
    for hh in range(M_HEADS):
        cols = slice(hh * HEAD_DIM, (hh + 1) * HEAD_DIM)
        k = kv_ref[:, cols]
        v = kv_ref[:, mw + hh * HEAD_DIM:mw + (hh + 1) * HEAD_DIM]
        s = _dot_t(q[:, cols], k)
        p = jnp.exp(s - jnp.max(s, axis=-1, keepdims=True))
        o = _dot(p.astype(BF), v) / jnp.sum(p, axis=-1, keepdims=True)
        att_ref[:, cols] = o.astype(BF)
    return x + _dot(att_ref[...], wo_ref[...])


def _merge_body(l_ref, x_ref, g_ref, wgate_ref, bgate_ref, ya_ref, yd_ref, yb_ref, yc_ref,
                wpa_ref, wpd_ref, wpb_ref, wpc_ref, wo_ref, gq_ref, wq_ref, kv_ref, wmo_ref,
                o_ref, h_ref, att_ref, *, tn):
    n = pl.program_id(1)

    @pl.when(n == 0)
    def _():
        x = x_ref[...]
        h_ref[...] = _rms(x, g_ref[...]).astype(BF)
        o_ref[...] = x

    gates = jax.nn.sigmoid(_dot(h_ref[...], wgate_ref[...]) + bgate_ref[...])
    merged = None
    for i, (br_ref, wp_ref) in enumerate(((ya_ref, wpa_ref), (yd_ref, wpd_ref),
                                          (yb_ref, wpb_ref), (yc_ref, wpc_ref))):
        term = gates[:, i * tn:(i + 1) * tn] * _dot(br_ref[...], wp_ref[...])
        merged = term if merged is None else merged + term
    o_ref[...] += _dot(merged.astype(BF), wo_ref[...])

    @pl.when(n == pl.num_programs(1) - 1)
    def _():
        o_ref[...] = _memory_update(o_ref[...], gq_ref, wq_ref, kv_ref, wmo_ref, att_ref)


def _merge(lidx, x, g, w_gate, b_gate, branches, projs, w_o, g_memq, w_mq, kv, w_mo, seq):
    m, d = x.shape
    tm = min(512, m)
    tn = w_gate.shape[-1] // 4
    n_mem, kvw = kv.shape[1], kv.shape[2]
    mw = w_mq.shape[-1]
    assert seq % tm == 0

    def row_tile(a):
        return pl.BlockSpec((tm, a.shape[1]), lambda i, n, l: (i, 0))

    def proj_tile(w):
        return pl.BlockSpec((None, w.shape[1], tn), lambda i, n, l: (l[0], 0, n))

    return pl.pallas_call(
        functools.partial(_merge_body, tn=tn),
        grid_spec=_grid_spec(
            (m // tm, d // tn),
            [row_tile(x),
             pl.BlockSpec((None, 1, d), lambda i, n, l: (l[0], 0, 0)),
             pl.BlockSpec((None, None, d, 4 * tn), lambda i, n, l: (l[0], n, 0, 0)),
             pl.BlockSpec((None, None, 1, 4 * tn), lambda i, n, l: (l[0], n, 0, 0))]
            + [row_tile(b) for b in branches] + [proj_tile(w) for w in projs]
            + [pl.BlockSpec((None, tn, d), lambda i, n, l: (l[0], n, 0)),
               pl.BlockSpec((None, 1, d), lambda i, n, l: (l[0], 0, 0)),
               pl.BlockSpec((None, d, mw), lambda i, n, l: (l[0], 0, 0)),
               pl.BlockSpec((None, n_mem, kvw), lambda i, n, l: ((i * tm) // seq, 0, 0)),
               pl.BlockSpec((None, mw, d), lambda i, n, l: (l[0], 0, 0))],
            row_tile(x),
            [pltpu.VMEM((tm, d), BF), pltpu.VMEM((tm, mw), BF)]),
        out_shape=jax.ShapeDtypeStruct((m, d), F32),
        compiler_params=_cparams(2),
    )(lidx, x, g, w_gate, b_gate, *branches, *projs, w_o, g_memq, w_mq, kv, w_mo)


def _final_norm_body(l_ref, x_ref, g_ref, o_ref):
    o_ref[...] = _rms(x_ref[...], g_ref[...])


def _final_norm(lidx, x, g):
    m, d = x.shape
    tm = min(512, m)
    return pl.pallas_call(
        _final_norm_body,
        grid_spec=_grid_spec((m // tm,),
                             [pl.BlockSpec((tm, d), lambda i, l: (i, 0)),
                              pl.BlockSpec((1, d), lambda i, l: (0, 0))],
                             pl.BlockSpec((tm, d), lambda i, l: (i, 0))),
        out_shape=jax.ShapeDtypeStruct((m, d), F32),
        compiler_params=_cparams(1),
    )(lidx, x, g)


def _rope_tables(seq):
    half = HEAD_DIM // 2
    inv = ROPE_THETA ** (-jnp.arange(half, dtype=F32) / half)
    ang = jnp.arange(seq, dtype=jnp.int32).astype(F32)[:, None] * inv[None, :]
    cos, sin = jnp.cos(ang), jnp.sin(ang)
    return jnp.concatenate([cos, cos], axis=1), jnp.concatenate([-sin, sin], axis=1)


def _layer(lidx, x, mem, p, batch, seq, tabs):
    cos_t, sin_t, dft = tabs
    x = _ffn(lidx, x, p["g_ffn1"], p["w_ffn1_gate"], p["w_ffn1_up"], p["w_ffn1_down"])

    s1, _ = _dft_split(seq)
    z, zg1, zg2, uc = _proj_in(lidx, x, p["g_mix"], p["w_in"], cos_t, sin_t, batch, seq, s1)
    z4 = z.reshape(batch, 1, seq, Z_WIDTH)
    outs, lses = [], []
    for src, cols in ((z4, (QA0_COL, KA0_COL, VA0_COL)),
                      (zg1, (0, GROUP_WIDTH, 2 * GROUP_WIDTH)),
                      (zg2, (0, GROUP_WIDTH, 2 * GROUP_WIDTH))):
        o, lse = _band_attention(lidx, src, *cols, A_HEADS_PER_GROUP, 1, A_RADIUS, want_lse=True)
        outs.append(o)
        lses.append(lse)
    ya = _group_merge(lidx, outs, lses, seq)
    (yd,) = _band_attention(lidx, z4, QD_COL, KD_COL, VD_COL, D_Q_HEADS,
                            D_Q_HEADS // D_KV_HEADS, D_RADIUS, sink=p["sink"])
    yd = yd.reshape(batch * seq, D_Q_HEADS * HEAD_DIM)
    yb = _short_conv(lidx, z, p["w_conv"], seq)
    yc = _fourier_mix(lidx, uc, dft)
    kv = _proj(lidx, mem, p["g_memkv"], p["w_mkv"])
    kv = kv.reshape(batch, mem.shape[0] // batch, kv.shape[1])
    x = _merge(lidx, x, p["g_mix"], p["w_gate"], p["b_gate"], (ya, yd, yb, yc),
               (p["w_pa"], p["w_pd"], p["w_pb"], p["w_pc"]), p["w_o"],
               p["g_memq"], p["w_mq"], kv, p["w_mo"], seq)

    x = _ffn(lidx, x, p["g_ffn2"], p["w_ffn2_gate"], p["w_ffn2_up"], p["w_ffn2_down"])
    return x


def _forward(x_prompt, x_sample, mem_prompt, mem_sample, p, g_final):
    depth = p["w_o"].shape[0]
    d = x_prompt.shape[-1]
    trunks = []
    for x, mem in ((x_prompt, mem_prompt), (x_sample, mem_sample)):
        batch, seq = x.shape[0], x.shape[1]
        tabs = _rope_tables(seq) + (_dft_tables(seq),)
        trunks.append((batch, seq, tabs, mem.reshape(-1, d)))

    def body(xs, l):
        lidx = l.reshape(1)
        return tuple(_layer(lidx, x, mem, p, batch, seq, tabs)
                     for x, (batch, seq, tabs, mem) in zip(xs, trunks)), None

    xs = (x_prompt.reshape(-1, d), x_sample.reshape(-1, d))
    xs, _ = lax.scan(body, xs, jnp.arange(depth, dtype=jnp.int32))
    zero = jnp.zeros((1,), jnp.int32)
    gf = g_final.reshape(1, d)
    return tuple(_final_norm(zero, x, gf).reshape(orig.shape)
                 for x, orig in zip(xs, (x_prompt, x_sample)))


def kernel(x_prompt, x_sample, mem_prompt, mem_sample, g_ffn1, w_ffn1_gate, w_ffn1_up, w_ffn1_down, g_mix, w_in, w_conv, sink, w_gate, b_gate, w_pa, w_pd, w_pb, w_pc, w_o, g_memq, g_memkv, w_mq, w_mkv, w_mo, g_ffn2, w_ffn2_gate, w_ffn2_up, w_ffn2_down, g_final):
    depth, d = g_ffn1.shape

    def gain(g):
        return g.reshape(depth, 1, d)

    def cols(lo, hi):
        return w_in[..., lo:hi]

    def head_group(base, gi):
        return cols(base + gi * GROUP_WIDTH, base + (gi + 1) * GROUP_WIDTH)

    qa, ka, va = 0, A_WIDTH, 2 * A_WIDTH
    w_in_perm = jnp.concatenate(
        [head_group(qa, 0), head_group(ka, 0), cols(4608, 6144), head_group(va, 0), cols(6144, 9216),
         head_group(qa, 1), head_group(ka, 1), head_group(va, 1),
         head_group(qa, 2), head_group(ka, 2), head_group(va, 2), cols(9216, 10240)], axis=-1).astype(BF)
    tn = min(MERGE_COLS, d)
    w_gate_slabs = (w_gate.reshape(depth, 4, d, d // tn, tn).transpose(0, 3, 2, 1, 4)
                    .reshape(depth, d // tn, d, 4 * tn).astype(BF))
    b_gate_slabs = (b_gate.reshape(depth, 4, d // tn, tn).transpose(0, 2, 1, 3)
                    .reshape(depth, d // tn, 1, 4 * tn))
    p = {
        "g_ffn1": gain(g_ffn1), "w_ffn1_gate": w_ffn1_gate.astype(BF), "w_ffn1_up": w_ffn1_up.astype(BF),
        "w_ffn1_down": w_ffn1_down.astype(BF),
        "g_mix": gain(g_mix), "w_in": w_in_perm,
        "w_conv": w_conv, "sink": sink,
        "w_gate": w_gate_slabs, "b_gate": b_gate_slabs,
        "w_pa": w_pa.astype(BF), "w_pd": w_pd.astype(BF), "w_pb": w_pb.astype(BF), "w_pc": w_pc.astype(BF),
        "w_o": w_o.astype(BF),
        "g_memq": gain(g_memq), "g_memkv": gain(g_memkv), "w_mq": w_mq.astype(BF),
        "w_mkv": w_mkv.astype(BF), "w_mo": w_mo.astype(BF),
        "g_ffn2": gain(g_ffn2), "w_ffn2_gate": w_ffn2_gate.astype(BF), "w_ffn2_up": w_ffn2_up.astype(BF),
        "w_ffn2_down": w_ffn2_down.astype(BF),
    }
    return _forward(x_prompt, x_sample, mem_prompt, mem_sample, p, g_final)
```

```python
import functools

import jax
import jax.numpy as jnp
from jax import lax
from jax.experimental import pallas as pl
from jax.experimental.pallas import tpu as pltpu

BF = jnp.bfloat16
F32 = jnp.float32

HEAD_DIM = 128
EPS = 1e-6
ROPE_THETA = 10000.0
A_DILATIONS = (1, 4, 16)
A_RADIUS = 64
A_HEADS_PER_GROUP = 4
A_WIDTH = 1536
D_Q_HEADS = 8
D_KV_HEADS = 2
D_RADIUS = 128
B_WIDTH = 1024
C_WIDTH = 1024
C_GROUP_DIM = 128
M_HEADS = 4

QA0_COL, KA0_COL, QD_COL, KD_COL, VD_COL, VA0_COL, BG_COL, CG_COL, XB_COL, Z_WIDTH = (
    0, 512, 1024, 2048, 2304, 2560, 3072, 4096, 5120, 6144)
PROJ_TILE = 512
GROUP_WIDTH = 512
PLAIN, ROT_Q, ROT_K, ROT_HALF = 0, 1, 2, 3
TILE_KINDS = ((ROT_Q, ROT_K, ROT_Q, ROT_Q, ROT_HALF) + (PLAIN,) * 7
              + (ROT_Q, ROT_K, PLAIN) * 2
              + (PLAIN,) * 2)
Z_TILES, G_TILES, U_TILES = 12, 3, 2

ROW_CHUNK = 128
NEG_BIG = -1e30
VMEM_LIMIT_BYTES = 60 * 1024 * 1024
FFN_ROWS = 1024
PROJ_ROWS = 1024
MERGE_COLS = 256


def _cparams(ndim):
    return pltpu.CompilerParams(dimension_semantics=("arbitrary",) * ndim,
                                vmem_limit_bytes=VMEM_LIMIT_BYTES)


def _grid_spec(grid, in_specs, out_specs, scratch=()):
    return pltpu.PrefetchScalarGridSpec(num_scalar_prefetch=1, grid=grid, in_specs=in_specs,
                                        out_specs=out_specs, scratch_shapes=list(scratch))


def _rms(x, g):
    return x * lax.rsqrt(jnp.mean(x * x, axis=-1, keepdims=True) + EPS) * g


def _dot(a, b):
    return jnp.dot(a, b, preferred_element_type=F32)


def _dot_t(a, b):
    return lax.dot_general(a, b, (((1,), (1,)), ((), ())), preferred_element_type=F32)


def _ffn_body(l_ref, x_ref, g_ref, wg_ref, wu_ref, wd_ref, o_ref, h_ref):
    j = pl.program_id(1)

    @pl.when(j == 0)
    def _():
        x = x_ref[...]
        h_ref[...] = _rms(x, g_ref[...]).astype(BF)
        o_ref[...] = x

    h = h_ref[...]
    gate = _dot(h, wg_ref[...])
    up = _dot(h, wu_ref[...])
    a = (0.5 * gate * jax.nn.sigmoid(gate) * up).astype(BF)
    o_ref[...] += _dot(a, wd_ref[...])


def _ffn(lidx, x, g, wg, wu, wd):
    m, d = x.shape
    f = wg.shape[-1]
    tm, tf = min(FFN_ROWS, m), min(512, f)
    return pl.pallas_call(
        _ffn_body,
        grid_spec=_grid_spec(
            (m // tm, f // tf),
            [pl.BlockSpec((tm, d), lambda i, j, l: (i, 0)),
             pl.BlockSpec((None, 1, d), lambda i, j, l: (l[0], 0, 0)),
             pl.BlockSpec((None, d, tf), lambda i, j, l: (l[0], 0, j)),
             pl.BlockSpec((None, d, tf), lambda i, j, l: (l[0], 0, j)),
             pl.BlockSpec((None, tf, d), lambda i, j, l: (l[0], j, 0))],
            pl.BlockSpec((tm, d), lambda i, j, l: (i, 0)),
            [pltpu.VMEM((tm, d), BF)]),
        out_shape=jax.ShapeDtypeStruct((m, d), F32),
        compiler_params=_cparams(2),
    )(lidx, x, g, wg, wu, wd)


def _proj_body(l_ref, x_ref, g_ref, w_ref, o_ref, h_ref):
    @pl.when(pl.program_id(1) == 0)
    def _():
        h_ref[...] = _rms(x_ref[...], g_ref[...]).astype(BF)

    o_ref[...] = _dot(h_ref[...], w_ref[...]).astype(o_ref.dtype)


def _proj_in_body(l_ref, kind_ref, x_ref, g_ref, w_ref, cos_ref, sin_ref,
                  z_ref, g1_ref, g2_ref, uc_ref, h_ref, acc_ref, res_ref, *, tm, s1):
    j = pl.program_id(1)
    n_tiles = len(TILE_KINDS)
    chunks = PROJ_TILE // HEAD_DIM
    half_chunks = (VD_COL - KD_COL) // HEAD_DIM
    g1_0, g2_0, u_0 = Z_TILES, Z_TILES + G_TILES, Z_TILES + 2 * G_TILES
    n_rot = {PLAIN: 0, ROT_Q: chunks, ROT_K: chunks, ROT_HALF: half_chunks}

    def multiply_tile():
        acc_ref[...] = _dot(h_ref[...], w_ref[...])

    @pl.when(j == 0)
    def _():
        h_ref[...] = _rms(x_ref[...], g_ref[...]).astype(BF)
        multiply_tile()

    def lanes(c):
        return slice(c * HEAD_DIM, (c + 1) * HEAD_DIM)

    def finished(c, tile_kind):
        xc = acc_ref[:, lanes(c)]
        if c >= n_rot[tile_kind]:
            return xc
        rotated = xc * cos_ref[...] + pltpu.roll(xc, HEAD_DIM // 2, 1) * sin_ref[...]
        return rotated * HEAD_DIM ** -0.5 if tile_kind == ROT_Q else rotated

    def store_natural(tile_kind):
        for c in range(chunks):
            z_ref[:, lanes(c)] = finished(c, tile_kind).astype(BF)

    def store_residue_major(dst_ref, dil, tile_kind):
        for c in range(chunks):
            res_ref[c] = finished(c, tile_kind)
        rows = tm // dil
        for r in range(dil):
            for c in range(chunks):
                dst_ref[r, :, lanes(c)] = res_ref[c, pl.ds(r, rows, stride=dil), :].astype(BF)

    t = j - 1
    kind = kind_ref[jnp.maximum(t, 0)]
    destinations = (
        (0, g1_0, (PLAIN, ROT_Q, ROT_K, ROT_HALF), store_natural),
        (g1_0, g2_0, (PLAIN, ROT_Q, ROT_K), functools.partial(store_residue_major, g1_ref, A_DILATIONS[1])),
        (g2_0, u_0, (PLAIN, ROT_Q, ROT_K), functools.partial(store_residue_major, g2_ref, A_DILATIONS[2])),
        (u_0, n_tiles, (PLAIN,), functools.partial(store_residue_major, uc_ref, s1)),
    )
    for lo, hi, kinds, store in destinations:
        for tile_kind in kinds:
            in_range = (t >= lo) & (t < hi) & (kind == tile_kind)

            @pl.when(in_range & (j < n_tiles))
            def _(store=store, tile_kind=tile_kind):
                store(tile_kind)
                multiply_tile()

            if hi == n_tiles:
                @pl.when(in_range & (j == n_tiles))
                def _(store=store, tile_kind=tile_kind):
                    store(tile_kind)


def _proj(lidx, x, g, w, out_dtype=BF, tn=512):
    m, d = x.shape
    n = w.shape[-1]
    tm, tn = min(512, m), min(tn, n)
    return pl.pallas_call(
        _proj_body,
        grid_spec=_grid_spec(
            (m // tm, n // tn),
            [pl.BlockSpec((tm, d), lambda i, j, l: (i, 0)),
             pl.BlockSpec((None, 1, d), lambda i, j, l: (l[0], 0, 0)),
             pl.BlockSpec((None, d, tn), lambda i, j, l: (l[0], 0, j))],
            pl.BlockSpec((tm, tn), lambda i, j, l: (i, j)),
            [pltpu.VMEM((tm, d), BF)]),
        out_shape=jax.ShapeDtypeStruct((m, n), out_dtype),
        compiler_params=_cparams(2),
    )(lidx, x, g, w)


def _proj_in(lidx, x, g, w, cos_t, sin_t, batch, seq, s1):
    m, d = x.shape
    tm, tn = min(PROJ_ROWS, m), PROJ_TILE
    n_tiles = len(TILE_KINDS)
    d1, d2 = A_DILATIONS[1], A_DILATIONS[2]
    assert w.shape[-1] == n_tiles * tn and seq % tm == 0 and tm % (16 * max(d2, s1)) == 0
    nsb = seq // tm
    g1_0, g2_0, u_0 = Z_TILES, Z_TILES + G_TILES, Z_TILES + 2 * G_TILES

    def residue_major(dil, first_tile, tiles):
        return pl.BlockSpec((None, dil, tm // dil, tn),
                            lambda i, j, l, k: (i // nsb, 0, i % nsb, jnp.clip(j - 1 - first_tile, 0, tiles - 1)))

    return pl.pallas_call(
        functools.partial(_proj_in_body, tm=tm, s1=s1),
        grid_spec=pltpu.PrefetchScalarGridSpec(
            num_scalar_prefetch=2,
            grid=(m // tm, n_tiles + 1),
            in_specs=[pl.BlockSpec((tm, d), lambda i, j, l, k: (i, 0)),
                      pl.BlockSpec((None, 1, d), lambda i, j, l, k: (l[0], 0, 0)),
                      pl.BlockSpec((None, d, tn), lambda i, j, l, k: (l[0], 0, jnp.minimum(j, n_tiles - 1))),
                      pl.BlockSpec((tm, HEAD_DIM), lambda i, j, l, k: (i % nsb, 0)),
                      pl.BlockSpec((tm, HEAD_DIM), lambda i, j, l, k: (i % nsb, 0))],
            out_specs=[pl.BlockSpec((tm, tn), lambda i, j, l, k: (i, jnp.clip(j - 1, 0, Z_TILES - 1))),
                       residue_major(d1, g1_0, G_TILES),
                       residue_major(d2, g2_0, G_TILES),
                       residue_major(s1, u_0, U_TILES)],
            scratch_shapes=[pltpu.VMEM((tm, d), BF), pltpu.VMEM((tm, tn), F32),
                            pltpu.VMEM((tn // HEAD_DIM, tm, HEAD_DIM), F32)]),
        out_shape=[jax.ShapeDtypeStruct((m, Z_WIDTH), BF),
                   jax.ShapeDtypeStruct((batch, d1, seq // d1, G_TILES * tn), BF),
                   jax.ShapeDtypeStruct((batch, d2, seq // d2, G_TILES * tn), BF),
                   jax.ShapeDtypeStruct((batch, s1, seq // s1, U_TILES * tn), BF)],
        compiler_params=_cparams(2),
    )(lidx, jnp.asarray(TILE_KINDS, jnp.int32), x, g, w, cos_t, sin_t)


def _band_attn_body(l_ref, q_ref, kp_ref, kc_ref, kn_ref, vp_ref, vc_ref, vn_ref, *rest,
                    tq, length, radius, n_heads, rep, has_sink, want_lse):
    rest = list(rest)
    sink_ref = rest.pop(0) if has_sink else None
    o_ref = rest.pop(0)
    lse_ref = rest.pop(0) if want_lse else None
    i = pl.program_id(2)
    rc = ROW_CHUNK
    nc = tq // rc
    win = rc + 2 * radius
    groups = n_heads // rep
    assert radius <= rc and radius % 16 == 0

    def window(prev_ref, cur_ref, next_ref, c, cols):
        lo, hi = c * rc - radius, (c + 1) * rc + radius
        parts = []
        if lo < 0:
            parts.append(prev_ref[rc - radius:rc, cols])
        parts.append(cur_ref[max(lo, 0):min(hi, tq), cols])
        if hi > tq:
            parts.append(next_ref[0:radius, cols])
        return parts[0] if len(parts) == 1 else jnp.concatenate(parts, axis=0)

    row = lax.broadcasted_iota(jnp.int32, (rc, win), 0)
    col = lax.broadcasted_iota(jnp.int32, (rc, win), 1)
    band_bias = jnp.where(jnp.abs(col - radius - row) <= radius, 0.0, NEG_BIG)
    key_col = lax.broadcasted_iota(jnp.int32, (1, win), 1)
    if has_sink:
        sink_col = jnp.concatenate(
            [jnp.full((rc, 1), sink_ref[l_ref[0], h], F32) for h in range(n_heads)], axis=0)
    for c in range(nc):
        kpos = i * tq + c * rc - radius + key_col
        bias = band_bias + jnp.where(kpos >= 0, jnp.where(kpos < length, 0.0, NEG_BIG), NEG_BIG)
        if rep > 1:
            bias = jnp.concatenate([bias] * rep, axis=0)
        rows = slice(c * rc, (c + 1) * rc)
        scores = []
        for g in range(groups):
            k = window(kp_ref, kc_ref, kn_ref, c, slice(g * HEAD_DIM, (g + 1) * HEAD_DIM))
            q = q_ref[rows, g * rep * HEAD_DIM:(g + 1) * rep * HEAD_DIM]
            if rep > 1:
                q = jnp.concatenate([q[:, r * HEAD_DIM:(r + 1) * HEAD_DIM] for r in range(rep)], axis=0)
            scores.append(_dot_t(q, k) + bias)
        s = jnp.concatenate(scores, axis=0)
        m = jnp.max(s, axis=-1, keepdims=True)
        if has_sink:
            m = jnp.maximum(m, sink_col)
        p = jnp.exp(s - m)
        den = jnp.sum(p, axis=-1, keepdims=True)
        if has_sink:
            den = den + jnp.exp(sink_col - m)
        p = p.astype(BF)
        inv_den = 1.0 / den
        lse = m + jnp.log(den) if want_lse else None
        for g in range(groups):
            v = window(vp_ref, vc_ref, vn_ref, c, slice(g * HEAD_DIM, (g + 1) * HEAD_DIM))
            grows = slice(g * rep * rc, (g + 1) * rep * rc)
            o = _dot(p[grows], v) * inv_den[grows]
            for r in range(rep):
                h = g * rep + r
                hcols = slice(h * HEAD_DIM, (h + 1) * HEAD_DIM)
                o_ref[rows, hcols] = o[r * rc:(r + 1) * rc].astype(o_ref.dtype)
                if want_lse:
                    lse_ref[rows, hcols] = jnp.broadcast_to(lse[h * rc:(h + 1) * rc], (rc, HEAD_DIM))


def _band_attention(lidx, src, q_col, k_col, v_col, n_heads, rep, radius, sink=None, want_lse=False):
    batch, dil, length, width = src.shape
    tq = min(512, length)
    nc = tq // ROW_CHUNK
    qw, kw = n_heads * HEAD_DIM, (n_heads // rep) * HEAD_DIM
    assert length % tq == 0 and tq % ROW_CHUNK == 0
    assert q_col % qw == 0 and k_col % kw == 0 and v_col % kw == 0
    last_chunk = length // ROW_CHUNK - 1

    def kv_specs(col):
        cb = col // kw
        return [
            pl.BlockSpec((None, None, ROW_CHUNK, kw),
                         lambda b, r, i, l: (b, r, jnp.maximum(i * nc - 1, 0), cb)),
            pl.BlockSpec((None, None, tq, kw), lambda b, r, i, l: (b, r, i, cb)),
            pl.BlockSpec((None, None, ROW_CHUNK, kw),
                         lambda b, r, i, l: (b, r, jnp.minimum((i + 1) * nc, last_chunk), cb)),
        ]

    in_specs = [pl.BlockSpec((None, None, tq, qw), lambda b, r, i, l: (b, r, i, q_col // qw))]
    in_specs += kv_specs(k_col) + kv_specs(v_col)
    args = [lidx] + [src] * 7
    if sink is not None:
        in_specs.append(pl.BlockSpec(memory_space=pltpu.SMEM))
        args.append(sink)
    out_spec = pl.BlockSpec((None, None, tq, qw), lambda b, r, i, l: (b, r, i, 0))
    out_shape = [jax.ShapeDtypeStruct((batch, dil, length, qw), BF)]
    out_specs = [out_spec]
    if want_lse:
        out_shape.append(jax.ShapeDtypeStruct((batch, dil, length, qw), F32))
        out_specs.append(out_spec)
    return pl.pallas_call(
        functools.partial(_band_attn_body, tq=tq, length=length, radius=radius, n_heads=n_heads,
                          rep=rep, has_sink=sink is not None, want_lse=want_lse),
        grid_spec=_grid_spec((batch, dil, length // tq), in_specs, out_specs),
        out_shape=out_shape,
        compiler_params=_cparams(3),
    )(*args)


def _group_merge_body(l_ref, o0_ref, l0_ref, o1_ref, l1_ref, o2_ref, l2_ref, y_ref, nat_ref, *, tm):
    heads = A_HEADS_PER_GROUP
    for slot, (src_ref, dil) in enumerate(((o1_ref, A_DILATIONS[1]), (l1_ref, A_DILATIONS[1]),
                                           (o2_ref, A_DILATIONS[2]), (l2_ref, A_DILATIONS[2]))):
        rows = tm // dil
        for r in range(dil):
            for h in range(heads):
                nat_ref[slot * heads + h, pl.ds(r, rows, stride=dil), :] = (
                    src_ref[r, :, h * HEAD_DIM:(h + 1) * HEAD_DIM].astype(F32))
    for h in range(heads):
        hc = slice(h * HEAD_DIM, (h + 1) * HEAD_DIM)
        o1, l1, o2, l2 = (nat_ref[slot * heads + h] for slot in range(4))
        l0 = l0_ref[:, hc]
        m = jnp.maximum(jnp.maximum(l0, l1), l2)
        e0, e1, e2 = jnp.exp(l0 - m), jnp.exp(l1 - m), jnp.exp(l2 - m)
        num = e0 * o0_ref[:, hc].astype(F32) + e1 * o1 + e2 * o2
        y_ref[:, hc] = (num / (e0 + e1 + e2)).astype(y_ref.dtype)


def _group_merge(lidx, outs, lses, seq):
    batch, _, _, w = outs[0].shape
    m = batch * seq
    tm = min(512, seq)
    nsb = seq // tm
    assert tm % (16 * A_DILATIONS[2]) == 0

    def spec(dil):
        return pl.BlockSpec((None, dil, tm // dil, w), lambda i, l: (i // nsb, 0, i % nsb, 0))

    args = []
    in_specs = []
    for o, lse, dil in zip(outs, lses, A_DILATIONS):
        args += [o, lse]
        in_specs += [spec(dil)] * 2
    in_specs[0] = in_specs[1] = pl.BlockSpec((None, None, tm, w), lambda i, l: (i // nsb, 0, i % nsb, 0))
    return pl.pallas_call(
        functools.partial(_group_merge_body, tm=tm),
        grid_spec=_grid_spec((m // tm,), in_specs, pl.BlockSpec((tm, w), lambda i, l: (i, 0)),
                             [pltpu.VMEM((4 * A_HEADS_PER_GROUP, tm, HEAD_DIM), F32)]),
        out_shape=jax.ShapeDtypeStruct((m, w), BF),
        compiler_params=_cparams(1),
    )(lidx, *args)


def _conv_body(l_ref, bg_ref, cg_ref, xb_ref, cgp_ref, xbp_ref, cgn_ref, xbn_ref, w_ref, o_ref,
               *, tm, seq):
    i = pl.program_id(0)
    u = cg_ref[...].astype(F32) * xb_ref[...].astype(F32)
    not_first = ((i * tm) % seq != 0).astype(F32)
    not_last = (((i + 1) * tm) % seq != 0).astype(F32)
    u_prev = cgp_ref[7:8, :].astype(F32) * xbp_ref[7:8, :].astype(F32) * not_first
    u_next = cgn_ref[0:1, :].astype(F32) * xbn_ref[0:1, :].astype(F32) * not_last
    row = lax.broadcasted_iota(jnp.int32, u.shape, 0)
    below = jnp.where(row == 0, u_prev, pltpu.roll(u, 1, 0))
    above = jnp.where(row == tm - 1, u_next, pltpu.roll(u, tm - 1, 0))
    y = below * w_ref[0:1, :] + u * w_ref[1:2, :] + above * w_ref[2:3, :]
    o_ref[...] = (bg_ref[...].astype(F32) * y).astype(o_ref.dtype)


def _short_conv(lidx, z, w_conv, seq):
    m = z.shape[0]
    tm, tc = min(512, m), 512
    halo = 8
    assert seq % tm == 0 and B_WIDTH % tc == 0
    last_halo = m // halo - 1

    def tile(col):
        return pl.BlockSpec((tm, tc), lambda i, j, l: (i, col // tc + j))

    def prev(col):
        return pl.BlockSpec((halo, tc), lambda i, j, l: (jnp.maximum(i * (tm // halo) - 1, 0), col // tc + j))

    def nxt(col):
        return pl.BlockSpec((halo, tc),
                            lambda i, j, l: (jnp.minimum((i + 1) * (tm // halo), last_halo), col // tc + j))

    return pl.pallas_call(
        functools.partial(_conv_body, tm=tm, seq=seq),
        grid_spec=_grid_spec(
            (m // tm, B_WIDTH // tc),
            [tile(BG_COL), tile(CG_COL), tile(XB_COL), prev(CG_COL), prev(XB_COL), nxt(CG_COL), nxt(XB_COL),
             pl.BlockSpec((None, 3, tc), lambda i, j, l: (l[0], 0, j))],
            pl.BlockSpec((tm, tc), lambda i, j, l: (i, j))),
        out_shape=jax.ShapeDtypeStruct((m, B_WIDTH), BF),
        compiler_params=_cparams(2),
    )(lidx, z, z, z, z, z, z, z, w_conv)


def _dft1_body(l_ref, u_ref, g_ref, ar_ref, ai_ref, *, tn1, tk2, s2, groups):
    for jj in range(tn1):
        r = _dot(g_ref[jj], u_ref[jj])
        for g in range(groups):
            gc = slice(g * C_GROUP_DIM, (g + 1) * C_GROUP_DIM)
            for t in range(s2 // tk2):
                dst = slice(jj * tk2, (jj + 1) * tk2)
                ar_ref[g, t, dst, :] = r[t * tk2:(t + 1) * tk2, gc]
                ai_ref[g, t, dst, :] = r[s2 + t * tk2:s2 + (t + 1) * tk2, gc]


def _dft2_body(l_ref, ar_ref, ai_ref, m2_ref, cc_ref, sc_ref, y_ref, xr_ref, xi_ref,
               *, tk2, s1, groups, scale):
    for j in range(tk2):
        rows = pl.ds(j, s1, stride=tk2)
        a_re = jnp.concatenate([ar_ref[g, rows, :] for g in range(groups)], axis=1)
        a_im = jnp.concatenate([ai_ref[g, rows, :] for g in range(groups)], axis=1)
        x = _dot(m2_ref[...], jnp.concatenate([a_re, a_im], axis=0).astype(BF))
        for g in range(groups):
            gc = slice(g * C_GROUP_DIM, (g + 1) * C_GROUP_DIM)
            xr_ref[g, rows, :] = x[:s1, gc]
            xi_ref[g, rows, :] = x[s1:, gc]
    for g in range(groups):
        gc = slice(g * C_GROUP_DIM, (g + 1) * C_GROUP_DIM)
        yg = (_dot(xr_ref[g].astype(BF), cc_ref[...]) + _dot(xi_ref[g].astype(BF), sc_ref[...])) * scale
        for k1 in range(s1):
            y_ref[k1, :, gc] = yg[k1 * tk2:(k1 + 1) * tk2].astype(y_ref.dtype)


def _fourier_mix(lidx, u, tables):
    g1, m2, cc, sc = tables
    batch, s1, s2, width = u.shape
    groups = width // C_GROUP_DIM
    tn1, tk2 = 4, 16
    a_shape = jax.ShapeDtypeStruct((batch, groups, s2 // tk2, s1 * tk2, C_GROUP_DIM), F32)
    a_out = pl.BlockSpec((None, groups, s2 // tk2, tn1 * tk2, C_GROUP_DIM), lambda b, t, l: (b, 0, 0, t, 0))
    ar, ai = pl.pallas_call(
        functools.partial(_dft1_body, tn1=tn1, tk2=tk2, s2=s2, groups=groups),
        grid_spec=_grid_spec(
            (batch, s1 // tn1),
            [pl.BlockSpec((None, tn1, s2, width), lambda b, t, l: (b, t, 0, 0)),
             pl.BlockSpec((tn1, 2 * s2, s2), lambda b, t, l: (t, 0, 0))],
            [a_out, a_out]),
        out_shape=[a_shape, a_shape],
        compiler_params=_cparams(2),
    )(lidx, u, g1)
    a_in = pl.BlockSpec((None, groups, None, s1 * tk2, C_GROUP_DIM), lambda b, t, l: (b, 0, t, 0, 0))
    y = pl.pallas_call(
        functools.partial(_dft2_body, tk2=tk2, s1=s1, groups=groups,
                          scale=float((s1 * s2 * C_GROUP_DIM) ** -0.5)),
        grid_spec=_grid_spec(
            (batch, s2 // tk2),
            [a_in, a_in,
             pl.BlockSpec((2 * s1, 2 * s1), lambda b, t, l: (0, 0)),
             pl.BlockSpec((C_GROUP_DIM, C_GROUP_DIM), lambda b, t, l: (0, 0)),
             pl.BlockSpec((C_GROUP_DIM, C_GROUP_DIM), lambda b, t, l: (0, 0))],
            pl.BlockSpec((None, s1, tk2, width), lambda b, t, l: (b, 0, t, 0)),
            [pltpu.VMEM((groups, tk2 * s1, C_GROUP_DIM), F32)] * 2),
        out_shape=jax.ShapeDtypeStruct((batch, s1, s2, width), BF),
        compiler_params=_cparams(2),
    )(lidx, ar, ai, m2, cc, sc)
    return y.reshape(batch * s1 * s2, width)


def _dft_split(seq):
    s2 = 256 if seq >= 8192 else 128
    return seq // s2, s2


def _dft_tables(seq):
    s1, s2 = _dft_split(seq)
    two_pi = 2.0 * jnp.pi

    def cs(idx, period):
        ang = (idx % period).astype(F32) * (two_pi / period)
        return jnp.cos(ang), jnp.sin(ang)

    n1 = jnp.arange(s1, dtype=jnp.int32)[:, None, None]
    k2 = jnp.arange(s2, dtype=jnp.int32)[None, :, None]
    n2 = jnp.arange(s2, dtype=jnp.int32)[None, None, :]
    ca, sa = cs(k2 * (n1 + s1 * n2), seq)
    g1 = jnp.concatenate([ca, -sa], axis=1).astype(BF)
    a = jnp.arange(s1, dtype=jnp.int32)
    cb, sb = cs(a[:, None] * a[None, :], s1)
    m2 = jnp.concatenate([jnp.concatenate([cb, sb], axis=1),
                          jnp.concatenate([-sb, cb], axis=1)], axis=0).astype(BF)
    c = jnp.arange(C_GROUP_DIM, dtype=jnp.int32)
    cc, sc = cs(c[:, None] * c[None, :], C_GROUP_DIM)
    return g1, m2, cc.astype(BF), sc.astype(BF)


def _memory_update(x, g_ref, wq_ref, kv_ref, wo_ref, att_ref):
    h = _rms(x, g_ref[...]).astype(BF)
    q = (_dot(h, wq_ref[...]) * HEAD_DIM ** -0.5).astype(BF)
    mw = M_HEADS * HEAD_DIM
    for hh in range(M_HEADS):
        cols = slice(hh * HEAD_DIM, (hh + 1) * HEAD_DIM)
        k = kv_ref[:, cols]
        v = kv_ref[:, mw + hh * HEAD_DIM:mw + (hh + 1) * HEAD_DIM]
        s = _dot_t(q[:, cols], k)
        p = jnp.exp(s - jnp.max(s, axis=-1, keepdims=True))
        o = _dot(p.astype(BF), v) / jnp.sum(p, axis=-1, keepdims=True)
        att_ref[:, cols] = o.astype(BF)
    return x + _dot(att_ref[...], wo_ref[...])


def _merge_body(l_ref, x_ref, g_ref, wgate_ref, bgate_ref, ya_ref, yd_ref, yb_ref, yc_ref,
                wpa_ref, wpd_ref, wpb_ref, wpc_ref, wo_ref, gq_ref, wq_ref, kv_ref, wmo_ref,
                o_ref, h_ref, att_ref, *, tn):
    n = pl.program_id(1)

    @pl.when(n == 0)
    def _():
        x = x_ref[...]
        h_ref[...] = _rms(x, g_ref[...]).astype(BF)
        o_ref[...] = x

    gates = jax.nn.sigmoid(_dot(h_ref[...], wgate_ref[...]) + bgate_ref[...])
    merged = None
    for i, (br_ref, wp_ref) in enumerate(((ya_ref, wpa_ref), (yd_ref, wpd_ref),
                                          (yb_ref, wpb_ref), (yc_ref, wpc_ref))):
        term = gates[:, i * tn:(i + 1) * tn] * _dot(br_ref[...], wp_ref[...])
        merged = term if merged is None else merged + term
    o_ref[...] += _dot(merged.astype(BF), wo_ref[...])

    @pl.when(n == pl.num_programs(1) - 1)
    def _():
        o_ref[...] = _memory_update(o_ref[...], gq_ref, wq_ref, kv_ref, wmo_ref, att_ref)


def _merge(lidx, x, g, w_gate, b_gate, branches, projs, w_o, g_memq, w_mq, kv, w_mo, seq):
    m, d = x.shape
    tm = min(512, m)
    tn = w_gate.shape[-1] // 4
    n_mem, kvw = kv.shape[1], kv.shape[2]
    mw = w_mq.shape[-1]
    assert seq % tm == 0

    def row_tile(a):
        return pl.BlockSpec((tm, a.shape[1]), lambda i, n, l: (i, 0))

    def proj_tile(w):
        return pl.BlockSpec((None, w.shape[1], tn), lambda i, n, l: (l[0], 0, n))

    return pl.pallas_call(
        functools.partial(_merge_body, tn=tn),
        grid_spec=_grid_spec(
            (m // tm, d // tn),
            [row_tile(x),
             pl.BlockSpec((None, 1, d), lambda i, n, l: (l[0], 0, 0)),
             pl.BlockSpec((None, None, d, 4 * tn), lambda i, n, l: (l[0], n, 0, 0)),
             pl.BlockSpec((None, None, 1, 4 * tn), lambda i, n, l: (l[0], n, 0, 0))]
            + [row_tile(b) for b in branches] + [proj_tile(w) for w in projs]
            + [pl.BlockSpec((None, tn, d), lambda i, n, l: (l[0], n, 0)),
               pl.BlockSpec((None, 1, d), lambda i, n, l: (l[0], 0, 0)),
               pl.BlockSpec((None, d, mw), lambda i, n, l: (l[0], 0, 0)),
               pl.BlockSpec((None, n_mem, kvw), lambda i, n, l: ((i * tm) // seq, 0, 0)),
               pl.BlockSpec((None, mw, d), lambda i, n, l: (l[0], 0, 0))],
            row_tile(x),
            [pltpu.VMEM((tm, d), BF), pltpu.VMEM((tm, mw), BF)]),
        out_shape=jax.ShapeDtypeStruct((m, d), F32),
        compiler_params=_cparams(2),
    )(lidx, x, g, w_gate, b_gate, *branches, *projs, w_o, g_memq, w_mq, kv, w_mo)


def _final_norm_body(l_ref, x_ref, g_ref, o_ref):
    o_ref[...] = _rms(x_ref[...], g_ref[...])


def _final_norm(lidx, x, g):
    m, d = x.shape
    tm = min(512, m)
    return pl.pallas_call(
        _final_norm_body,
        grid_spec=_grid_spec((m // tm,),
                             [pl.BlockSpec((tm, d), lambda i, l: (i, 0)),
                              pl.BlockSpec((1, d), lambda i, l: (0, 0))],
                             pl.BlockSpec((tm, d), lambda i, l: (i, 0))),
        out_shape=jax.ShapeDtypeStruct((m, d), F32),
        compiler_params=_cparams(1),
    )(lidx, x, g)


def _rope_tables(seq):
    half = HEAD_DIM // 2
    inv = ROPE_THETA ** (-jnp.arange(half, dtype=F32) / half)
    ang = jnp.arange(seq, dtype=jnp.int32).astype(F32)[:, None] * inv[None, :]
    cos, sin = jnp.cos(ang), jnp.sin(ang)
    return jnp.concatenate([cos, cos], axis=1), jnp.concatenate([-sin, sin], axis=1)


def _layer(lidx, x, mem, p, batch, seq, tabs):
    cos_t, sin_t, dft = tabs
    x = _ffn(lidx, x, p["g_ffn1"], p["w_ffn1_gate"], p["w_ffn1_up"], p["w_ffn1_down"])

    s1, _ = _dft_split(seq)
    z, zg1, zg2, uc = _proj_in(lidx, x, p["g_mix"], p["w_in"], cos_t, sin_t, batch, seq, s1)
    z4 = z.reshape(batch, 1, seq, Z_WIDTH)
    outs, lses = [], []
    for src, cols in ((z4, (QA0_COL, KA0_COL, VA0_COL)),
                      (zg1, (0, GROUP_WIDTH, 2 * GROUP_WIDTH)),
                      (zg2, (0, GROUP_WIDTH, 2 * GROUP_WIDTH))):
        o, lse = _band_attention(lidx, src, *cols, A_HEADS_PER_GROUP, 1, A_RADIUS, want_lse=True)
        outs.append(o)
        lses.append(lse)
    ya = _group_merge(lidx, outs, lses, seq)
    (yd,) = _band_attention(lidx, z4, QD_COL, KD_COL, VD_COL, D_Q_HEADS,
                            D_Q_HEADS // D_KV_HEADS, D_RADIUS, sink=p["sink"])
    yd = yd.reshape(batch * seq, D_Q_HEADS * HEAD_DIM)
    yb = _short_conv(lidx, z, p["w_conv"], seq)
    yc = _fourier_mix(lidx, uc, dft)
    kv = _proj(lidx, mem, p["g_memkv"], p["w_mkv"])
    kv = kv.reshape(batch, mem.shape[0] // batch, kv.shape[1])
    x = _merge(lidx, x, p["g_mix"], p["w_gate"], p["b_gate"], (ya, yd, yb, yc),
               (p["w_pa"], p["w_pd"], p["w_pb"], p["w_pc"]), p["w_o"],
               p["g_memq"], p["w_mq"], kv, p["w_mo"], seq)

    x = _ffn(lidx, x, p["g_ffn2"], p["w_ffn2_gate"], p["w_ffn2_up"], p["w_ffn2_down"])
    return x


def _forward(x_prompt, x_sample, mem_prompt, mem_sample, p, g_final):
    depth = p["w_o"].shape[0]
    d = x_prompt.shape[-1]
    trunks = []
    for x, mem in ((x_prompt, mem_prompt), (x_sample, mem_sample)):
        batch, seq = x.shape[0], x.shape[1]
        tabs = _rope_tables(seq) + (_dft_tables(seq),)
        trunks.append((batch, seq, tabs, mem.reshape(-1, d)))

    def body(xs, l):
        lidx = l.reshape(1)
        return tuple(_layer(lidx, x, mem, p, batch, seq, tabs)
                     for x, (batch, seq, tabs, mem) in zip(xs, trunks)), None

    xs = (x_prompt.reshape(-1, d), x_sample.reshape(-1, d))
    xs, _ = lax.scan(body, xs, jnp.arange(depth, dtype=jnp.int32))
    zero = jnp.zeros((1,), jnp.int32)
    gf = g_final.reshape(1, d)
    return tuple(_final_norm(zero, x, gf).reshape(orig.shape)
                 for x, orig in zip(xs, (x_prompt, x_sample)))


def kernel(x_prompt, x_sample, mem_prompt, mem_sample, g_ffn1, w_ffn1_gate, w_ffn1_up, w_ffn1_down, g_mix, w_in, w_conv, sink, w_gate, b_gate, w_pa, w_pd, w_pb, w_pc, w_o, g_memq, g_memkv, w_mq, w_mkv, w_mo, g_ffn2, w_ffn2_gate, w_ffn2_up, w_ffn2_down, g_final):
    depth, d = g_ffn1.shape

    def gain(g):
        return g.reshape(depth, 1, d)

    def cols(lo, hi):
        return w_in[..., lo:hi]

    def head_group(base, gi):
        return cols(base + gi * GROUP_WIDTH, base + (gi + 1) * GROUP_WIDTH)

    qa, ka, va = 0, A_WIDTH, 2 * A_WIDTH
    w_in_perm = jnp.concatenate(
        [head_group(qa, 0), head_group(ka, 0), cols(4608, 6144), head_group(va, 0), cols(6144, 9216),
         head_group(qa, 1), head_group(ka, 1), head_group(va, 1),
         head_group(qa, 2), head_group(ka, 2), head_group(va, 2), cols(9216, 10240)], axis=-1).astype(BF)
    tn = min(MERGE_COLS, d)
    w_gate_slabs = (w_gate.reshape(depth, 4, d, d // tn, tn).transpose(0, 3, 2, 1, 4)
                    .reshape(depth, d // tn, d, 4 * tn).astype(BF))
    b_gate_slabs = (b_gate.reshape(depth, 4, d // tn, tn).transpose(0, 2, 1, 3)
                    .reshape(depth, d // tn, 1, 4 * tn))
    p = {
        "g_ffn1": gain(g_ffn1), "w_ffn1_gate": w_ffn1_gate.astype(BF), "w_ffn1_up": w_ffn1_up.astype(BF),
        "w_ffn1_down": w_ffn1_down.astype(BF),
        "g_mix": gain(g_mix), "w_in": w_in_perm,
        "w_conv": w_conv, "sink": sink,
        "w_gate": w_gate_slabs, "b_gate": b_gate_slabs,
        "w_pa": w_pa.astype(BF), "w_pd": w_pd.astype(BF), "w_pb": w_pb.astype(BF), "w_pc": w_pc.astype(BF),
        "w_o": w_o.astype(BF),
        "g_memq": gain(g_memq), "g_memkv": gain(g_memkv), "w_mq": w_mq.astype(BF),
        "w_mkv": w_mkv.astype(BF), "w_mo": w_mo.astype(BF),
        "g_ffn2": gain(g_ffn2), "w_ffn2_gate": w_ffn2_gate.astype(BF), "w_ffn2_up": w_ffn2_up.astype(BF),
        "w_ffn2_down": w_ffn2_down.astype(BF),
    }
    return _forward(x_prompt, x_sample, mem_prompt, mem_sample, p, g_final)
```

```python
import functools

import jax
import jax.numpy as jnp
from jax import lax
from jax.experimental import pallas as pl
from jax.experimental.pallas import tpu as pltpu

BF = jnp.bfloat16
F32 = jnp.float32

HEAD_DIM = 128
EPS = 1e-6
ROPE_THETA = 10000.0
A_DILATIONS = (1, 4, 16)
A_RADIUS = 64
A_HEADS_PER_GROUP = 4
A_WIDTH = 1536
D_Q_HEADS = 8
D_KV_HEADS = 2
D_RADIUS = 128
B_WIDTH = 1024
C_WIDTH = 1024
C_GROUP_DIM = 128
M_HEADS = 4

QA0_COL, KA0_COL, QD_COL, KD_COL, VD_COL, VA0_COL, BG_COL, CG_COL, XB_COL, Z_WIDTH = (
    0, 512, 1024, 2048, 2304, 2560, 3072, 4096, 5120, 6144)
PROJ_TILE = 512
GROUP_WIDTH = 512
PLAIN, ROT_Q, ROT_K, ROT_HALF = 0, 1, 2, 3
TILE_KINDS = ((ROT_Q, ROT_K, ROT_Q, ROT_Q, ROT_HALF) + (PLAIN,) * 7
              + (ROT_Q, ROT_K, PLAIN) * 2
              + (PLAIN,) * 2)
Z_TILES, G_TILES, U_TILES = 12, 3, 2

ROW_CHUNK = 128
ATTN_ROWS = 1024
SOFTMAX_ROWS = 32
NEG_BIG = -1e30
VMEM_LIMIT_BYTES = 60 * 1024 * 1024
FFN_ROWS = 1024
PROJ_ROWS = 1024
MERGE_COLS = 256


def _cparams(ndim):
    return pltpu.CompilerParams(dimension_semantics=("arbitrary",) * ndim,
                                vmem_limit_bytes=VMEM_LIMIT_BYTES)


def _grid_spec(grid, in_specs, out_specs, scratch=()):
    return pltpu.PrefetchScalarGridSpec(num_scalar_prefetch=1, grid=grid, in_specs=in_specs,
                                        out_specs=out_specs, scratch_shapes=list(scratch))


def _rms(x, g):
    return x * lax.rsqrt(jnp.mean(x * x, axis=-1, keepdims=True) + EPS) * g


def _dot(a, b):
    return jnp.dot(a, b, preferred_element_type=F32)


def _dot_t(a, b):
    return lax.dot_general(a, b, (((1,), (1,)), ((), ())), preferred_element_type=F32)


def _ffn_body(l_ref, x_ref, g_ref, wg_ref, wu_ref, wd_ref, o_ref, h_ref):
    j = pl.program_id(1)

    @pl.when(j == 0)
    def _():
        x = x_ref[...]
        h_ref[...] = _rms(x, g_ref[...]).astype(BF)
        o_ref[...] = x

    h = h_ref[...]
    gate = _dot(h, wg_ref[...])
    up = _dot(h, wu_ref[...])
    a = (0.5 * gate * jax.nn.sigmoid(gate) * up).astype(BF)
    o_ref[...] += _dot(a, wd_ref[...])


def _ffn(lidx, x, g, wg, wu, wd):
    m, d = x.shape
    f = wg.shape[-1]
    tm, tf = min(FFN_ROWS, m), min(512, f)
    return pl.pallas_call(
        _ffn_body,
        grid_spec=_grid_spec(
            (m // tm, f // tf),
            [pl.BlockSpec((tm, d), lambda i, j, l: (i, 0)),
             pl.BlockSpec((None, 1, d), lambda i, j, l: (l[0], 0, 0)),
             pl.BlockSpec((None, d, tf), lambda i, j, l: (l[0], 0, j)),
             pl.BlockSpec((None, d, tf), lambda i, j, l: (l[0], 0, j)),
             pl.BlockSpec((None, tf, d), lambda i, j, l: (l[0], j, 0))],
            pl.BlockSpec((tm, d), lambda i, j, l: (i, 0)),
            [pltpu.VMEM((tm, d), BF)]),
        out_shape=jax.ShapeDtypeStruct((m, d), F32),
        compiler_params=_cparams(2),
    )(lidx, x, g, wg, wu, wd)


def _proj_body(l_ref, x_ref, g_ref, w_ref, o_ref, h_ref):
    @pl.when(pl.program_id(1) == 0)
    def _():
        h_ref[...] = _rms(x_ref[...], g_ref[...]).astype(BF)

    o_ref[...] = _dot(h_ref[...], w_ref[...]).astype(o_ref.dtype)


def _proj_in_body(l_ref, kind_ref, x_ref, g_ref, w_ref, cos_ref, sin_ref,
                  z_ref, g1_ref, g2_ref, uc_ref, h_ref, acc_ref, res_ref, *, tm, s1):
    j = pl.program_id(1)
    n_tiles = len(TILE_KINDS)
    chunks = PROJ_TILE // HEAD_DIM
    half_chunks = (VD_COL - KD_COL) // HEAD_DIM
    g1_0, g2_0, u_0 = Z_TILES, Z_TILES + G_TILES, Z_TILES + 2 * G_TILES
    n_rot = {PLAIN: 0, ROT_Q: chunks, ROT_K: chunks, ROT_HALF: half_chunks}

    def multiply_tile():
        acc_ref[...] = _dot(h_ref[...], w_ref[...])

    @pl.when(j == 0)
    def _():
        h_ref[...] = _rms(x_ref[...], g_ref[...]).astype(BF)
        multiply_tile()

    def lanes(c):
        return slice(c * HEAD_DIM, (c + 1) * HEAD_DIM)

    def finished(c, tile_kind):
        xc = acc_ref[:, lanes(c)]
        if c >= n_rot[tile_kind]:
            return xc
        rotated = xc * cos_ref[...] + pltpu.roll(xc, HEAD_DIM // 2, 1) * sin_ref[...]
        return rotated * HEAD_DIM ** -0.5 if tile_kind == ROT_Q else rotated

    def store_natural(tile_kind):
        for c in range(chunks):
            z_ref[:, lanes(c)] = finished(c, tile_kind).astype(BF)

    def store_residue_major(dst_ref, dil, tile_kind):
        for c in range(chunks):
            res_ref[c] = finished(c, tile_kind)
        rows = tm // dil
        for r in range(dil):
            for c in range(chunks):
                dst_ref[r, :, lanes(c)] = res_ref[c, pl.ds(r, rows, stride=dil), :].astype(BF)

    t = j - 1
    kind = kind_ref[jnp.maximum(t, 0)]
    destinations = (
        (0, g1_0, (PLAIN, ROT_Q, ROT_K, ROT_HALF), store_natural),
        (g1_0, g2_0, (PLAIN, ROT_Q, ROT_K), functools.partial(store_residue_major, g1_ref, A_DILATIONS[1])),
        (g2_0, u_0, (PLAIN, ROT_Q, ROT_K), functools.partial(store_residue_major, g2_ref, A_DILATIONS[2])),
        (u_0, n_tiles, (PLAIN,), functools.partial(store_residue_major, uc_ref, s1)),
    )
    for lo, hi, kinds, store in destinations:
        for tile_kind in kinds:
            in_range = (t >= lo) & (t < hi) & (kind == tile_kind)

            @pl.when(in_range & (j < n_tiles))
            def _(store=store, tile_kind=tile_kind):
                store(tile_kind)
                multiply_tile()

            if hi == n_tiles:
                @pl.when(in_range & (j == n_tiles))
                def _(store=store, tile_kind=tile_kind):
                    store(tile_kind)


def _proj(lidx, x, g, w, out_dtype=BF, tn=512):
    m, d = x.shape
    n = w.shape[-1]
    tm, tn = min(512, m), min(tn, n)
    return pl.pallas_call(
        _proj_body,
        grid_spec=_grid_spec(
            (m // tm, n // tn),
            [pl.BlockSpec((tm, d), lambda i, j, l: (i, 0)),
             pl.BlockSpec((None, 1, d), lambda i, j, l: (l[0], 0, 0)),
             pl.BlockSpec((None, d, tn), lambda i, j, l: (l[0], 0, j))],
            pl.BlockSpec((tm, tn), lambda i, j, l: (i, j)),
            [pltpu.VMEM((tm, d), BF)]),
        out_shape=jax.ShapeDtypeStruct((m, n), out_dtype),
        compiler_params=_cparams(2),
    )(lidx, x, g, w)


def _proj_in(lidx, x, g, w, cos_t, sin_t, batch, seq, s1):
    m, d = x.shape
    tm, tn = min(PROJ_ROWS, m), PROJ_TILE
    n_tiles = len(TILE_KINDS)
    d1, d2 = A_DILATIONS[1], A_DILATIONS[2]
    assert w.shape[-1] == n_tiles * tn and seq % tm == 0 and tm % (16 * max(d2, s1)) == 0
    nsb = seq // tm
    g1_0, g2_0, u_0 = Z_TILES, Z_TILES + G_TILES, Z_TILES + 2 * G_TILES

    def residue_major(dil, first_tile, tiles):
        return pl.BlockSpec((None, dil, tm // dil, tn),
                            lambda i, j, l, k: (i // nsb, 0, i % nsb, jnp.clip(j - 1 - first_tile, 0, tiles - 1)))

    return pl.pallas_call(
        functools.partial(_proj_in_body, tm=tm, s1=s1),
        grid_spec=pltpu.PrefetchScalarGridSpec(
            num_scalar_prefetch=2,
            grid=(m // tm, n_tiles + 1),
            in_specs=[pl.BlockSpec((tm, d), lambda i, j, l, k: (i, 0)),
                      pl.BlockSpec((None, 1, d), lambda i, j, l, k: (l[0], 0, 0)),
                      pl.BlockSpec((None, d, tn), lambda i, j, l, k: (l[0], 0, jnp.minimum(j, n_tiles - 1))),
                      pl.BlockSpec((tm, HEAD_DIM), lambda i, j, l, k: (i % nsb, 0)),
                      pl.BlockSpec((tm, HEAD_DIM), lambda i, j, l, k: (i % nsb, 0))],
            out_specs=[pl.BlockSpec((tm, tn), lambda i, j, l, k: (i, jnp.clip(j - 1, 0, Z_TILES - 1))),
                       residue_major(d1, g1_0, G_TILES),
                       residue_major(d2, g2_0, G_TILES),
                       residue_major(s1, u_0, U_TILES)],
            scratch_shapes=[pltpu.VMEM((tm, d), BF), pltpu.VMEM((tm, tn), F32),
                            pltpu.VMEM((tn // HEAD_DIM, tm, HEAD_DIM), F32)]),
        out_shape=[jax.ShapeDtypeStruct((m, Z_WIDTH), BF),
                   jax.ShapeDtypeStruct((batch, d1, seq // d1, G_TILES * tn), BF),
                   jax.ShapeDtypeStruct((batch, d2, seq // d2, G_TILES * tn), BF),
                   jax.ShapeDtypeStruct((batch, s1, seq // s1, U_TILES * tn), BF)],
        compiler_params=_cparams(2),
    )(lidx, jnp.asarray(TILE_KINDS, jnp.int32), x, g, w, cos_t, sin_t)


def _band_attn_body(l_ref, q_ref, kp_ref, kc_ref, kn_ref, vp_ref, vc_ref, vn_ref, *rest,
                    tq, length, radius, n_heads, rep, has_sink, want_lse):
    rest = list(rest)
    sink_ref = rest.pop(0) if has_sink else None
    o_ref = rest.pop(0)
    lse_ref = rest.pop(0) if want_lse else None
    s_ref, p_ref, inv_ref = rest
    i = pl.program_id(2)
    rc = ROW_CHUNK
    nc = tq // rc
    win = rc + 2 * radius
    groups = n_heads // rep
    assert radius <= rc and radius % 16 == 0

    def window(prev_ref, cur_ref, next_ref, c, cols):
        lo, hi = c * rc - radius, (c + 1) * rc + radius
        parts = []
        if lo < 0:
            parts.append(prev_ref[rc - radius:rc, cols])
        parts.append(cur_ref[max(lo, 0):min(hi, tq), cols])
        if hi > tq:
            parts.append(next_ref[0:radius, cols])
        return parts[0] if len(parts) == 1 else jnp.concatenate(parts, axis=0)

    row = lax.broadcasted_iota(jnp.int32, (rc, win), 0)
    col = lax.broadcasted_iota(jnp.int32, (rc, win), 1)
    band_bias = jnp.where(jnp.abs(col - radius - row) <= radius, 0.0, NEG_BIG)
    key_col = lax.broadcasted_iota(jnp.int32, (1, win), 1)
    rb = SOFTMAX_ROWS
    for c in range(nc):
        kpos = i * tq + c * rc - radius + key_col
        bias = band_bias + jnp.where(kpos >= 0, jnp.where(kpos < length, 0.0, NEG_BIG), NEG_BIG)
        rows = slice(c * rc, (c + 1) * rc)
        for g in range(groups):
            k = window(kp_ref, kc_ref, kn_ref, c, slice(g * HEAD_DIM, (g + 1) * HEAD_DIM))
            q = q_ref[rows, g * rep * HEAD_DIM:(g + 1) * rep * HEAD_DIM]
            if rep > 1:
                q = jnp.concatenate([q[:, r * HEAD_DIM:(r + 1) * HEAD_DIM] for r in range(rep)], axis=0)
            s_ref[c % 2, g * rep * rc:(g + 1) * rep * rc, :] = _dot_t(q, k)
        for blk in range(n_heads * rc // rb):
            h, r0 = divmod(blk * rb, rc)
            srows = slice(blk * rb, (blk + 1) * rb)
            s = s_ref[c % 2, srows, :] + bias[r0:r0 + rb]
            m = jnp.max(s, axis=-1, keepdims=True)
            if has_sink:
                sk = sink_ref[l_ref[0], h]
                m = jnp.maximum(m, sk)
            p = jnp.exp(s - m)
            den = jnp.sum(p, axis=-1, keepdims=True)
            if has_sink:
                den = den + jnp.exp(sk - m)
            p_ref[c % 2, srows, :] = p.astype(BF)
            inv_ref[c % 2, srows, :] = jnp.broadcast_to(1.0 / den, (rb, HEAD_DIM))
            if want_lse:
                lse_ref[c * rc + r0:c * rc + r0 + rb, h * HEAD_DIM:(h + 1) * HEAD_DIM] = (
                    jnp.broadcast_to(m + jnp.log(den), (rb, HEAD_DIM)))
        for g in range(groups):
            v = window(vp_ref, vc_ref, vn_ref, c, slice(g * HEAD_DIM, (g + 1) * HEAD_DIM))
            grows = slice(g * rep * rc, (g + 1) * rep * rc)
            o = _dot(p_ref[c % 2, grows, :], v) * inv_ref[c % 2, grows, :]
            for r in range(rep):
                h = g * rep + r
                o_ref[rows, h * HEAD_DIM:(h + 1) * HEAD_DIM] = o[r * rc:(r + 1) * rc].astype(o_ref.dtype)


def _band_attention(lidx, src, q_col, k_col, v_col, n_heads, rep, radius, sink=None, want_lse=False):
    batch, dil, length, width = src.shape
    tq = min(ATTN_ROWS, length)
    nc = tq // ROW_CHUNK
    qw, kw = n_heads * HEAD_DIM, (n_heads // rep) * HEAD_DIM
    assert length % tq == 0 and tq % ROW_CHUNK == 0
    assert q_col % qw == 0 and k_col % kw == 0 and v_col % kw == 0
    last_chunk = length // ROW_CHUNK - 1

    def kv_specs(col):
        cb = col // kw
        return [
            pl.BlockSpec((None, None, ROW_CHUNK, kw),
                         lambda b, r, i, l: (b, r, jnp.maximum(i * nc - 1, 0), cb)),
            pl.BlockSpec((None, None, tq, kw), lambda b, r, i, l: (b, r, i, cb)),
            pl.BlockSpec((None, None, ROW_CHUNK, kw),
                         lambda b, r, i, l: (b, r, jnp.minimum((i + 1) * nc, last_chunk), cb)),
        ]

    in_specs = [pl.BlockSpec((None, None, tq, qw), lambda b, r, i, l: (b, r, i, q_col // qw))]
    in_specs += kv_specs(k_col) + kv_specs(v_col)
    args = [lidx] + [src] * 7
    if sink is not None:
        in_specs.append(pl.BlockSpec(memory_space=pltpu.SMEM))
        args.append(sink)
    out_spec = pl.BlockSpec((None, None, tq, qw), lambda b, r, i, l: (b, r, i, 0))
    out_shape = [jax.ShapeDtypeStruct((batch, dil, length, qw), BF)]
    out_specs = [out_spec]
    if want_lse:
        out_shape.append(jax.ShapeDtypeStruct((batch, dil, length, qw), F32))
        out_specs.append(out_spec)
    return pl.pallas_call(
        functools.partial(_band_attn_body, tq=tq, length=length, radius=radius, n_heads=n_heads,
                          rep=rep, has_sink=sink is not None, want_lse=want_lse),
        grid_spec=_grid_spec((batch, dil, length // tq), in_specs, out_specs,
                             [pltpu.VMEM((2, n_heads * ROW_CHUNK, ROW_CHUNK + 2 * radius), F32),
                              pltpu.VMEM((2, n_heads * ROW_CHUNK, ROW_CHUNK + 2 * radius), BF),
                              pltpu.VMEM((2, n_heads * ROW_CHUNK, HEAD_DIM), F32)]),
        out_shape=out_shape,
        compiler_params=_cparams(3),
    )(*args)


def _group_merge_body(l_ref, o0_ref, l0_ref, o1_ref, l1_ref, o2_ref, l2_ref, y_ref, nat_ref, *, tm):
    heads = A_HEADS_PER_GROUP
    for slot, (src_ref, dil) in enumerate(((o1_ref, A_DILATIONS[1]), (l1_ref, A_DILATIONS[1]),
                                           (o2_ref, A_DILATIONS[2]), (l2_ref, A_DILATIONS[2]))):
        rows = tm // dil
        for r in range(dil):
            for h in range(heads):
                nat_ref[slot * heads + h, pl.ds(r, rows, stride=dil), :] = (
                    src_ref[r, :, h * HEAD_DIM:(h + 1) * HEAD_DIM].astype(F32))
    for h in range(heads):
        hc = slice(h * HEAD_DIM, (h + 1) * HEAD_DIM)
        o1, l1, o2, l2 = (nat_ref[slot * heads + h] for slot in range(4))
        l0 = l0_ref[:, hc]
        m = jnp.maximum(jnp.maximum(l0, l1), l2)
        e0, e1, e2 = jnp.exp(l0 - m), jnp.exp(l1 - m), jnp.exp(l2 - m)
        num = e0 * o0_ref[:, hc].astype(F32) + e1 * o1 + e2 * o2
        y_ref[:, hc] = (num / (e0 + e1 + e2)).astype(y_ref.dtype)


def _group_merge(lidx, outs, lses, seq):
    batch, _, _, w = outs[0].shape
    m = batch * seq
    tm = min(1024, seq)
    nsb = seq // tm
    assert tm % (16 * A_DILATIONS[2]) == 0

    def spec(dil):
        return pl.BlockSpec((None, dil, tm // dil, w), lambda i, l: (i // nsb, 0, i % nsb, 0))

    args = []
    in_specs = []
    for o, lse, dil in zip(outs, lses, A_DILATIONS):
        args += [o, lse]
        in_specs += [spec(dil)] * 2
    in_specs[0] = in_specs[1] = pl.BlockSpec((None, None, tm, w), lambda i, l: (i // nsb, 0, i % nsb, 0))
    return pl.pallas_call(
        functools.partial(_group_merge_body, tm=tm),
        grid_spec=_grid_spec((m // tm,), in_specs, pl.BlockSpec((tm, w), lambda i, l: (i, 0)),
                             [pltpu.VMEM((4 * A_HEADS_PER_GROUP, tm, HEAD_DIM), F32)]),
        out_shape=jax.ShapeDtypeStruct((m, w), BF),
        compiler_params=_cparams(1),
    )(lidx, *args)


def _conv_body(l_ref, bg_ref, cg_ref, xb_ref, cgp_ref, xbp_ref, cgn_ref, xbn_ref, w_ref, o_ref,
               *, tm, seq):
    i = pl.program_id(0)
    u = cg_ref[...].astype(F32) * xb_ref[...].astype(F32)
    not_first = ((i * tm) % seq != 0).astype(F32)
    not_last = (((i + 1) * tm) % seq != 0).astype(F32)
    u_prev = cgp_ref[7:8, :].astype(F32) * xbp_ref[7:8, :].astype(F32) * not_first
    u_next = cgn_ref[0:1, :].astype(F32) * xbn_ref[0:1, :].astype(F32) * not_last
    row = lax.broadcasted_iota(jnp.int32, u.shape, 0)
    below = jnp.where(row == 0, u_prev, pltpu.roll(u, 1, 0))
    above = jnp.where(row == tm - 1, u_next, pltpu.roll(u, tm - 1, 0))
    y = below * w_ref[0:1, :] + u * w_ref[1:2, :] + above * w_ref[2:3, :]
    o_ref[...] = (bg_ref[...].astype(F32) * y).astype(o_ref.dtype)


def _short_conv(lidx, z, w_conv, seq):
    m = z.shape[0]
    tm, tc = min(1024, m), B_WIDTH
    halo = 8
    assert seq % tm == 0 and B_WIDTH % tc == 0
    last_halo = m // halo - 1

    def tile(col):
        return pl.BlockSpec((tm, tc), lambda i, j, l: (i, col // tc + j))

    def prev(col):
        return pl.BlockSpec((halo, tc), lambda i, j, l: (jnp.maximum(i * (tm // halo) - 1, 0), col // tc + j))

    def nxt(col):
        return pl.BlockSpec((halo, tc),
                            lambda i, j, l: (jnp.minimum((i + 1) * (tm // halo), last_halo), col // tc + j))

    return pl.pallas_call(
        functools.partial(_conv_body, tm=tm, seq=seq),
        grid_spec=_grid_spec(
            (m // tm, B_WIDTH // tc),
            [tile(BG_COL), tile(CG_COL), tile(XB_COL), prev(CG_COL), prev(XB_COL), nxt(CG_COL), nxt(XB_COL),
             pl.BlockSpec((None, 3, tc), lambda i, j, l: (l[0], 0, j))],
            pl.BlockSpec((tm, tc), lambda i, j, l: (i, j))),
        out_shape=jax.ShapeDtypeStruct((m, B_WIDTH), BF),
        compiler_params=_cparams(2),
    )(lidx, z, z, z, z, z, z, z, w_conv)


def _dft1_body(l_ref, u_ref, g_ref, ar_ref, ai_ref, *, tn1, tk2, s2, groups):
    for jj in range(tn1):
        r = _dot(g_ref[jj], u_ref[jj])
        for g in range(groups):
            gc = slice(g * C_GROUP_DIM, (g + 1) * C_GROUP_DIM)
            for t in range(s2 // tk2):
                dst = slice(jj * tk2, (jj + 1) * tk2)
                ar_ref[g, t, dst, :] = r[t * tk2:(t + 1) * tk2, gc]
                ai_ref[g, t, dst, :] = r[s2 + t * tk2:s2 + (t + 1) * tk2, gc]


def _dft2_body(l_ref, ar_ref, ai_ref, m2_ref, cc_ref, sc_ref, y_ref, xr_ref, xi_ref,
               *, tk2, s1, groups, scale):
    for j in range(tk2):
        rows = pl.ds(j, s1, stride=tk2)
        a_re = jnp.concatenate([ar_ref[g, rows, :] for g in range(groups)], axis=1)
        a_im = jnp.concatenate([ai_ref[g, rows, :] for g in range(groups)], axis=1)
        x = _dot(m2_ref[...], jnp.concatenate([a_re, a_im], axis=0).astype(BF))
        for g in range(groups):
            gc = slice(g * C_GROUP_DIM, (g + 1) * C_GROUP_DIM)
            xr_ref[g, rows, :] = x[:s1, gc]
            xi_ref[g, rows, :] = x[s1:, gc]
    for g in range(groups):
        gc = slice(g * C_GROUP_DIM, (g + 1) * C_GROUP_DIM)
        yg = (_dot(xr_ref[g].astype(BF), cc_ref[...]) + _dot(xi_ref[g].astype(BF), sc_ref[...])) * scale
        for k1 in range(s1):
            y_ref[k1, :, gc] = yg[k1 * tk2:(k1 + 1) * tk2].astype(y_ref.dtype)


def _fourier_mix(lidx, u, tables):
    g1, m2, cc, sc = tables
    batch, s1, s2, width = u.shape
    groups = width // C_GROUP_DIM
    tn1, tk2 = 4, 16
    a_shape = jax.ShapeDtypeStruct((batch, groups, s2 // tk2, s1 * tk2, C_GROUP_DIM), F32)
    a_out = pl.BlockSpec((None, groups, s2 // tk2, tn1 * tk2, C_GROUP_DIM), lambda b, t, l: (b, 0, 0, t, 0))
    ar, ai = pl.pallas_call(
        functools.partial(_dft1_body, tn1=tn1, tk2=tk2, s2=s2, groups=groups),
        grid_spec=_grid_spec(
            (batch, s1 // tn1),
            [pl.BlockSpec((None, tn1, s2, width), lambda b, t, l: (b, t, 0, 0)),
             pl.BlockSpec((tn1, 2 * s2, s2), lambda b, t, l: (t, 0, 0))],
            [a_out, a_out]),
        out_shape=[a_shape, a_shape],
        compiler_params=_cparams(2),
    )(lidx, u, g1)
    a_in = pl.BlockSpec((None, groups, None, s1 * tk2, C_GROUP_DIM), lambda b, t, l: (b, 0, t, 0, 0))
    y = pl.pallas_call(
        functools.partial(_dft2_body, tk2=tk2, s1=s1, groups=groups,
                          scale=float((s1 * s2 * C_GROUP_DIM) ** -0.5)),
        grid_spec=_grid_spec(
            (batch, s2 // tk2),
            [a_in, a_in,
             pl.BlockSpec((2 * s1, 2 * s1), lambda b, t, l: (0, 0)),
             pl.BlockSpec((C_GROUP_DIM, C_GROUP_DIM), lambda b, t, l: (0, 0)),
             pl.BlockSpec((C_GROUP_DIM, C_GROUP_DIM), lambda b, t, l: (0, 0))],
            pl.BlockSpec((None, s1, tk2, width), lambda b, t, l: (b, 0, t, 0)),
            [pltpu.VMEM((groups, tk2 * s1, C_GROUP_DIM), F32)] * 2),
        out_shape=jax.ShapeDtypeStruct((batch, s1, s2, width), BF),
        compiler_params=_cparams(2),
    )(lidx, ar, ai, m2, cc, sc)
    return y.reshape(batch * s1 * s2, width)


def _dft_split(seq):
    s2 = 256 if seq >= 8192 else 128
    return seq // s2, s2


def _dft_tables(seq):
    s1, s2 = _dft_split(seq)
    two_pi = 2.0 * jnp.pi

    def cs(idx, period):
        ang = (idx % period).astype(F32) * (two_pi / period)
        return jnp.cos(ang), jnp.sin(ang)

    n1 = jnp.arange(s1, dtype=jnp.int32)[:, None, None]
    k2 = jnp.arange(s2, dtype=jnp.int32)[None, :, None]
    n2 = jnp.arange(s2, dtype=jnp.int32)[None, None, :]
    ca, sa = cs(k2 * (n1 + s1 * n2), seq)
    g1 = jnp.concatenate([ca, -sa], axis=1).astype(BF)
    a = jnp.arange(s1, dtype=jnp.int32)
    cb, sb = cs(a[:, None] * a[None, :], s1)
    m2 = jnp.concatenate([jnp.concatenate([cb, sb], axis=1),
                          jnp.concatenate([-sb, cb], axis=1)], axis=0).astype(BF)
    c = jnp.arange(C_GROUP_DIM, dtype=jnp.int32)
    cc, sc = cs(c[:, None] * c[None, :], C_GROUP_DIM)
    return g1, m2, cc.astype(BF), sc.astype(BF)


def _memory_update(x, g_ref, wq_ref, kv_ref, wo_ref, att_ref):
    h = _rms(x, g_ref[...]).astype(BF)
    q = (_dot(h, wq_ref[...]) * HEAD_DIM ** -0.5).astype(BF)
    mw = M_HEADS * HEAD_DIM
    for hh in range(M_HEADS):
        cols = slice(hh * HEAD_DIM, (hh + 1) * HEAD_DIM)
        k = kv_ref[:, cols]
        v = kv_ref[:, mw + hh * HEAD_DIM:mw + (hh + 1) * HEAD_DIM]
        s = _dot_t(q[:, cols], k)
        p = jnp.exp(s - jnp.max(s, axis=-1, keepdims=True))
        o = _dot(p.astype(BF), v) / jnp.sum(p, axis=-1, keepdims=True)
        att_ref[:, cols] = o.astype(BF)
    return x + _dot(att_ref[...], wo_ref[...])


def _merge_body(l_ref, x_ref, g_ref, wgate_ref, bgate_ref, ya_ref, yd_ref, yb_ref, yc_ref,
                wpa_ref, wpd_ref, wpb_ref, wpc_ref, wo_ref, gq_ref, wq_ref, kv_ref, wmo_ref,
                o_ref, h_ref, att_ref, *, tn):
    n = pl.program_id(1)

    @pl.when(n == 0)
    def _():
        x = x_ref[...]
        h_ref[...] = _rms(x, g_ref[...]).astype(BF)
        o_ref[...] = x

    gates = jax.nn.sigmoid(_dot(h_ref[...], wgate_ref[...]) + bgate_ref[...])
    merged = None
    for i, (br_ref, wp_ref) in enumerate(((ya_ref, wpa_ref), (yd_ref, wpd_ref),
                                          (yb_ref, wpb_ref), (yc_ref, wpc_ref))):
        term = gates[:, i * tn:(i + 1) * tn] * _dot(br_ref[...], wp_ref[...])
        merged = term if merged is None else merged + term
    o_ref[...] += _dot(merged.astype(BF), wo_ref[...])

    @pl.when(n == pl.num_programs(1) - 1)
    def _():
        o_ref[...] = _memory_update(o_ref[...], gq_ref, wq_ref, kv_ref, wmo_ref, att_ref)


def _merge(lidx, x, g, w_gate, b_gate, branches, projs, w_o, g_memq, w_mq, kv, w_mo, seq):
    m, d = x.shape
    tm = min(512, m)
    tn = w_gate.shape[-1] // 4
    n_mem, kvw = kv.shape[1], kv.shape[2]
    mw = w_mq.shape[-1]
    assert seq % tm == 0

    def row_tile(a):
        return pl.BlockSpec((tm, a.shape[1]), lambda i, n, l: (i, 0))

    def proj_tile(w):
        return pl.BlockSpec((None, w.shape[1], tn), lambda i, n, l: (l[0], 0, n))

    return pl.pallas_call(
        functools.partial(_merge_body, tn=tn),
        grid_spec=_grid_spec(
            (m // tm, d // tn),
            [row_tile(x),
             pl.BlockSpec((None, 1, d), lambda i, n, l: (l[0], 0, 0)),
             pl.BlockSpec((None, None, d, 4 * tn), lambda i, n, l: (l[0], n, 0, 0)),
             pl.BlockSpec((None, None, 1, 4 * tn), lambda i, n, l: (l[0], n, 0, 0))]
            + [row_tile(b) for b in branches] + [proj_tile(w) for w in projs]
            + [pl.BlockSpec((None, tn, d), lambda i, n, l: (l[0], n, 0)),
               pl.BlockSpec((None, 1, d), lambda i, n, l: (l[0], 0, 0)),
               pl.BlockSpec((None, d, mw), lambda i, n, l: (l[0], 0, 0)),
               pl.BlockSpec((None, n_mem, kvw), lambda i, n, l: ((i * tm) // seq, 0, 0)),
               pl.BlockSpec((None, mw, d), lambda i, n, l: (l[0], 0, 0))],
            row_tile(x),
            [pltpu.VMEM((tm, d), BF), pltpu.VMEM((tm, mw), BF)]),
        out_shape=jax.ShapeDtypeStruct((m, d), F32),
        compiler_params=_cparams(2),
    )(lidx, x, g, w_gate, b_gate, *branches, *projs, w_o, g_memq, w_mq, kv, w_mo)


def _final_norm_body(l_ref, x_ref, g_ref, o_ref):
    o_ref[...] = _rms(x_ref[...], g_ref[...])


def _final_norm(lidx, x, g):
    m, d = x.shape
    tm = min(512, m)
    return pl.pallas_call(
        _final_norm_body,
        grid_spec=_grid_spec((m // tm,),
                             [pl.BlockSpec((tm, d), lambda i, l: (i, 0)),
                              pl.BlockSpec((1, d), lambda i, l: (0, 0))],
                             pl.BlockSpec((tm, d), lambda i, l: (i, 0))),
        out_shape=jax.ShapeDtypeStruct((m, d), F32),
        compiler_params=_cparams(1),
    )(lidx, x, g)


def _rope_tables(seq):
    half = HEAD_DIM // 2
    inv = ROPE_THETA ** (-jnp.arange(half, dtype=F32) / half)
    ang = jnp.arange(seq, dtype=jnp.int32).astype(F32)[:, None] * inv[None, :]
    cos, sin = jnp.cos(ang), jnp.sin(ang)
    return jnp.concatenate([cos, cos], axis=1), jnp.concatenate([-sin, sin], axis=1)


def _layer(lidx, x, mem, p, batch, seq, tabs):
    cos_t, sin_t, dft = tabs
    x = _ffn(lidx, x, p["g_ffn1"], p["w_ffn1_gate"], p["w_ffn1_up"], p["w_ffn1_down"])

    s1, _ = _dft_split(seq)
    z, zg1, zg2, uc = _proj_in(lidx, x, p["g_mix"], p["w_in"], cos_t, sin_t, batch, seq, s1)
    z4 = z.reshape(batch, 1, seq, Z_WIDTH)
    outs, lses = [], []
    for src, cols in ((z4, (QA0_COL, KA0_COL, VA0_COL)),
                      (zg1, (0, GROUP_WIDTH, 2 * GROUP_WIDTH)),
                      (zg2, (0, GROUP_WIDTH, 2 * GROUP_WIDTH))):
        o, lse = _band_attention(lidx, src, *cols, A_HEADS_PER_GROUP, 1, A_RADIUS, want_lse=True)
        outs.append(o)
        lses.append(lse)
    ya = _group_merge(lidx, outs, lses, seq)
    (yd,) = _band_attention(lidx, z4, QD_COL, KD_COL, VD_COL, D_Q_HEADS,
                            D_Q_HEADS // D_KV_HEADS, D_RADIUS, sink=p["sink"])
    yd = yd.reshape(batch * seq, D_Q_HEADS * HEAD_DIM)
    yb = _short_conv(lidx, z, p["w_conv"], seq)
    yc = _fourier_mix(lidx, uc, dft)
    kv = _proj(lidx, mem, p["g_memkv"], p["w_mkv"])
    kv = kv.reshape(batch, mem.shape[0] // batch, kv.shape[1])
    x = _merge(lidx, x, p["g_mix"], p["w_gate"], p["b_gate"], (ya, yd, yb, yc),
               (p["w_pa"], p["w_pd"], p["w_pb"], p["w_pc"]), p["w_o"],
               p["g_memq"], p["w_mq"], kv, p["w_mo"], seq)

    x = _ffn(lidx, x, p["g_ffn2"], p["w_ffn2_gate"], p["w_ffn2_up"], p["w_ffn2_down"])
    return x


def _forward(x_prompt, x_sample, mem_prompt, mem_sample, p, g_final):
    depth = p["w_o"].shape[0]
    d = x_prompt.shape[-1]
    trunks = []
    for x, mem in ((x_prompt, mem_prompt), (x_sample, mem_sample)):
        batch, seq = x.shape[0], x.shape[1]
        tabs = _rope_tables(seq) + (_dft_tables(seq),)
        trunks.append((batch, seq, tabs, mem.reshape(-1, d)))

    def body(xs, l):
        lidx = l.reshape(1)
        return tuple(_layer(lidx, x, mem, p, batch, seq, tabs)
                     for x, (batch, seq, tabs, mem) in zip(xs, trunks)), None

    xs = (x_prompt.reshape(-1, d), x_sample.reshape(-1, d))
    xs, _ = lax.scan(body, xs, jnp.arange(depth, dtype=jnp.int32))
    zero = jnp.zeros((1,), jnp.int32)
    gf = g_final.reshape(1, d)
    return tuple(_final_norm(zero, x, gf).reshape(orig.shape)
                 for x, orig in zip(xs, (x_prompt, x_sample)))


def kernel(x_prompt, x_sample, mem_prompt, mem_sample, g_ffn1, w_ffn1_gate, w_ffn1_up, w_ffn1_down, g_mix, w_in, w_conv, sink, w_gate, b_gate, w_pa, w_pd, w_pb, w_pc, w_o, g_memq, g_memkv, w_mq, w_mkv, w_mo, g_ffn2, w_ffn2_gate, w_ffn2_up, w_ffn2_down, g_final):
    depth, d = g_ffn1.shape

    def gain(g):
        return g.reshape(depth, 1, d)

    def cols(lo, hi):
        return w_in[..., lo:hi]

    def head_group(base, gi):
        return cols(base + gi * GROUP_WIDTH, base + (gi + 1) * GROUP_WIDTH)

    qa, ka, va = 0, A_WIDTH, 2 * A_WIDTH
    w_in_perm = jnp.concatenate(
        [head_group(qa, 0), head_group(ka, 0), cols(4608, 6144), head_group(va, 0), cols(6144, 9216),
         head_group(qa, 1), head_group(ka, 1), head_group(va, 1),
         head_group(qa, 2), head_group(ka, 2), head_group(va, 2), cols(9216, 10240)], axis=-1).astype(BF)
    tn = min(MERGE_COLS, d)
    w_gate_slabs = (w_gate.reshape(depth, 4, d, d // tn, tn).transpose(0, 3, 2, 1, 4)
                    .reshape(depth, d // tn, d, 4 * tn).astype(BF))
    b_gate_slabs = (b_gate.reshape(depth, 4, d // tn, tn).transpose(0, 2, 1, 3)
                    .reshape(depth, d // tn, 1, 4 * tn))
    p = {
        "g_ffn1": gain(g_ffn1), "w_ffn1_gate": w_ffn1_gate.astype(BF), "w_ffn1_up": w_ffn1_up.astype(BF),
        "w_ffn1_down": w_ffn1_down.astype(BF),
        "g_mix": gain(g_mix), "w_in": w_in_perm,
        "w_conv": w_conv, "sink": sink,
        "w_gate": w_gate_slabs, "b_gate": b_gate_slabs,
        "w_pa": w_pa.astype(BF), "w_pd": w_pd.astype(BF), "w_pb": w_pb.astype(BF), "w_pc": w_pc.astype(BF),
        "w_o": w_o.astype(BF),
        "g_memq": gain(g_memq), "g_memkv": gain(g_memkv), "w_mq": w_mq.astype(BF),
        "w_mkv": w_mkv.astype(BF), "w_mo": w_mo.astype(BF),
        "g_ffn2": gain(g_ffn2), "w_ffn2_gate": w_ffn2_gate.astype(BF), "w_ffn2_up": w_ffn2_up.astype(BF),
        "w_ffn2_down": w_ffn2_down.astype(BF),
    }
    return _forward(x_prompt, x_sample, mem_prompt, mem_sample, p, g_final)
```

```python
import functools

import jax
import jax.numpy as jnp
from jax import lax
from jax.experimental import pallas as pl
from jax.experimental.pallas import tpu as pltpu

BF = jnp.bfloat16
F32 = jnp.float32

HEAD_DIM = 128
EPS = 1e-6
ROPE_THETA = 10000.0
A_DILATIONS = (1, 4, 16)
A_RADIUS = 64
A_HEADS_PER_GROUP = 4
A_WIDTH = 1536
D_Q_HEADS = 8
D_KV_HEADS = 2
D_RADIUS = 128
B_WIDTH = 1024
C_WIDTH = 1024
C_GROUP_DIM = 128
M_HEADS = 4

QA0_COL, KA0_COL, QD_COL, KD_COL, VD_COL, VA0_COL, BG_COL, CG_COL, XB_COL, Z_WIDTH = (
    0, 512, 1024, 2048, 2304, 2560, 3072, 4096, 5120, 6144)
PROJ_TILE = 512
GROUP_WIDTH = 512
PLAIN, ROT_Q, ROT_K, ROT_HALF = 0, 1, 2, 3
TILE_KINDS = ((ROT_Q, ROT_K, ROT_Q, ROT_Q, ROT_HALF) + (PLAIN,) * 7
              + (ROT_Q, ROT_K, PLAIN) * 2
              + (PLAIN,) * 2)
Z_TILES, G_TILES, U_TILES = 12, 3, 2

ROW_CHUNK = 128
ATTN_ROWS = 1024
SOFTMAX_ROWS = 32
NEG_BIG = -1e30
VMEM_LIMIT_BYTES = 60 * 1024 * 1024
FFN_ROWS = 1024
PROJ_ROWS = 1024
MERGE_COLS = 256


def _cparams(ndim):
    return pltpu.CompilerParams(dimension_semantics=("arbitrary",) * ndim,
                                vmem_limit_bytes=VMEM_LIMIT_BYTES)


def _grid_spec(grid, in_specs, out_specs, scratch=()):
    return pltpu.PrefetchScalarGridSpec(num_scalar_prefetch=1, grid=grid, in_specs=in_specs,
                                        out_specs=out_specs, scratch_shapes=list(scratch))


def _rms(x, g):
    return x * lax.rsqrt(jnp.mean(x * x, axis=-1, keepdims=True) + EPS) * g


def _dot(a, b):
    return jnp.dot(a, b, preferred_element_type=F32)


def _dot_t(a, b):
    return lax.dot_general(a, b, (((1,), (1,)), ((), ())), preferred_element_type=F32)


def _ffn_body(l_ref, x_ref, g_ref, wg_ref, wu_ref, wd_ref, o_ref, h_ref):
    j = pl.program_id(1)

    def step(first):
        if first:
            h_ref[...] = _rms(x_ref[...], g_ref[...]).astype(BF)
        h = h_ref[...]
        gate = _dot(h, wg_ref[...])
        up = _dot(h, wu_ref[...])
        a = (0.5 * gate * jax.nn.sigmoid(gate) * up).astype(BF)
        o_ref[...] = (x_ref[...] if first else o_ref[...]) + _dot(a, wd_ref[...])

    pl.when(j == 0)(functools.partial(step, True))
    pl.when(j > 0)(functools.partial(step, False))


def _ffn(lidx, x, g, wg, wu, wd):
    m, d = x.shape
    f = wg.shape[-1]
    tm, tf = min(FFN_ROWS, m), min(512, f)
    return pl.pallas_call(
        _ffn_body,
        grid_spec=_grid_spec(
            (m // tm, f // tf),
            [pl.BlockSpec((tm, d), lambda i, j, l: (i, 0)),
             pl.BlockSpec((None, 1, d), lambda i, j, l: (l[0], 0, 0)),
             pl.BlockSpec((None, d, tf), lambda i, j, l: (l[0], 0, j)),
             pl.BlockSpec((None, d, tf), lambda i, j, l: (l[0], 0, j)),
             pl.BlockSpec((None, tf, d), lambda i, j, l: (l[0], j, 0))],
            pl.BlockSpec((tm, d), lambda i, j, l: (i, 0)),
            [pltpu.VMEM((tm, d), BF)]),
        out_shape=jax.ShapeDtypeStruct((m, d), F32),
        compiler_params=_cparams(2),
    )(lidx, x, g, wg, wu, wd)


def _proj_body(l_ref, x_ref, g_ref, w_ref, o_ref, h_ref):
    @pl.when(pl.program_id(1) == 0)
    def _():
        h_ref[...] = _rms(x_ref[...], g_ref[...]).astype(BF)

    o_ref[...] = _dot(h_ref[...], w_ref[...]).astype(o_ref.dtype)


def _proj_in_body(l_ref, kind_ref, x_ref, g_ref, w_ref, cos_ref, sin_ref,
                  z_ref, g1_ref, g2_ref, uc_ref, h_ref, acc_ref, res_ref, *, tm, s1):
    j = pl.program_id(2)
    n_tiles = len(TILE_KINDS)
    chunks = PROJ_TILE // HEAD_DIM
    half_chunks = (VD_COL - KD_COL) // HEAD_DIM
    g1_0, g2_0, u_0 = Z_TILES, Z_TILES + G_TILES, Z_TILES + 2 * G_TILES
    n_rot = {PLAIN: 0, ROT_Q: chunks, ROT_K: chunks, ROT_HALF: half_chunks}

    def multiply_tile():
        acc_ref[...] = _dot(h_ref[...], w_ref[...])

    @pl.when(j == 0)
    def _():
        h_ref[...] = _rms(x_ref[...], g_ref[...]).astype(BF)
        multiply_tile()

    def lanes(c):
        return slice(c * HEAD_DIM, (c + 1) * HEAD_DIM)

    def finished(c, tile_kind):
        xc = acc_ref[:, lanes(c)]
        if c >= n_rot[tile_kind]:
            return xc
        rotated = xc * cos_ref[...] + pltpu.roll(xc, HEAD_DIM // 2, 1) * sin_ref[...]
        return rotated * HEAD_DIM ** -0.5 if tile_kind == ROT_Q else rotated

    def store_natural(tile_kind):
        for c in range(chunks):
            z_ref[:, lanes(c)] = finished(c, tile_kind).astype(BF)

    def store_residue_major(dst_ref, dil, tile_kind):
        for c in range(chunks):
            res_ref[c] = finished(c, tile_kind)
        rows = tm // dil
        for r in range(dil):
            for c in range(chunks):
                dst_ref[r, :, lanes(c)] = res_ref[c, pl.ds(r, rows, stride=dil), :].astype(BF)

    t = j - 1
    kind = kind_ref[jnp.maximum(t, 0)]
    destinations = (
        (0, g1_0, (PLAIN, ROT_Q, ROT_K, ROT_HALF), store_natural),
        (g1_0, g2_0, (PLAIN, ROT_Q, ROT_K), functools.partial(store_residue_major, g1_ref, A_DILATIONS[1])),
        (g2_0, u_0, (PLAIN, ROT_Q, ROT_K), functools.partial(store_residue_major, g2_ref, A_DILATIONS[2])),
        (u_0, n_tiles, (PLAIN,), functools.partial(store_residue_major, uc_ref, s1)),
    )
    for lo, hi, kinds, store in destinations:
        for tile_kind in kinds:
            in_range = (t >= lo) & (t < hi) & (kind == tile_kind)

            @pl.when(in_range & (j < n_tiles))
            def _(store=store, tile_kind=tile_kind):
                store(tile_kind)
                multiply_tile()

            if hi == n_tiles:
                @pl.when(in_range & (j == n_tiles))
                def _(store=store, tile_kind=tile_kind):
                    store(tile_kind)


def _proj(lidx, x, g, w, out_dtype=BF, tn=512):
    m, d = x.shape
    n = w.shape[-1]
    tm, tn = min(512, m), min(tn, n)
    return pl.pallas_call(
        _proj_body,
        grid_spec=_grid_spec(
            (m // tm, n // tn),
            [pl.BlockSpec((tm, d), lambda i, j, l: (i, 0)),
             pl.BlockSpec((None, 1, d), lambda i, j, l: (l[0], 0, 0)),
             pl.BlockSpec((None, d, tn), lambda i, j, l: (l[0], 0, j))],
            pl.BlockSpec((tm, tn), lambda i, j, l: (i, j)),
            [pltpu.VMEM((tm, d), BF)]),
        out_shape=jax.ShapeDtypeStruct((m, n), out_dtype),
        compiler_params=_cparams(2),
    )(lidx, x, g, w)


def _proj_in(lidx, x, g, w, cos_t, sin_t, batch, seq, s1):
    m, d = x.shape
    tm, tn = min(PROJ_ROWS, m), PROJ_TILE
    n_tiles = len(TILE_KINDS)
    d1, d2 = A_DILATIONS[1], A_DILATIONS[2]
    assert w.shape[-1] == n_tiles * tn and seq % tm == 0 and tm % (16 * max(d2, s1)) == 0
    nsb = seq // tm
    g1_0, g2_0, u_0 = Z_TILES, Z_TILES + G_TILES, Z_TILES + 2 * G_TILES

    def residue_major(dil, first_tile, tiles):
        return pl.BlockSpec((None, dil, tm // dil, tn),
                            lambda b, s, j, l, k: (b, 0, s, jnp.clip(j - 1 - first_tile, 0, tiles - 1)))

    return pl.pallas_call(
        functools.partial(_proj_in_body, tm=tm, s1=s1),
        grid_spec=pltpu.PrefetchScalarGridSpec(
            num_scalar_prefetch=2,
            grid=(batch, nsb, n_tiles + 1),
            in_specs=[pl.BlockSpec((tm, d), lambda b, s, j, l, k: (b * nsb + s, 0)),
                      pl.BlockSpec((None, 1, d), lambda b, s, j, l, k: (l[0], 0, 0)),
                      pl.BlockSpec((None, d, tn), lambda b, s, j, l, k: (l[0], 0, jnp.minimum(j, n_tiles - 1))),
                      pl.BlockSpec((tm, HEAD_DIM), lambda b, s, j, l, k: (s, 0)),
                      pl.BlockSpec((tm, HEAD_DIM), lambda b, s, j, l, k: (s, 0))],
            out_specs=[pl.BlockSpec((tm, tn),
                                    lambda b, s, j, l, k: (b * nsb + s, jnp.clip(j - 1, 0, Z_TILES - 1))),
                       residue_major(d1, g1_0, G_TILES),
                       residue_major(d2, g2_0, G_TILES),
                       residue_major(s1, u_0, U_TILES)],
            scratch_shapes=[pltpu.VMEM((tm, d), BF), pltpu.VMEM((tm, tn), F32),
                            pltpu.VMEM((tn // HEAD_DIM, tm, HEAD_DIM), F32)]),
        out_shape=[jax.ShapeDtypeStruct((m, Z_WIDTH), BF),
                   jax.ShapeDtypeStruct((batch, d1, seq // d1, G_TILES * tn), BF),
                   jax.ShapeDtypeStruct((batch, d2, seq // d2, G_TILES * tn), BF),
                   jax.ShapeDtypeStruct((batch, s1, seq // s1, U_TILES * tn), BF)],
        compiler_params=_cparams(3),
    )(lidx, jnp.asarray(TILE_KINDS, jnp.int32), x, g, w, cos_t, sin_t)


def _band_attn_body(l_ref, q_ref, kp_ref, kc_ref, kn_ref, vp_ref, vc_ref, vn_ref, *rest,
                    tq, rps, length, radius, n_heads, rep, has_sink, want_lse):
    rest = list(rest)
    sink_ref = rest.pop(0) if has_sink else None
    o_ref = rest.pop(0)
    lse_ref = rest.pop(0) if want_lse else None
    s_ref, p_ref, inv_ref = rest
    per_sequence = [[ref.at[ri] for ref in (q_ref, kp_ref, kc_ref, kn_ref, vp_ref, vc_ref, vn_ref,
                                            o_ref, lse_ref) if ref is not None] for ri in range(rps)]
    i = pl.program_id(2)
    rc = ROW_CHUNK
    nc = tq // rc
    win = rc + 2 * radius
    groups = n_heads // rep
    assert radius <= rc and radius % 16 == 0

    def window(prev_ref, cur_ref, next_ref, c, cols):
        lo, hi = c * rc - radius, (c + 1) * rc + radius
        parts = []
        if lo < 0:
            parts.append(prev_ref[rc - radius:rc, cols])
        parts.append(cur_ref[max(lo, 0):min(hi, tq), cols])
        if hi > tq:
            parts.append(next_ref[0:radius, cols])
        return parts[0] if len(parts) == 1 else jnp.concatenate(parts, axis=0)

    row = lax.broadcasted_iota(jnp.int32, (rc, win), 0)
    col = lax.broadcasted_iota(jnp.int32, (rc, win), 1)
    band_bias = jnp.where(jnp.abs(col - radius - row) <= radius, 0.0, NEG_BIG)
    key_col = lax.broadcasted_iota(jnp.int32, (1, win), 1)
    rb = SOFTMAX_ROWS
    for step, (ri, c) in enumerate((ri, c) for ri in range(rps) for c in range(nc)):
        q_ref, kp_ref, kc_ref, kn_ref, vp_ref, vc_ref, vn_ref, o_ref = per_sequence[ri][:8]
        lse_ref = per_sequence[ri][8] if want_lse else None
        slot = step % 2
        kpos = i * tq + c * rc - radius + key_col
        bias = band_bias + jnp.where(kpos >= 0, jnp.where(kpos < length, 0.0, NEG_BIG), NEG_BIG)
        rows = slice(c * rc, (c + 1) * rc)
        for g in range(groups):
            k = window(kp_ref, kc_ref, kn_ref, c, slice(g * HEAD_DIM, (g + 1) * HEAD_DIM))
            q = q_ref[rows, g * rep * HEAD_DIM:(g + 1) * rep * HEAD_DIM]
            if rep > 1:
                q = jnp.concatenate([q[:, r * HEAD_DIM:(r + 1) * HEAD_DIM] for r in range(rep)], axis=0)
            s_ref[slot, g * rep * rc:(g + 1) * rep * rc, :] = _dot_t(q, k)
        for blk in range(n_heads * rc // rb):
            h, r0 = divmod(blk * rb, rc)
            srows = slice(blk * rb, (blk + 1) * rb)
            s = s_ref[slot, srows, :] + bias[r0:r0 + rb]
            m = jnp.max(s, axis=-1, keepdims=True)
            if has_sink:
                sk = sink_ref[l_ref[0], h]
                m = jnp.maximum(m, sk)
            p = jnp.exp(s - m)
            den = jnp.sum(p, axis=-1, keepdims=True)
            if has_sink:
                den = den + jnp.exp(sk - m)
            p_ref[slot, srows, :] = p.astype(BF)
            inv_ref[slot, srows, :] = jnp.broadcast_to(1.0 / den, (rb, HEAD_DIM))
            if want_lse:
                lse_ref[c * rc + r0:c * rc + r0 + rb, h * HEAD_DIM:(h + 1) * HEAD_DIM] = (
                    jnp.broadcast_to(m + jnp.log(den), (rb, HEAD_DIM)))
        for g in range(groups):
            v = window(vp_ref, vc_ref, vn_ref, c, slice(g * HEAD_DIM, (g + 1) * HEAD_DIM))
            grows = slice(g * rep * rc, (g + 1) * rep * rc)
            o = _dot(p_ref[slot, grows, :], v) * inv_ref[slot, grows, :]
            for r in range(rep):
                h = g * rep + r
                o_ref[rows, h * HEAD_DIM:(h + 1) * HEAD_DIM] = o[r * rc:(r + 1) * rc].astype(o_ref.dtype)


def _band_attention(lidx, src, q_col, k_col, v_col, n_heads, rep, radius, sink=None, want_lse=False):
    batch, dil, length, width = src.shape
    tq = min(ATTN_ROWS, length)
    nc = tq // ROW_CHUNK
    qw, kw = n_heads * HEAD_DIM, (n_heads // rep) * HEAD_DIM
    assert length % tq == 0 and tq % ROW_CHUNK == 0
    assert q_col % qw == 0 and k_col % kw == 0 and v_col % kw == 0
    last_chunk = length // ROW_CHUNK - 1
    rps = min(dil, ATTN_ROWS // tq)
    assert dil % rps == 0

    def kv_specs(col):
        cb = col // kw
        return [
            pl.BlockSpec((None, rps, ROW_CHUNK, kw),
                         lambda b, r, i, l: (b, r, jnp.maximum(i * nc - 1, 0), cb)),
            pl.BlockSpec((None, rps, tq, kw), lambda b, r, i, l: (b, r, i, cb)),
            pl.BlockSpec((None, rps, ROW_CHUNK, kw),
                         lambda b, r, i, l: (b, r, jnp.minimum((i + 1) * nc, last_chunk), cb)),
        ]

    in_specs = [pl.BlockSpec((None, rps, tq, qw), lambda b, r, i, l: (b, r, i, q_col // qw))]
    in_specs += kv_specs(k_col) + kv_specs(v_col)
    args = [lidx] + [src] * 7
    if sink is not None:
        in_specs.append(pl.BlockSpec(memory_space=pltpu.SMEM))
        args.append(sink)
    out_spec = pl.BlockSpec((None, rps, tq, qw), lambda b, r, i, l: (b, r, i, 0))
    out_shape = [jax.ShapeDtypeStruct((batch, dil, length, qw), BF)]
    out_specs = [out_spec]
    if want_lse:
        out_shape.append(jax.ShapeDtypeStruct((batch, dil, length, qw), F32))
        out_specs.append(out_spec)
    return pl.pallas_call(
        functools.partial(_band_attn_body, tq=tq, rps=rps, length=length, radius=radius, n_heads=n_heads,
                          rep=rep, has_sink=sink is not None, want_lse=want_lse),
        grid_spec=_grid_spec((batch, dil // rps, length // tq), in_specs, out_specs,
                             [pltpu.VMEM((2, n_heads * ROW_CHUNK, ROW_CHUNK + 2 * radius), F32),
                              pltpu.VMEM((2, n_heads * ROW_CHUNK, ROW_CHUNK + 2 * radius), BF),
                              pltpu.VMEM((2, n_heads * ROW_CHUNK, HEAD_DIM), F32)]),
        out_shape=out_shape,
        compiler_params=_cparams(3),
    )(*args)


def _group_merge_body(l_ref, o0_ref, l0_ref, o1_ref, l1_ref, o2_ref, l2_ref, y_ref, nat_ref, *, tm):
    heads = A_HEADS_PER_GROUP
    for slot, (src_ref, dil) in enumerate(((o1_ref, A_DILATIONS[1]), (l1_ref, A_DILATIONS[1]),
                                           (o2_ref, A_DILATIONS[2]), (l2_ref, A_DILATIONS[2]))):
        rows = tm // dil
        for r in range(dil):
            for h in range(heads):
                nat_ref[slot * heads + h, pl.ds(r, rows, stride=dil), :] = (
                    src_ref[r, :, h * HEAD_DIM:(h + 1) * HEAD_DIM].astype(F32))
    for h in range(heads):
        hc = slice(h * HEAD_DIM, (h + 1) * HEAD_DIM)
        o1, l1, o2, l2 = (nat_ref[slot * heads + h] for slot in range(4))
        l0 = l0_ref[:, hc]
        m = jnp.maximum(jnp.maximum(l0, l1), l2)
        e0, e1, e2 = jnp.exp(l0 - m), jnp.exp(l1 - m), jnp.exp(l2 - m)
        num = e0 * o0_ref[:, hc].astype(F32) + e1 * o1 + e2 * o2
        y_ref[:, hc] = (num / (e0 + e1 + e2)).astype(y_ref.dtype)


def _group_merge(lidx, outs, lses, seq):
    batch, _, _, w = outs[0].shape
    m = batch * seq
    tm = min(1024, seq)
    nsb = seq // tm
    assert tm % (16 * A_DILATIONS[2]) == 0

    def spec(dil):
        return pl.BlockSpec((None, dil, tm // dil, w), lambda b, s, l: (b, 0, s, 0))

    args = []
    in_specs = []
    for o, lse, dil in zip(outs, lses, A_DILATIONS):
        args += [o, lse]
        in_specs += [spec(dil)] * 2
    in_specs[0] = in_specs[1] = pl.BlockSpec((None, None, tm, w), lambda b, s, l: (b, 0, s, 0))
    return pl.pallas_call(
        functools.partial(_group_merge_body, tm=tm),
        grid_spec=_grid_spec((batch, nsb), in_specs, pl.BlockSpec((tm, w), lambda b, s, l: (b * nsb + s, 0)),
                             [pltpu.VMEM((4 * A_HEADS_PER_GROUP, tm, HEAD_DIM), F32)]),
        out_shape=jax.ShapeDtypeStruct((m, w), BF),
        compiler_params=_cparams(2),
    )(lidx, *args)


def _conv_body(l_ref, bg_ref, cg_ref, xb_ref, cgp_ref, xbp_ref, cgn_ref, xbn_ref, w_ref, o_ref,
               *, tm, seq):
    i = pl.program_id(0)
    u = cg_ref[...].astype(F32) * xb_ref[...].astype(F32)
    not_first = ((i * tm) % seq != 0).astype(F32)
    not_last = (((i + 1) * tm) % seq != 0).astype(F32)
    u_prev = cgp_ref[7:8, :].astype(F32) * xbp_ref[7:8, :].astype(F32) * not_first
    u_next = cgn_ref[0:1, :].astype(F32) * xbn_ref[0:1, :].astype(F32) * not_last
    row = lax.broadcasted_iota(jnp.int32, u.shape, 0)
    below = jnp.where(row == 0, u_prev, pltpu.roll(u, 1, 0))
    above = jnp.where(row == tm - 1, u_next, pltpu.roll(u, tm - 1, 0))
    y = below * w_ref[0:1, :] + u * w_ref[1:2, :] + above * w_ref[2:3, :]
    o_ref[...] = (bg_ref[...].astype(F32) * y).astype(o_ref.dtype)


def _short_conv(lidx, z, w_conv, seq):
    m = z.shape[0]
    tm, tc = min(1024, m), B_WIDTH
    halo = 8
    assert seq % tm == 0 and B_WIDTH % tc == 0
    last_halo = m // halo - 1

    def tile(col):
        return pl.BlockSpec((tm, tc), lambda i, j, l: (i, col // tc + j))

    def prev(col):
        return pl.BlockSpec((halo, tc), lambda i, j, l: (jnp.maximum(i * (tm // halo) - 1, 0), col // tc + j))

    def nxt(col):
        return pl.BlockSpec((halo, tc),
                            lambda i, j, l: (jnp.minimum((i + 1) * (tm // halo), last_halo), col // tc + j))

    return pl.pallas_call(
        functools.partial(_conv_body, tm=tm, seq=seq),
        grid_spec=_grid_spec(
            (m // tm, B_WIDTH // tc),
            [tile(BG_COL), tile(CG_COL), tile(XB_COL), prev(CG_COL), prev(XB_COL), nxt(CG_COL), nxt(XB_COL),
             pl.BlockSpec((None, 3, tc), lambda i, j, l: (l[0], 0, j))],
            pl.BlockSpec((tm, tc), lambda i, j, l: (i, j))),
        out_shape=jax.ShapeDtypeStruct((m, B_WIDTH), BF),
        compiler_params=_cparams(2),
    )(lidx, z, z, z, z, z, z, z, w_conv)


def _dft1_body(l_ref, u_ref, g_ref, ar_ref, ai_ref, *, tn1, tk2, s2, groups):
    for jj in range(tn1):
        r = _dot(g_ref[jj], u_ref[jj])
        for g in range(groups):
            gc = slice(g * C_GROUP_DIM, (g + 1) * C_GROUP_DIM)
            for t in range(s2 // tk2):
                dst = slice(jj * tk2, (jj + 1) * tk2)
                ar_ref[g, t, dst, :] = r[t * tk2:(t + 1) * tk2, gc]
                ai_ref[g, t, dst, :] = r[s2 + t * tk2:s2 + (t + 1) * tk2, gc]


def _dft2_body(l_ref, ar_ref, ai_ref, m2_ref, cc_ref, sc_ref, y_ref, xr_ref, xi_ref,
               *, tk2, s1, groups, scale):
    for j in range(tk2):
        rows = pl.ds(j, s1, stride=tk2)
        a_re = jnp.concatenate([ar_ref[g, rows, :] for g in range(groups)], axis=1)
        a_im = jnp.concatenate([ai_ref[g, rows, :] for g in range(groups)], axis=1)
        x = _dot(m2_ref[...], jnp.concatenate([a_re, a_im], axis=0).astype(BF))
        for g in range(groups):
            gc = slice(g * C_GROUP_DIM, (g + 1) * C_GROUP_DIM)
            xr_ref[g, rows, :] = x[:s1, gc]
            xi_ref[g, rows, :] = x[s1:, gc]
    for g in range(groups):
        gc = slice(g * C_GROUP_DIM, (g + 1) * C_GROUP_DIM)
        yg = (_dot(xr_ref[g].astype(BF), cc_ref[...]) + _dot(xi_ref[g].astype(BF), sc_ref[...])) * scale
        for k1 in range(s1):
            y_ref[k1, :, gc] = yg[k1 * tk2:(k1 + 1) * tk2].astype(y_ref.dtype)


def _fourier_mix(lidx, u, tables):
    g1, m2, cc, sc = tables
    batch, s1, s2, width = u.shape
    groups = width // C_GROUP_DIM
    tn1, tk2 = 4, 16
    a_shape = jax.ShapeDtypeStruct((batch, groups, s2 // tk2, s1 * tk2, C_GROUP_DIM), F32)
    a_out = pl.BlockSpec((None, groups, s2 // tk2, tn1 * tk2, C_GROUP_DIM), lambda b, t, l: (b, 0, 0, t, 0))
    ar, ai = pl.pallas_call(
        functools.partial(_dft1_body, tn1=tn1, tk2=tk2, s2=s2, groups=groups),
        grid_spec=_grid_spec(
            (batch, s1 // tn1),
            [pl.BlockSpec((None, tn1, s2, width), lambda b, t, l: (b, t, 0, 0)),
             pl.BlockSpec((tn1, 2 * s2, s2), lambda b, t, l: (t, 0, 0))],
            [a_out, a_out]),
        out_shape=[a_shape, a_shape],
        compiler_params=_cparams(2),
    )(lidx, u, g1)
    a_in = pl.BlockSpec((None, groups, None, s1 * tk2, C_GROUP_DIM), lambda b, t, l: (b, 0, t, 0, 0))
    y = pl.pallas_call(
        functools.partial(_dft2_body, tk2=tk2, s1=s1, groups=groups,
                          scale=float((s1 * s2 * C_GROUP_DIM) ** -0.5)),
        grid_spec=_grid_spec(
            (batch, s2 // tk2),
            [a_in, a_in,
             pl.BlockSpec((2 * s1, 2 * s1), lambda b, t, l: (0, 0)),
             pl.BlockSpec((C_GROUP_DIM, C_GROUP_DIM), lambda b, t, l: (0, 0)),
             pl.BlockSpec((C_GROUP_DIM, C_GROUP_DIM), lambda b, t, l: (0, 0))],
            pl.BlockSpec((None, s1, tk2, width), lambda b, t, l: (b, 0, t, 0)),
            [pltpu.VMEM((groups, tk2 * s1, C_GROUP_DIM), F32)] * 2),
        out_shape=jax.ShapeDtypeStruct((batch, s1, s2, width), BF),
        compiler_params=_cparams(2),
    )(lidx, ar, ai, m2, cc, sc)
    return y.reshape(batch * s1 * s2, width)


def _dft_split(seq):
    s2 = 256 if seq >= 8192 else 128
    return seq // s2, s2


def _dft_tables(seq):
    s1, s2 = _dft_split(seq)
    two_pi = 2.0 * jnp.pi

    def cs(idx, period):
        ang = (idx % period).astype(F32) * (two_pi / period)
        return jnp.cos(ang), jnp.sin(ang)

    n1 = jnp.arange(s1, dtype=jnp.int32)[:, None, None]
    k2 = jnp.arange(s2, dtype=jnp.int32)[None, :, None]
    n2 = jnp.arange(s2, dtype=jnp.int32)[None, None, :]
    ca, sa = cs(k2 * (n1 + s1 * n2), seq)
    g1 = jnp.concatenate([ca, -sa], axis=1).astype(BF)
    a = jnp.arange(s1, dtype=jnp.int32)
    cb, sb = cs(a[:, None] * a[None, :], s1)
    m2 = jnp.concatenate([jnp.concatenate([cb, sb], axis=1),
                          jnp.concatenate([-sb, cb], axis=1)], axis=0).astype(BF)
    c = jnp.arange(C_GROUP_DIM, dtype=jnp.int32)
    cc, sc = cs(c[:, None] * c[None, :], C_GROUP_DIM)
    return g1, m2, cc.astype(BF), sc.astype(BF)


def _memory_update(x, g_ref, wq_ref, kv_ref, wo_ref, att_ref):
    h = _rms(x, g_ref[...]).astype(BF)
    q = (_dot(h, wq_ref[...]) * HEAD_DIM ** -0.5).astype(BF)
    mw = M_HEADS * HEAD_DIM
    for hh in range(M_HEADS):
        cols = slice(hh * HEAD_DIM, (hh + 1) * HEAD_DIM)
        k = kv_ref[:, cols]
        v = kv_ref[:, mw + hh * HEAD_DIM:mw + (hh + 1) * HEAD_DIM]
        s = _dot_t(q[:, cols], k)
        p = jnp.exp(s - jnp.max(s, axis=-1, keepdims=True))
        o = _dot(p.astype(BF), v) / jnp.sum(p, axis=-1, keepdims=True)
        att_ref[:, cols] = o.astype(BF)
    return x + _dot(att_ref[...], wo_ref[...])


def _merge_body(l_ref, x_ref, g_ref, wgate_ref, bgate_ref, ya_ref, yd_ref, yb_ref, yc_ref,
                wpa_ref, wpd_ref, wpb_ref, wpc_ref, wo_ref, gq_ref, wq_ref, kv_ref, wmo_ref,
                o_ref, h_ref, att_ref, *, tn):
    n = pl.program_id(1)

    def step(first):
        if first:
            h_ref[...] = _rms(x_ref[...], g_ref[...]).astype(BF)
        gates = jax.nn.sigmoid(_dot(h_ref[...], wgate_ref[...]) + bgate_ref[...])
        merged = None
        for i, (br_ref, wp_ref) in enumerate(((ya_ref, wpa_ref), (yd_ref, wpd_ref),
                                              (yb_ref, wpb_ref), (yc_ref, wpc_ref))):
            term = gates[:, i * tn:(i + 1) * tn] * _dot(br_ref[...], wp_ref[...])
            merged = term if merged is None else merged + term
        o_ref[...] = (x_ref[...] if first else o_ref[...]) + _dot(merged.astype(BF), wo_ref[...])

    pl.when(n == 0)(functools.partial(step, True))
    pl.when(n > 0)(functools.partial(step, False))

    @pl.when(n == pl.num_programs(1) - 1)
    def _():
        o_ref[...] = _memory_update(o_ref[...], gq_ref, wq_ref, kv_ref, wmo_ref, att_ref)


def _merge(lidx, x, g, w_gate, b_gate, branches, projs, w_o, g_memq, w_mq, kv, w_mo, seq):
    m, d = x.shape
    tm = min(512, m)
    tn = w_gate.shape[-1] // 4
    n_mem, kvw = kv.shape[1], kv.shape[2]
    mw = w_mq.shape[-1]
    assert seq % tm == 0

    def row_tile(a):
        return pl.BlockSpec((tm, a.shape[1]), lambda i, n, l: (i, 0))

    def proj_tile(w):
        return pl.BlockSpec((None, w.shape[1], tn), lambda i, n, l: (l[0], 0, n))

    return pl.pallas_call(
        functools.partial(_merge_body, tn=tn),
        grid_spec=_grid_spec(
            (m // tm, d // tn),
            [row_tile(x),
             pl.BlockSpec((None, 1, d), lambda i, n, l: (l[0], 0, 0)),
             pl.BlockSpec((None, None, d, 4 * tn), lambda i, n, l: (l[0], n, 0, 0)),
             pl.BlockSpec((None, None, 1, 4 * tn), lambda i, n, l: (l[0], n, 0, 0))]
            + [row_tile(b) for b in branches] + [proj_tile(w) for w in projs]
            + [pl.BlockSpec((None, tn, d), lambda i, n, l: (l[0], n, 0)),
               pl.BlockSpec((None, 1, d), lambda i, n, l: (l[0], 0, 0)),
               pl.BlockSpec((None, d, mw), lambda i, n, l: (l[0], 0, 0)),
               pl.BlockSpec((None, n_mem, kvw), lambda i, n, l: ((i * tm) // seq, 0, 0)),
               pl.BlockSpec((None, mw, d), lambda i, n, l: (l[0], 0, 0))],
            row_tile(x),
            [pltpu.VMEM((tm, d), BF), pltpu.VMEM((tm, mw), BF)]),
        out_shape=jax.ShapeDtypeStruct((m, d), F32),
        compiler_params=_cparams(2),
    )(lidx, x, g, w_gate, b_gate, *branches, *projs, w_o, g_memq, w_mq, kv, w_mo)


def _final_norm_body(l_ref, x_ref, g_ref, o_ref):
    o_ref[...] = _rms(x_ref[...], g_ref[...])


def _final_norm(lidx, x, g):
    m, d = x.shape
    tm = min(512, m)
    return pl.pallas_call(
        _final_norm_body,
        grid_spec=_grid_spec((m // tm,),
                             [pl.BlockSpec((tm, d), lambda i, l: (i, 0)),
                              pl.BlockSpec((1, d), lambda i, l: (0, 0))],
                             pl.BlockSpec((tm, d), lambda i, l: (i, 0))),
        out_shape=jax.ShapeDtypeStruct((m, d), F32),
        compiler_params=_cparams(1),
    )(lidx, x, g)


def _rope_tables(seq):
    half = HEAD_DIM // 2
    inv = ROPE_THETA ** (-jnp.arange(half, dtype=F32) / half)
    ang = jnp.arange(seq, dtype=jnp.int32).astype(F32)[:, None] * inv[None, :]
    cos, sin = jnp.cos(ang), jnp.sin(ang)
    return jnp.concatenate([cos, cos], axis=1), jnp.concatenate([-sin, sin], axis=1)


def _layer(lidx, x, mem, p, batch, seq, tabs):
    cos_t, sin_t, dft = tabs
    x = _ffn(lidx, x, p["g_ffn1"], p["w_ffn1_gate"], p["w_ffn1_up"], p["w_ffn1_down"])

    s1, _ = _dft_split(seq)
    z, zg1, zg2, uc = _proj_in(lidx, x, p["g_mix"], p["w_in"], cos_t, sin_t, batch, seq, s1)
    z4 = z.reshape(batch, 1, seq, Z_WIDTH)
    outs, lses = [], []
    for src, cols in ((z4, (QA0_COL, KA0_COL, VA0_COL)),
                      (zg1, (0, GROUP_WIDTH, 2 * GROUP_WIDTH)),
                      (zg2, (0, GROUP_WIDTH, 2 * GROUP_WIDTH))):
        o, lse = _band_attention(lidx, src, *cols, A_HEADS_PER_GROUP, 1, A_RADIUS, want_lse=True)
        outs.append(o)
        lses.append(lse)
    ya = _group_merge(lidx, outs, lses, seq)
    (yd,) = _band_attention(lidx, z4, QD_COL, KD_COL, VD_COL, D_Q_HEADS,
                            D_Q_HEADS // D_KV_HEADS, D_RADIUS, sink=p["sink"])
    yd = yd.reshape(batch * seq, D_Q_HEADS * HEAD_DIM)
    yb = _short_conv(lidx, z, p["w_conv"], seq)
    yc = _fourier_mix(lidx, uc, dft)
    kv = _proj(lidx, mem, p["g_memkv"], p["w_mkv"])
    kv = kv.reshape(batch, mem.shape[0] // batch, kv.shape[1])
    x = _merge(lidx, x, p["g_mix"], p["w_gate"], p["b_gate"], (ya, yd, yb, yc),
               (p["w_pa"], p["w_pd"], p["w_pb"], p["w_pc"]), p["w_o"],
               p["g_memq"], p["w_mq"], kv, p["w_mo"], seq)

    x = _ffn(lidx, x, p["g_ffn2"], p["w_ffn2_gate"], p["w_ffn2_up"], p["w_ffn2_down"])
    return x


def _forward(x_prompt, x_sample, mem_prompt, mem_sample, p, g_final):
    depth = p["w_o"].shape[0]
    d = x_prompt.shape[-1]
    trunks = []
    for x, mem in ((x_prompt, mem_prompt), (x_sample, mem_sample)):
        batch, seq = x.shape[0], x.shape[1]
        tabs = _rope_tables(seq) + (_dft_tables(seq),)
        trunks.append((batch, seq, tabs, mem.reshape(-1, d)))

    def body(xs, l):
        lidx = l.reshape(1)
        return tuple(_layer(lidx, x, mem, p, batch, seq, tabs)
                     for x, (batch, seq, tabs, mem) in zip(xs, trunks)), None

    xs = (x_prompt.reshape(-1, d), x_sample.reshape(-1, d))
    xs, _ = lax.scan(body, xs, jnp.arange(depth, dtype=jnp.int32))
    zero = jnp.zeros((1,), jnp.int32)
    gf = g_final.reshape(1, d)
    return tuple(_final_norm(zero, x, gf).reshape(orig.shape)
                 for x, orig in zip(xs, (x_prompt, x_sample)))


def kernel(x_prompt, x_sample, mem_prompt, mem_sample, g_ffn1, w_ffn1_gate, w_ffn1_up, w_ffn1_down, g_mix, w_in, w_conv, sink, w_gate, b_gate, w_pa, w_pd, w_pb, w_pc, w_o, g_memq, g_memkv, w_mq, w_mkv, w_mo, g_ffn2, w_ffn2_gate, w_ffn2_up, w_ffn2_down, g_final):
    depth, d = g_ffn1.shape

    def gain(g):
        return g.reshape(depth, 1, d)

    def cols(lo, hi):
        return w_in[..., lo:hi]

    def head_group(base, gi):
        return cols(base + gi * GROUP_WIDTH, base + (gi + 1) * GROUP_WIDTH)

    qa, ka, va = 0, A_WIDTH, 2 * A_WIDTH
    w_in_perm = jnp.concatenate(
        [head_group(qa, 0), head_group(ka, 0), cols(4608, 6144), head_group(va, 0), cols(6144, 9216),
         head_group(qa, 1), head_group(ka, 1), head_group(va, 1),
         head_group(qa, 2), head_group(ka, 2), head_group(va, 2), cols(9216, 10240)], axis=-1).astype(BF)
    tn = min(MERGE_COLS, d)
    w_gate_slabs = (w_gate.reshape(depth, 4, d, d // tn, tn).transpose(0, 3, 2, 1, 4)
                    .reshape(depth, d // tn, d, 4 * tn).astype(BF))
    b_gate_slabs = (b_gate.reshape(depth, 4, d // tn, tn).transpose(0, 2, 1, 3)
                    .reshape(depth, d // tn, 1, 4 * tn))
    p = {
        "g_ffn1": gain(g_ffn1), "w_ffn1_gate": w_ffn1_gate.astype(BF), "w_ffn1_up": w_ffn1_up.astype(BF),
        "w_ffn1_down": w_ffn1_down.astype(BF),
        "g_mix": gain(g_mix), "w_in": w_in_perm,
        "w_conv": w_conv, "sink": sink,
        "w_gate": w_gate_slabs, "b_gate": b_gate_slabs,
        "w_pa": w_pa.astype(BF), "w_pd": w_pd.astype(BF), "w_pb": w_pb.astype(BF), "w_pc": w_pc.astype(BF),
        "w_o": w_o.astype(BF),
        "g_memq": gain(g_memq), "g_memkv": gain(g_memkv), "w_mq": w_mq.astype(BF),
        "w_mkv": w_mkv.astype(BF), "w_mo": w_mo.astype(BF),
        "g_ffn2": gain(g_ffn2), "w_ffn2_gate": w_ffn2_gate.astype(BF), "w_ffn2_up": w_ffn2_up.astype(BF),
        "w_ffn2_down": w_ffn2_down.astype(BF),
    }
    return _forward(x_prompt, x_sample, mem_prompt, mem_sample, p, g_final)
```

```python
import functools

import jax
import jax.numpy as jnp
from jax import lax
from jax.experimental import pallas as pl
from jax.experimental.pallas import tpu as pltpu

BF = jnp.bfloat16
F32 = jnp.float32

HEAD_DIM = 128
EPS = 1e-6
ROPE_THETA = 10000.0
A_DILATIONS = (1, 4, 16)
A_RADIUS = 64
A_HEADS_PER_GROUP = 4
A_WIDTH = 1536
D_Q_HEADS = 8
D_KV_HEADS = 2
D_RADIUS = 128
B_WIDTH = 1024
C_WIDTH = 1024
C_GROUP_DIM = 128
M_HEADS = 4

QA0_COL, KA0_COL, QD_COL, KD_COL, VD_COL, VA0_COL, BG_COL, CG_COL, XB_COL, Z_WIDTH = (
    0, 512, 1024, 2048, 2304, 2560, 3072, 4096, 5120, 6144)
PROJ_TILE = 512
GROUP_WIDTH = 512
PLAIN, ROT_Q, ROT_K, ROT_HALF = 0, 1, 2, 3
TILE_KINDS = ((ROT_Q, ROT_K, ROT_Q, ROT_Q, ROT_HALF) + (PLAIN,) * 7
              + (ROT_Q, ROT_K, PLAIN) * 2
              + (PLAIN,) * 2)
Z_TILES, G_TILES, U_TILES = 12, 3, 2

ROW_CHUNK = 128
ATTN_ROWS = 1024
SOFTMAX_ROWS = 32
NEG_BIG = -1e30
VMEM_LIMIT_BYTES = 60 * 1024 * 1024
FFN_ROWS = 1024
PROJ_ROWS = 1024
MERGE_COLS = 256


def _cparams(ndim):
    return pltpu.CompilerParams(dimension_semantics=("arbitrary",) * ndim,
                                vmem_limit_bytes=VMEM_LIMIT_BYTES)


def _grid_spec(grid, in_specs, out_specs, scratch=()):
    return pltpu.PrefetchScalarGridSpec(num_scalar_prefetch=1, grid=grid, in_specs=in_specs,
                                        out_specs=out_specs, scratch_shapes=list(scratch))


def _rms(x, g):
    return x * lax.rsqrt(jnp.mean(x * x, axis=-1, keepdims=True) + EPS) * g


def _dot(a, b):
    return jnp.dot(a, b, preferred_element_type=F32)


def _dot_t(a, b):
    return lax.dot_general(a, b, (((1,), (1,)), ((), ())), preferred_element_type=F32)


def _ffn_body(l_ref, x_ref, g_ref, wg_ref, wu_ref, wd_ref, o_ref, h_ref):
    j = pl.program_id(1)

    def step(first):
        if first:
            h_ref[...] = _rms(x_ref[...], g_ref[...]).astype(BF)
        h = h_ref[...]
        gate = _dot(h, wg_ref[...])
        up = _dot(h, wu_ref[...])
        a = (0.5 * gate * jax.nn.sigmoid(gate) * up).astype(BF)
        o_ref[...] = (x_ref[...] if first else o_ref[...]) + _dot(a, wd_ref[...])

    pl.when(j == 0)(functools.partial(step, True))
    pl.when(j > 0)(functools.partial(step, False))


def _ffn(lidx, x, g, wg, wu, wd):
    m, d = x.shape
    f = wg.shape[-1]
    tm, tf = min(FFN_ROWS, m), min(512, f)
    return pl.pallas_call(
        _ffn_body,
        grid_spec=_grid_spec(
            (m // tm, f // tf),
            [pl.BlockSpec((tm, d), lambda i, j, l: (i, 0)),
             pl.BlockSpec((None, 1, d), lambda i, j, l: (l[0], 0, 0)),
             pl.BlockSpec((None, d, tf), lambda i, j, l: (l[0], 0, j)),
             pl.BlockSpec((None, d, tf), lambda i, j, l: (l[0], 0, j)),
             pl.BlockSpec((None, tf, d), lambda i, j, l: (l[0], j, 0))],
            pl.BlockSpec((tm, d), lambda i, j, l: (i, 0)),
            [pltpu.VMEM((tm, d), BF)]),
        out_shape=jax.ShapeDtypeStruct((m, d), F32),
        compiler_params=_cparams(2),
    )(lidx, x, g, wg, wu, wd)


def _proj_body(l_ref, x_ref, g_ref, w_ref, o_ref, h_ref):
    @pl.when(pl.program_id(1) == 0)
    def _():
        h_ref[...] = _rms(x_ref[...], g_ref[...]).astype(BF)

    o_ref[...] = _dot(h_ref[...], w_ref[...]).astype(o_ref.dtype)


def _proj_in_body(l_ref, kind_ref, x_ref, g_ref, w_ref, cos_ref, sin_ref,
                  z_ref, g1_ref, g2_ref, uc_ref, h_ref, acc_ref, res_ref, *, tm, s1):
    j = pl.program_id(2)
    n_tiles = len(TILE_KINDS)
    chunks = PROJ_TILE // HEAD_DIM
    half_chunks = (VD_COL - KD_COL) // HEAD_DIM
    g1_0, g2_0, u_0 = Z_TILES, Z_TILES + G_TILES, Z_TILES + 2 * G_TILES
    n_rot = {PLAIN: 0, ROT_Q: chunks, ROT_K: chunks, ROT_HALF: half_chunks}

    def multiply_tile():
        acc_ref[...] = _dot(h_ref[...], w_ref[...])

    @pl.when(j == 0)
    def _():
        h_ref[...] = _rms(x_ref[...], g_ref[...]).astype(BF)
        multiply_tile()

    def lanes(c):
        return slice(c * HEAD_DIM, (c + 1) * HEAD_DIM)

    def finished(c, tile_kind):
        xc = acc_ref[:, lanes(c)]
        if c >= n_rot[tile_kind]:
            return xc
        rotated = xc * cos_ref[...] + pltpu.roll(xc, HEAD_DIM // 2, 1) * sin_ref[...]
        return rotated * HEAD_DIM ** -0.5 if tile_kind == ROT_Q else rotated

    def store_natural(tile_kind):
        for c in range(chunks):
            z_ref[:, lanes(c)] = finished(c, tile_kind).astype(BF)

    def store_residue_major(dst_ref, dil, tile_kind):
        for c in range(chunks):
            res_ref[c] = finished(c, tile_kind)
        rows = tm // dil
        for r in range(dil):
            for c in range(chunks):
                dst_ref[r, :, lanes(c)] = res_ref[c, pl.ds(r, rows, stride=dil), :].astype(BF)

    t = j - 1
    kind = kind_ref[jnp.maximum(t, 0)]
    destinations = (
        (0, g1_0, (PLAIN, ROT_Q, ROT_K, ROT_HALF), store_natural),
        (g1_0, g2_0, (PLAIN, ROT_Q, ROT_K), functools.partial(store_residue_major, g1_ref, A_DILATIONS[1])),
        (g2_0, u_0, (PLAIN, ROT_Q, ROT_K), functools.partial(store_residue_major, g2_ref, A_DILATIONS[2])),
        (u_0, n_tiles, (PLAIN,), functools.partial(store_residue_major, uc_ref, s1)),
    )
    for lo, hi, kinds, store in destinations:
        for tile_kind in kinds:
            in_range = (t >= lo) & (t < hi) & (kind == tile_kind)

            @pl.when(in_range & (j < n_tiles))
            def _(store=store, tile_kind=tile_kind):
                store(tile_kind)
                multiply_tile()

            if hi == n_tiles:
                @pl.when(in_range & (j == n_tiles))
                def _(store=store, tile_kind=tile_kind):
                    store(tile_kind)


def _proj(lidx, x, g, w, out_dtype=BF, tn=512):
    m, d = x.shape
    n = w.shape[-1]
    tm, tn = min(512, m), min(tn, n)
    return pl.pallas_call(
        _proj_body,
        grid_spec=_grid_spec(
            (m // tm, n // tn),
            [pl.BlockSpec((tm, d), lambda i, j, l: (i, 0)),
             pl.BlockSpec((None, 1, d), lambda i, j, l: (l[0], 0, 0)),
             pl.BlockSpec((None, d, tn), lambda i, j, l: (l[0], 0, j))],
            pl.BlockSpec((tm, tn), lambda i, j, l: (i, j)),
            [pltpu.VMEM((tm, d), BF)]),
        out_shape=jax.ShapeDtypeStruct((m, n), out_dtype),
        compiler_params=_cparams(2),
    )(lidx, x, g, w)


def _proj_in(lidx, x, g, w, cos_t, sin_t, batch, seq, s1):
    m, d = x.shape
    tm, tn = min(PROJ_ROWS, m), PROJ_TILE
    n_tiles = len(TILE_KINDS)
    d1, d2 = A_DILATIONS[1], A_DILATIONS[2]
    assert w.shape[1:] == (n_tiles, d, tn) and seq % tm == 0 and tm % (16 * max(d2, s1)) == 0
    nsb = seq // tm
    g1_0, g2_0, u_0 = Z_TILES, Z_TILES + G_TILES, Z_TILES + 2 * G_TILES

    def residue_major(dil, first_tile, tiles):
        return pl.BlockSpec((None, dil, tm // dil, tn),
                            lambda b, s, j, l, k: (b, 0, s, jnp.clip(j - 1 - first_tile, 0, tiles - 1)))

    return pl.pallas_call(
        functools.partial(_proj_in_body, tm=tm, s1=s1),
        grid_spec=pltpu.PrefetchScalarGridSpec(
            num_scalar_prefetch=2,
            grid=(batch, nsb, n_tiles + 1),
            in_specs=[pl.BlockSpec((tm, d), lambda b, s, j, l, k: (b * nsb + s, 0)),
                      pl.BlockSpec((None, 1, d), lambda b, s, j, l, k: (l[0], 0, 0)),
                      pl.BlockSpec((None, None, d, tn),
                                   lambda b, s, j, l, k: (l[0], jnp.minimum(j, n_tiles - 1), 0, 0)),
                      pl.BlockSpec((tm, HEAD_DIM), lambda b, s, j, l, k: (s, 0)),
                      pl.BlockSpec((tm, HEAD_DIM), lambda b, s, j, l, k: (s, 0))],
            out_specs=[pl.BlockSpec((tm, tn),
                                    lambda b, s, j, l, k: (b * nsb + s, jnp.clip(j - 1, 0, Z_TILES - 1))),
                       residue_major(d1, g1_0, G_TILES),
                       residue_major(d2, g2_0, G_TILES),
                       residue_major(s1, u_0, U_TILES)],
            scratch_shapes=[pltpu.VMEM((tm, d), BF), pltpu.VMEM((tm, tn), F32),
                            pltpu.VMEM((tn // HEAD_DIM, tm, HEAD_DIM), F32)]),
        out_shape=[jax.ShapeDtypeStruct((m, Z_WIDTH), BF),
                   jax.ShapeDtypeStruct((batch, d1, seq // d1, G_TILES * tn), BF),
                   jax.ShapeDtypeStruct((batch, d2, seq // d2, G_TILES * tn), BF),
                   jax.ShapeDtypeStruct((batch, s1, seq // s1, U_TILES * tn), BF)],
        compiler_params=_cparams(3),
    )(lidx, jnp.asarray(TILE_KINDS, jnp.int32), x, g, w, cos_t, sin_t)


def _band_attn_body(l_ref, q_ref, kp_ref, kc_ref, kn_ref, vp_ref, vc_ref, vn_ref, *rest,
                    tq, rps, length, radius, n_heads, rep, has_sink, want_lse):
    rest = list(rest)
    sink_ref = rest.pop(0) if has_sink else None
    o_ref = rest.pop(0)
    lse_ref = rest.pop(0) if want_lse else None
    s_ref, p_ref, inv_ref = rest
    per_sequence = [[ref.at[ri] for ref in (q_ref, kp_ref, kc_ref, kn_ref, vp_ref, vc_ref, vn_ref,
                                            o_ref, lse_ref) if ref is not None] for ri in range(rps)]
    i = pl.program_id(2)
    rc = ROW_CHUNK
    nc = tq // rc
    win = rc + 2 * radius
    groups = n_heads // rep
    assert radius <= rc and radius % 16 == 0

    def window(prev_ref, cur_ref, next_ref, c, cols):
        lo, hi = c * rc - radius, (c + 1) * rc + radius
        parts = []
        if lo < 0:
            parts.append(prev_ref[rc - radius:rc, cols])
        parts.append(cur_ref[max(lo, 0):min(hi, tq), cols])
        if hi > tq:
            parts.append(next_ref[0:radius, cols])
        return parts[0] if len(parts) == 1 else jnp.concatenate(parts, axis=0)

    row = lax.broadcasted_iota(jnp.int32, (rc, win), 0)
    col = lax.broadcasted_iota(jnp.int32, (rc, win), 1)
    band_bias = jnp.where(jnp.abs(col - radius - row) <= radius, 0.0, NEG_BIG)
    key_col = lax.broadcasted_iota(jnp.int32, (1, win), 1)
    rb = SOFTMAX_ROWS
    for step, (ri, c) in enumerate((ri, c) for ri in range(rps) for c in range(nc)):
        q_ref, kp_ref, kc_ref, kn_ref, vp_ref, vc_ref, vn_ref, o_ref = per_sequence[ri][:8]
        lse_ref = per_sequence[ri][8] if want_lse else None
        slot = step % 2
        kpos = i * tq + c * rc - radius + key_col
        bias = band_bias + jnp.where(kpos >= 0, jnp.where(kpos < length, 0.0, NEG_BIG), NEG_BIG)
        rows = slice(c * rc, (c + 1) * rc)
        for g in range(groups):
            k = window(kp_ref, kc_ref, kn_ref, c, slice(g * HEAD_DIM, (g + 1) * HEAD_DIM))
            q = q_ref[rows, g * rep * HEAD_DIM:(g + 1) * rep * HEAD_DIM]
            if rep > 1:
                q = jnp.concatenate([q[:, r * HEAD_DIM:(r + 1) * HEAD_DIM] for r in range(rep)], axis=0)
            s_ref[slot, g * rep * rc:(g + 1) * rep * rc, :] = _dot_t(q, k)
        for blk in range(n_heads * rc // rb):
            h, r0 = divmod(blk * rb, rc)
            srows = slice(blk * rb, (blk + 1) * rb)
            s = s_ref[slot, srows, :] + bias[r0:r0 + rb]
            m = jnp.max(s, axis=-1, keepdims=True)
            if has_sink:
                sk = sink_ref[l_ref[0], h]
                m = jnp.maximum(m, sk)
            p = jnp.exp(s - m)
            den = jnp.sum(p, axis=-1, keepdims=True)
            if has_sink:
                den = den + jnp.exp(sk - m)
            p_ref[slot, srows, :] = p.astype(BF)
            inv_ref[slot, srows, :] = jnp.broadcast_to(1.0 / den, (rb, HEAD_DIM))
            if want_lse:
                lse_ref[c * rc + r0:c * rc + r0 + rb, h * HEAD_DIM:(h + 1) * HEAD_DIM] = (
                    jnp.broadcast_to(m + jnp.log(den), (rb, HEAD_DIM)))
        for g in range(groups):
            v = window(vp_ref, vc_ref, vn_ref, c, slice(g * HEAD_DIM, (g + 1) * HEAD_DIM))
            grows = slice(g * rep * rc, (g + 1) * rep * rc)
            o = _dot(p_ref[slot, grows, :], v) * inv_ref[slot, grows, :]
            for r in range(rep):
                h = g * rep + r
                o_ref[rows, h * HEAD_DIM:(h + 1) * HEAD_DIM] = o[r * rc:(r + 1) * rc].astype(o_ref.dtype)


def _band_attention(lidx, src, q_col, k_col, v_col, n_heads, rep, radius, sink=None, want_lse=False):
    batch, dil, length, width = src.shape
    tq = min(ATTN_ROWS, length)
    nc = tq // ROW_CHUNK
    qw, kw = n_heads * HEAD_DIM, (n_heads // rep) * HEAD_DIM
    assert length % tq == 0 and tq % ROW_CHUNK == 0
    assert q_col % qw == 0 and k_col % kw == 0 and v_col % kw == 0
    last_chunk = length // ROW_CHUNK - 1
    rps = min(dil, ATTN_ROWS // tq)
    assert dil % rps == 0

    def kv_specs(col):
        cb = col // kw
        return [
            pl.BlockSpec((None, rps, ROW_CHUNK, kw),
                         lambda b, r, i, l: (b, r, jnp.maximum(i * nc - 1, 0), cb)),
            pl.BlockSpec((None, rps, tq, kw), lambda b, r, i, l: (b, r, i, cb)),
            pl.BlockSpec((None, rps, ROW_CHUNK, kw),
                         lambda b, r, i, l: (b, r, jnp.minimum((i + 1) * nc, last_chunk), cb)),
        ]

    in_specs = [pl.BlockSpec((None, rps, tq, qw), lambda b, r, i, l: (b, r, i, q_col // qw))]
    in_specs += kv_specs(k_col) + kv_specs(v_col)
    args = [lidx] + [src] * 7
    if sink is not None:
        in_specs.append(pl.BlockSpec(memory_space=pltpu.SMEM))
        args.append(sink)
    out_spec = pl.BlockSpec((None, rps, tq, qw), lambda b, r, i, l: (b, r, i, 0))
    out_shape = [jax.ShapeDtypeStruct((batch, dil, length, qw), BF)]
    out_specs = [out_spec]
    if want_lse:
        out_shape.append(jax.ShapeDtypeStruct((batch, dil, length, qw), F32))
        out_specs.append(out_spec)
    return pl.pallas_call(
        functools.partial(_band_attn_body, tq=tq, rps=rps, length=length, radius=radius, n_heads=n_heads,
                          rep=rep, has_sink=sink is not None, want_lse=want_lse),
        grid_spec=_grid_spec((batch, dil // rps, length // tq), in_specs, out_specs,
                             [pltpu.VMEM((2, n_heads * ROW_CHUNK, ROW_CHUNK + 2 * radius), F32),
                              pltpu.VMEM((2, n_heads * ROW_CHUNK, ROW_CHUNK + 2 * radius), BF),
                              pltpu.VMEM((2, n_heads * ROW_CHUNK, HEAD_DIM), F32)]),
        out_shape=out_shape,
        compiler_params=_cparams(3),
    )(*args)


def _group_merge_body(l_ref, o0_ref, l0_ref, o1_ref, l1_ref, o2_ref, l2_ref, y_ref, nat_ref, *, tm):
    heads = A_HEADS_PER_GROUP
    for slot, (src_ref, dil) in enumerate(((o1_ref, A_DILATIONS[1]), (l1_ref, A_DILATIONS[1]),
                                           (o2_ref, A_DILATIONS[2]), (l2_ref, A_DILATIONS[2]))):
        rows = tm // dil
        for r in range(dil):
            for h in range(heads):
                nat_ref[slot * heads + h, pl.ds(r, rows, stride=dil), :] = (
                    src_ref[r, :, h * HEAD_DIM:(h + 1) * HEAD_DIM].astype(F32))
    for h in range(heads):
        hc = slice(h * HEAD_DIM, (h + 1) * HEAD_DIM)
        o1, l1, o2, l2 = (nat_ref[slot * heads + h] for slot in range(4))
        l0 = l0_ref[:, hc]
        m = jnp.maximum(jnp.maximum(l0, l1), l2)
        e0, e1, e2 = jnp.exp(l0 - m), jnp.exp(l1 - m), jnp.exp(l2 - m)
        num = e0 * o0_ref[:, hc].astype(F32) + e1 * o1 + e2 * o2
        y_ref[:, hc] = (num / (e0 + e1 + e2)).astype(y_ref.dtype)


def _group_merge(lidx, outs, lses, seq):
    batch, _, _, w = outs[0].shape
    m = batch * seq
    tm = min(1024, seq)
    nsb = seq // tm
    assert tm % (16 * A_DILATIONS[2]) == 0

    def spec(dil):
        return pl.BlockSpec((None, dil, tm // dil, w), lambda b, s, l: (b, 0, s, 0))

    args = []
    in_specs = []
    for o, lse, dil in zip(outs, lses, A_DILATIONS):
        args += [o, lse]
        in_specs += [spec(dil)] * 2
    in_specs[0] = in_specs[1] = pl.BlockSpec((None, None, tm, w), lambda b, s, l: (b, 0, s, 0))
    return pl.pallas_call(
        functools.partial(_group_merge_body, tm=tm),
        grid_spec=_grid_spec((batch, nsb), in_specs, pl.BlockSpec((tm, w), lambda b, s, l: (b * nsb + s, 0)),
                             [pltpu.VMEM((4 * A_HEADS_PER_GROUP, tm, HEAD_DIM), F32)]),
        out_shape=jax.ShapeDtypeStruct((m, w), BF),
        compiler_params=_cparams(2),
    )(lidx, *args)


def _conv_body(l_ref, bg_ref, cg_ref, xb_ref, cgp_ref, xbp_ref, cgn_ref, xbn_ref, w_ref, o_ref,
               *, tm, seq):
    i = pl.program_id(0)
    u = cg_ref[...].astype(F32) * xb_ref[...].astype(F32)
    not_first = ((i * tm) % seq != 0).astype(F32)
    not_last = (((i + 1) * tm) % seq != 0).astype(F32)
    u_prev = cgp_ref[7:8, :].astype(F32) * xbp_ref[7:8, :].astype(F32) * not_first
    u_next = cgn_ref[0:1, :].astype(F32) * xbn_ref[0:1, :].astype(F32) * not_last
    row = lax.broadcasted_iota(jnp.int32, u.shape, 0)
    below = jnp.where(row == 0, u_prev, pltpu.roll(u, 1, 0))
    above = jnp.where(row == tm - 1, u_next, pltpu.roll(u, tm - 1, 0))
    y = below * w_ref[0:1, :] + u * w_ref[1:2, :] + above * w_ref[2:3, :]
    o_ref[...] = (bg_ref[...].astype(F32) * y).astype(o_ref.dtype)


def _short_conv(lidx, z, w_conv, seq):
    m = z.shape[0]
    tm, tc = min(1024, m), B_WIDTH
    halo = 8
    assert seq % tm == 0 and B_WIDTH % tc == 0
    last_halo = m // halo - 1

    def tile(col):
        return pl.BlockSpec((tm, tc), lambda i, j, l: (i, col // tc + j))

    def prev(col):
        return pl.BlockSpec((halo, tc), lambda i, j, l: (jnp.maximum(i * (tm // halo) - 1, 0), col // tc + j))

    def nxt(col):
        return pl.BlockSpec((halo, tc),
                            lambda i, j, l: (jnp.minimum((i + 1) * (tm // halo), last_halo), col // tc + j))

    return pl.pallas_call(
        functools.partial(_conv_body, tm=tm, seq=seq),
        grid_spec=_grid_spec(
            (m // tm, B_WIDTH // tc),
            [tile(BG_COL), tile(CG_COL), tile(XB_COL), prev(CG_COL), prev(XB_COL), nxt(CG_COL), nxt(XB_COL),
             pl.BlockSpec((None, 3, tc), lambda i, j, l: (l[0], 0, j))],
            pl.BlockSpec((tm, tc), lambda i, j, l: (i, j))),
        out_shape=jax.ShapeDtypeStruct((m, B_WIDTH), BF),
        compiler_params=_cparams(2),
    )(lidx, z, z, z, z, z, z, z, w_conv)


def _dft1_body(l_ref, u_ref, g_ref, ar_ref, ai_ref, *, tn1, tk2, s2, groups):
    for jj in range(tn1):
        r = _dot(g_ref[jj], u_ref[jj])
        for g in range(groups):
            gc = slice(g * C_GROUP_DIM, (g + 1) * C_GROUP_DIM)
            for t in range(s2 // tk2):
                dst = slice(jj * tk2, (jj + 1) * tk2)
                ar_ref[g, t, dst, :] = r[t * tk2:(t + 1) * tk2, gc]
                ai_ref[g, t, dst, :] = r[s2 + t * tk2:s2 + (t + 1) * tk2, gc]


def _dft2_body(l_ref, ar_ref, ai_ref, m2_ref, cc_ref, sc_ref, y_ref, xr_ref, xi_ref,
               *, tk2, s1, groups, scale):
    for j in range(tk2):
        rows = pl.ds(j, s1, stride=tk2)
        a_re = jnp.concatenate([ar_ref[g, rows, :] for g in range(groups)], axis=1)
        a_im = jnp.concatenate([ai_ref[g, rows, :] for g in range(groups)], axis=1)
        x = _dot(m2_ref[...], jnp.concatenate([a_re, a_im], axis=0).astype(BF))
        for g in range(groups):
            gc = slice(g * C_GROUP_DIM, (g + 1) * C_GROUP_DIM)
            xr_ref[g, rows, :] = x[:s1, gc]
            xi_ref[g, rows, :] = x[s1:, gc]
    for g in range(groups):
        gc = slice(g * C_GROUP_DIM, (g + 1) * C_GROUP_DIM)
        yg = (_dot(xr_ref[g].astype(BF), cc_ref[...]) + _dot(xi_ref[g].astype(BF), sc_ref[...])) * scale
        for k1 in range(s1):
            y_ref[k1, :, gc] = yg[k1 * tk2:(k1 + 1) * tk2].astype(y_ref.dtype)


def _fourier_mix(lidx, u, tables):
    g1, m2, cc, sc = tables
    batch, s1, s2, width = u.shape
    groups = width // C_GROUP_DIM
    tn1, tk2 = 4, 16
    a_shape = jax.ShapeDtypeStruct((batch, groups, s2 // tk2, s1 * tk2, C_GROUP_DIM), F32)
    a_out = pl.BlockSpec((None, groups, s2 // tk2, tn1 * tk2, C_GROUP_DIM), lambda b, t, l: (b, 0, 0, t, 0))
    ar, ai = pl.pallas_call(
        functools.partial(_dft1_body, tn1=tn1, tk2=tk2, s2=s2, groups=groups),
        grid_spec=_grid_spec(
            (batch, s1 // tn1),
            [pl.BlockSpec((None, tn1, s2, width), lambda b, t, l: (b, t, 0, 0)),
             pl.BlockSpec((tn1, 2 * s2, s2), lambda b, t, l: (t, 0, 0))],
            [a_out, a_out]),
        out_shape=[a_shape, a_shape],
        compiler_params=_cparams(2),
    )(lidx, u, g1)
    a_in = pl.BlockSpec((None, groups, None, s1 * tk2, C_GROUP_DIM), lambda b, t, l: (b, 0, t, 0, 0))
    y = pl.pallas_call(
        functools.partial(_dft2_body, tk2=tk2, s1=s1, groups=groups,
                          scale=float((s1 * s2 * C_GROUP_DIM) ** -0.5)),
        grid_spec=_grid_spec(
            (batch, s2 // tk2),
            [a_in, a_in,
             pl.BlockSpec((2 * s1, 2 * s1), lambda b, t, l: (0, 0)),
             pl.BlockSpec((C_GROUP_DIM, C_GROUP_DIM), lambda b, t, l: (0, 0)),
             pl.BlockSpec((C_GROUP_DIM, C_GROUP_DIM), lambda b, t, l: (0, 0))],
            pl.BlockSpec((None, s1, tk2, width), lambda b, t, l: (b, 0, t, 0)),
            [pltpu.VMEM((groups, tk2 * s1, C_GROUP_DIM), F32)] * 2),
        out_shape=jax.ShapeDtypeStruct((batch, s1, s2, width), BF),
        compiler_params=_cparams(2),
    )(lidx, ar, ai, m2, cc, sc)
    return y.reshape(batch * s1 * s2, width)


def _dft_split(seq):
    s2 = 256 if seq >= 8192 else 128
    return seq // s2, s2


def _dft_tables(seq):
    s1, s2 = _dft_split(seq)
    two_pi = 2.0 * jnp.pi

    def cs(idx, period):
        ang = (idx % period).astype(F32) * (two_pi / period)
        return jnp.cos(ang), jnp.sin(ang)

    n1 = jnp.arange(s1, dtype=jnp.int32)[:, None, None]
    k2 = jnp.arange(s2, dtype=jnp.int32)[None, :, None]
    n2 = jnp.arange(s2, dtype=jnp.int32)[None, None, :]
    ca, sa = cs(k2 * (n1 + s1 * n2), seq)
    g1 = jnp.concatenate([ca, -sa], axis=1).astype(BF)
    a = jnp.arange(s1, dtype=jnp.int32)
    cb, sb = cs(a[:, None] * a[None, :], s1)
    m2 = jnp.concatenate([jnp.concatenate([cb, sb], axis=1),
                          jnp.concatenate([-sb, cb], axis=1)], axis=0).astype(BF)
    c = jnp.arange(C_GROUP_DIM, dtype=jnp.int32)
    cc, sc = cs(c[:, None] * c[None, :], C_GROUP_DIM)
    return g1, m2, cc.astype(BF), sc.astype(BF)


def _memory_update(x, g_ref, wq_ref, kv_ref, wo_ref, att_ref):
    h = _rms(x, g_ref[...]).astype(BF)
    q = (_dot(h, wq_ref[...]) * HEAD_DIM ** -0.5).astype(BF)
    mw = M_HEADS * HEAD_DIM
    for hh in range(M_HEADS):
        cols = slice(hh * HEAD_DIM, (hh + 1) * HEAD_DIM)
        k = kv_ref[:, cols]
        v = kv_ref[:, mw + hh * HEAD_DIM:mw + (hh + 1) * HEAD_DIM]
        s = _dot_t(q[:, cols], k)
        p = jnp.exp(s - jnp.max(s, axis=-1, keepdims=True))
        o = _dot(p.astype(BF), v) / jnp.sum(p, axis=-1, keepdims=True)
        att_ref[:, cols] = o.astype(BF)
    return x + _dot(att_ref[...], wo_ref[...])


def _merge_body(l_ref, x_ref, g_ref, wgate_ref, bgate_ref, ya_ref, yd_ref, yb_ref, yc_ref,
                wpa_ref, wpd_ref, wpb_ref, wpc_ref, wo_ref, gq_ref, wq_ref, kv_ref, wmo_ref,
                o_ref, h_ref, att_ref, *, tn):
    n = pl.program_id(1)

    def step(first):
        if first:
            h_ref[...] = _rms(x_ref[...], g_ref[...]).astype(BF)
        gates = jax.nn.sigmoid(_dot(h_ref[...], wgate_ref[...]) + bgate_ref[...])
        merged = None
        for i, (br_ref, wp_ref) in enumerate(((ya_ref, wpa_ref), (yd_ref, wpd_ref),
                                              (yb_ref, wpb_ref), (yc_ref, wpc_ref))):
            term = gates[:, i * tn:(i + 1) * tn] * _dot(br_ref[...], wp_ref[...])
            merged = term if merged is None else merged + term
        o_ref[...] = (x_ref[...] if first else o_ref[...]) + _dot(merged.astype(BF), wo_ref[...])

    pl.when(n == 0)(functools.partial(step, True))
    pl.when(n > 0)(functools.partial(step, False))

    @pl.when(n == pl.num_programs(1) - 1)
    def _():
        o_ref[...] = _memory_update(o_ref[...], gq_ref, wq_ref, kv_ref, wmo_ref, att_ref)


def _merge(lidx, x, g, w_gate, b_gate, branches, projs, w_o, g_memq, w_mq, kv, w_mo, seq):
    m, d = x.shape
    tm = min(512, m)
    tn = w_gate.shape[-1] // 4
    n_mem, kvw = kv.shape[1], kv.shape[2]
    mw = w_mq.shape[-1]
    assert seq % tm == 0

    def row_tile(a):
        return pl.BlockSpec((tm, a.shape[1]), lambda i, n, l: (i, 0))

    def proj_tile(w):
        return pl.BlockSpec((None, w.shape[1], tn), lambda i, n, l: (l[0], 0, n))

    return pl.pallas_call(
        functools.partial(_merge_body, tn=tn),
        grid_spec=_grid_spec(
            (m // tm, d // tn),
            [row_tile(x),
             pl.BlockSpec((None, 1, d), lambda i, n, l: (l[0], 0, 0)),
             pl.BlockSpec((None, None, d, 4 * tn), lambda i, n, l: (l[0], n, 0, 0)),
             pl.BlockSpec((None, None, 1, 4 * tn), lambda i, n, l: (l[0], n, 0, 0))]
            + [row_tile(b) for b in branches] + [proj_tile(w) for w in projs]
            + [pl.BlockSpec((None, tn, d), lambda i, n, l: (l[0], n, 0)),
               pl.BlockSpec((None, 1, d), lambda i, n, l: (l[0], 0, 0)),
               pl.BlockSpec((None, d, mw), lambda i, n, l: (l[0], 0, 0)),
               pl.BlockSpec((None, n_mem, kvw), lambda i, n, l: ((i * tm) // seq, 0, 0)),
               pl.BlockSpec((None, mw, d), lambda i, n, l: (l[0], 0, 0))],
            row_tile(x),
            [pltpu.VMEM((tm, d), BF), pltpu.VMEM((tm, mw), BF)]),
        out_shape=jax.ShapeDtypeStruct((m, d), F32),
        compiler_params=_cparams(2),
    )(lidx, x, g, w_gate, b_gate, *branches, *projs, w_o, g_memq, w_mq, kv, w_mo)


def _final_norm_body(l_ref, x_ref, g_ref, o_ref):
    o_ref[...] = _rms(x_ref[...], g_ref[...])


def _final_norm(lidx, x, g):
    m, d = x.shape
    tm = min(512, m)
    return pl.pallas_call(
        _final_norm_body,
        grid_spec=_grid_spec((m // tm,),
                             [pl.BlockSpec((tm, d), lambda i, l: (i, 0)),
                              pl.BlockSpec((1, d), lambda i, l: (0, 0))],
                             pl.BlockSpec((tm, d), lambda i, l: (i, 0))),
        out_shape=jax.ShapeDtypeStruct((m, d), F32),
        compiler_params=_cparams(1),
    )(lidx, x, g)


def _cast_body(*refs):
    x_ref, o_ref = refs[-2:]
    o_ref[...] = x_ref[...].astype(o_ref.dtype)


def _cast_w_in_tiles(w_in, source_tiles):
    depth, d, _ = w_in.shape
    n = len(source_tiles)
    return pl.pallas_call(
        _cast_body,
        grid_spec=_grid_spec((depth, n),
                             [pl.BlockSpec((None, d, PROJ_TILE), lambda l, j, t: (l, 0, t[j]))],
                             pl.BlockSpec((None, None, d, PROJ_TILE), lambda l, j, t: (l, j, 0, 0))),
        out_shape=jax.ShapeDtypeStruct((depth, n, d, PROJ_TILE), BF),
        compiler_params=_cparams(2),
    )(jnp.asarray(source_tiles, jnp.int32), w_in)


def _cast_gate_slabs(w_gate, tn):
    depth, n_gates, d, _ = w_gate.shape
    return pl.pallas_call(
        _cast_body,
        grid=(depth, d // tn, n_gates),
        in_specs=[pl.BlockSpec((None, None, d, tn), lambda l, n, i: (l, i, 0, n))],
        out_specs=pl.BlockSpec((None, None, d, tn), lambda l, n, i: (l, n, 0, i)),
        out_shape=jax.ShapeDtypeStruct((depth, d // tn, d, n_gates * tn), BF),
        compiler_params=_cparams(3),
    )(w_gate)


def _rope_tables(seq):
    half = HEAD_DIM // 2
    inv = ROPE_THETA ** (-jnp.arange(half, dtype=F32) / half)
    ang = jnp.arange(seq, dtype=jnp.int32).astype(F32)[:, None] * inv[None, :]
    cos, sin = jnp.cos(ang), jnp.sin(ang)
    return jnp.concatenate([cos, cos], axis=1), jnp.concatenate([-sin, sin], axis=1)


def _layer(lidx, x, mem, p, batch, seq, tabs):
    cos_t, sin_t, dft = tabs
    x = _ffn(lidx, x, p["g_ffn1"], p["w_ffn1_gate"], p["w_ffn1_up"], p["w_ffn1_down"])

    s1, _ = _dft_split(seq)
    z, zg1, zg2, uc = _proj_in(lidx, x, p["g_mix"], p["w_in"], cos_t, sin_t, batch, seq, s1)
    z4 = z.reshape(batch, 1, seq, Z_WIDTH)
    outs, lses = [], []
    for src, cols in ((z4, (QA0_COL, KA0_COL, VA0_COL)),
                      (zg1, (0, GROUP_WIDTH, 2 * GROUP_WIDTH)),
                      (zg2, (0, GROUP_WIDTH, 2 * GROUP_WIDTH))):
        o, lse = _band_attention(lidx, src, *cols, A_HEADS_PER_GROUP, 1, A_RADIUS, want_lse=True)
        outs.append(o)
        lses.append(lse)
    ya = _group_merge(lidx, outs, lses, seq)
    (yd,) = _band_attention(lidx, z4, QD_COL, KD_COL, VD_COL, D_Q_HEADS,
                            D_Q_HEADS // D_KV_HEADS, D_RADIUS, sink=p["sink"])
    yd = yd.reshape(batch * seq, D_Q_HEADS * HEAD_DIM)
    yb = _short_conv(lidx, z, p["w_conv"], seq)
    yc = _fourier_mix(lidx, uc, dft)
    kv = _proj(lidx, mem, p["g_memkv"], p["w_mkv"])
    kv = kv.reshape(batch, mem.shape[0] // batch, kv.shape[1])
    x = _merge(lidx, x, p["g_mix"], p["w_gate"], p["b_gate"], (ya, yd, yb, yc),
               (p["w_pa"], p["w_pd"], p["w_pb"], p["w_pc"]), p["w_o"],
               p["g_memq"], p["w_mq"], kv, p["w_mo"], seq)

    x = _ffn(lidx, x, p["g_ffn2"], p["w_ffn2_gate"], p["w_ffn2_up"], p["w_ffn2_down"])
    return x


def _forward(x_prompt, x_sample, mem_prompt, mem_sample, p, g_final):
    depth = p["w_o"].shape[0]
    d = x_prompt.shape[-1]
    trunks = []
    for x, mem in ((x_prompt, mem_prompt), (x_sample, mem_sample)):
        batch, seq = x.shape[0], x.shape[1]
        tabs = _rope_tables(seq) + (_dft_tables(seq),)
        trunks.append((batch, seq, tabs, mem.reshape(-1, d)))

    def body(xs, l):
        lidx = l.reshape(1)
        return tuple(_layer(lidx, x, mem, p, batch, seq, tabs)
                     for x, (batch, seq, tabs, mem) in zip(xs, trunks)), None

    xs = (x_prompt.reshape(-1, d), x_sample.reshape(-1, d))
    xs, _ = lax.scan(body, xs, jnp.arange(depth, dtype=jnp.int32))
    zero = jnp.zeros((1,), jnp.int32)
    gf = g_final.reshape(1, d)
    return tuple(_final_norm(zero, x, gf).reshape(orig.shape)
                 for x, orig in zip(xs, (x_prompt, x_sample)))


def kernel(x_prompt, x_sample, mem_prompt, mem_sample, g_ffn1, w_ffn1_gate, w_ffn1_up, w_ffn1_down, g_mix, w_in, w_conv, sink, w_gate, b_gate, w_pa, w_pd, w_pb, w_pc, w_o, g_memq, g_memkv, w_mq, w_mkv, w_mo, g_ffn2, w_ffn2_gate, w_ffn2_up, w_ffn2_down, g_final):
    depth, d = g_ffn1.shape

    def gain(g):
        return g.reshape(depth, 1, d)

    qa, ka, va = 0, A_WIDTH // PROJ_TILE, 2 * A_WIDTH // PROJ_TILE
    source_tiles = ((qa, ka, 9, 10, 11, va) + tuple(range(12, 18))
                    + (qa + 1, ka + 1, va + 1, qa + 2, ka + 2, va + 2, 18, 19))
    w_in_tiles = _cast_w_in_tiles(w_in, source_tiles)
    tn = min(MERGE_COLS, d)
    w_gate_slabs = _cast_gate_slabs(w_gate, tn)
    b_gate_slabs = (b_gate.reshape(depth, 4, d // tn, tn).transpose(0, 2, 1, 3)
                    .reshape(depth, d // tn, 1, 4 * tn))
    p = {
        "g_ffn1": gain(g_ffn1), "w_ffn1_gate": w_ffn1_gate.astype(BF), "w_ffn1_up": w_ffn1_up.astype(BF),
        "w_ffn1_down": w_ffn1_down.astype(BF),
        "g_mix": gain(g_mix), "w_in": w_in_tiles,
        "w_conv": w_conv, "sink": sink,
        "w_gate": w_gate_slabs, "b_gate": b_gate_slabs,
        "w_pa": w_pa.astype(BF), "w_pd": w_pd.astype(BF), "w_pb": w_pb.astype(BF), "w_pc": w_pc.astype(BF),
        "w_o": w_o.astype(BF),
        "g_memq": gain(g_memq), "g_memkv": gain(g_memkv), "w_mq": w_mq.astype(BF),
        "w_mkv": w_mkv.astype(BF), "w_mo": w_mo.astype(BF),
        "g_ffn2": gain(g_ffn2), "w_ffn2_gate": w_ffn2_gate.astype(BF), "w_ffn2_up": w_ffn2_up.astype(BF),
        "w_ffn2_down": w_ffn2_down.astype(BF),
    }
    return _forward(x_prompt, x_sample, mem_prompt, mem_sample, p, g_final)
```

```python
import functools

import jax
import jax.numpy as jnp
from jax import lax
from jax.experimental import pallas as pl
from jax.experimental.pallas import tpu as pltpu

BF = jnp.bfloat16
F32 = jnp.float32

HEAD_DIM = 128
EPS = 1e-6
ROPE_THETA = 10000.0
A_DILATIONS = (1, 4, 16)
A_RADIUS = 64
A_HEADS_PER_GROUP = 4
A_WIDTH = 1536
D_Q_HEADS = 8
D_KV_HEADS = 2
D_RADIUS = 128
B_WIDTH = 1024
C_WIDTH = 1024
C_GROUP_DIM = 128
M_HEADS = 4

QA0_COL, KA0_COL, QD_COL, KD_COL, VD_COL, VA0_COL, BG_COL, CG_COL, XB_COL, Z_WIDTH = (
    0, 512, 1024, 2048, 2304, 2560, 3072, 4096, 5120, 6144)
PROJ_TILE = 512
GROUP_WIDTH = 512
PLAIN, ROT_Q, ROT_K, ROT_HALF = 0, 1, 2, 3
TILE_KINDS = ((ROT_Q, ROT_K, ROT_Q, ROT_Q, ROT_HALF) + (PLAIN,) * 7
              + (ROT_Q, ROT_K, PLAIN) * 2
              + (PLAIN,) * 2)
Z_TILES, G_TILES, U_TILES = 12, 3, 2

VMEM_LIMIT_BYTES = 60 * 1024 * 1024
SUBLANES = 8
BF16_SUBLANES = 16

FFN_ROWS = 1024
FFN_COLS = 512
PROJ_ROWS = 1024
MERGE_ROWS = 512
MERGE_COLS = 256
WIDE_ROWS = 1024
ROW_TILE = 512
ATTN_ROWS = 1024
ROW_CHUNK = 128
SOFTMAX_ROWS = 32
DFT_RESIDUES = 4
DFT_K2 = 16
NEG_BIG = -1e30


def _cparams(ndim):
    return pltpu.CompilerParams(dimension_semantics=("arbitrary",) * ndim,
                                vmem_limit_bytes=VMEM_LIMIT_BYTES)


def _grid_spec(grid, in_specs, out_specs, scratch=()):
    return pltpu.PrefetchScalarGridSpec(num_scalar_prefetch=1, grid=grid, in_specs=in_specs,
                                        out_specs=out_specs, scratch_shapes=list(scratch))


def _rms(x, g):
    return x * lax.rsqrt(jnp.mean(x * x, axis=-1, keepdims=True) + EPS) * g


def _dot(a, b):
    return jnp.dot(a, b, preferred_element_type=F32)


def _dot_t(a, b):
    return lax.dot_general(a, b, (((1,), (1,)), ((), ())), preferred_element_type=F32)


def _ffn_body(l_ref, x_ref, g_ref, wg_ref, wu_ref, wd_ref, o_ref, h_ref):
    j = pl.program_id(1)

    def step(first):
        if first:
            h_ref[...] = _rms(x_ref[...], g_ref[...]).astype(BF)
        h = h_ref[...]
        gate = _dot(h, wg_ref[...])
        up = _dot(h, wu_ref[...])
        a = (0.5 * gate * jax.nn.sigmoid(gate) * up).astype(BF)
        o_ref[...] = (x_ref[...] if first else o_ref[...]) + _dot(a, wd_ref[...])

    pl.when(j == 0)(functools.partial(step, True))
    pl.when(j > 0)(functools.partial(step, False))


def _ffn(lidx, x, g, wg, wu, wd):
    m, d = x.shape
    f = wg.shape[-1]
    tm, tf = min(FFN_ROWS, m), min(FFN_COLS, f)
    return pl.pallas_call(
        _ffn_body,
        grid_spec=_grid_spec(
            (m // tm, f // tf),
            [pl.BlockSpec((tm, d), lambda i, j, l: (i, 0)),
             pl.BlockSpec((None, 1, d), lambda i, j, l: (l[0], 0, 0)),
             pl.BlockSpec((None, d, tf), lambda i, j, l: (l[0], 0, j)),
             pl.BlockSpec((None, d, tf), lambda i, j, l: (l[0], 0, j)),
             pl.BlockSpec((None, tf, d), lambda i, j, l: (l[0], j, 0))],
            pl.BlockSpec((tm, d), lambda i, j, l: (i, 0)),
            [pltpu.VMEM((tm, d), BF)]),
        out_shape=jax.ShapeDtypeStruct((m, d), F32),
        compiler_params=_cparams(2),
    )(lidx, x, g, wg, wu, wd)


def _proj_body(l_ref, x_ref, g_ref, w_ref, o_ref, h_ref):
    @pl.when(pl.program_id(1) == 0)
    def _():
        h_ref[...] = _rms(x_ref[...], g_ref[...]).astype(BF)

    o_ref[...] = _dot(h_ref[...], w_ref[...]).astype(o_ref.dtype)


def _proj_in_body(l_ref, kind_ref, x_ref, g_ref, w_ref, cos_ref, sin_ref,
                  z_ref, g1_ref, g2_ref, uc_ref, h_ref, acc_ref, res_ref, *, tm, s1):
    j = pl.program_id(2)
    n_tiles = len(TILE_KINDS)
    chunks = PROJ_TILE // HEAD_DIM
    half_chunks = (VD_COL - KD_COL) // HEAD_DIM
    g1_0, g2_0, u_0 = Z_TILES, Z_TILES + G_TILES, Z_TILES + 2 * G_TILES
    n_rot = {PLAIN: 0, ROT_Q: chunks, ROT_K: chunks, ROT_HALF: half_chunks}

    def multiply_tile():
        acc_ref[...] = _dot(h_ref[...], w_ref[...])

    @pl.when(j == 0)
    def _():
        h_ref[...] = _rms(x_ref[...], g_ref[...]).astype(BF)
        multiply_tile()

    def lanes(c):
        return slice(c * HEAD_DIM, (c + 1) * HEAD_DIM)

    def finished(c, tile_kind):
        xc = acc_ref[:, lanes(c)]
        if c >= n_rot[tile_kind]:
            return xc
        rotated = xc * cos_ref[...] + pltpu.roll(xc, HEAD_DIM // 2, 1) * sin_ref[...]
        return rotated * HEAD_DIM ** -0.5 if tile_kind == ROT_Q else rotated

    def store_natural(tile_kind):
        for c in range(chunks):
            z_ref[:, lanes(c)] = finished(c, tile_kind).astype(BF)

    def store_residue_major(dst_ref, dil, tile_kind):
        for c in range(chunks):
            res_ref[c] = finished(c, tile_kind)
        rows = tm // dil
        for r in range(dil):
            for c in range(chunks):
                dst_ref[r, :, lanes(c)] = res_ref[c, pl.ds(r, rows, stride=dil), :].astype(BF)

    t = j - 1
    kind = kind_ref[jnp.maximum(t, 0)]
    destinations = (
        (0, g1_0, (PLAIN, ROT_Q, ROT_K, ROT_HALF), store_natural),
        (g1_0, g2_0, (PLAIN, ROT_Q, ROT_K), functools.partial(store_residue_major, g1_ref, A_DILATIONS[1])),
        (g2_0, u_0, (PLAIN, ROT_Q, ROT_K), functools.partial(store_residue_major, g2_ref, A_DILATIONS[2])),
        (u_0, n_tiles, (PLAIN,), functools.partial(store_residue_major, uc_ref, s1)),
    )
    for lo, hi, kinds, store in destinations:
        for tile_kind in kinds:
            in_range = (t >= lo) & (t < hi) & (kind == tile_kind)

            @pl.when(in_range & (j < n_tiles))
            def _(store=store, tile_kind=tile_kind):
                store(tile_kind)
                multiply_tile()

            if hi == n_tiles:
                @pl.when(in_range & (j == n_tiles))
                def _(store=store, tile_kind=tile_kind):
                    store(tile_kind)


def _proj(lidx, x, g, w, out_dtype=BF, tn=512):
    m, d = x.shape
    n = w.shape[-1]
    tm, tn = min(ROW_TILE, m), min(tn, n)
    return pl.pallas_call(
        _proj_body,
        grid_spec=_grid_spec(
            (m // tm, n // tn),
            [pl.BlockSpec((tm, d), lambda i, j, l: (i, 0)),
             pl.BlockSpec((None, 1, d), lambda i, j, l: (l[0], 0, 0)),
             pl.BlockSpec((None, d, tn), lambda i, j, l: (l[0], 0, j))],
            pl.BlockSpec((tm, tn), lambda i, j, l: (i, j)),
            [pltpu.VMEM((tm, d), BF)]),
        out_shape=jax.ShapeDtypeStruct((m, n), out_dtype),
        compiler_params=_cparams(2),
    )(lidx, x, g, w)


def _proj_in(lidx, x, g, w, cos_t, sin_t, batch, seq, s1):
    m, d = x.shape
    tm, tn = min(PROJ_ROWS, m), PROJ_TILE
    n_tiles = len(TILE_KINDS)
    d1, d2 = A_DILATIONS[1], A_DILATIONS[2]
    assert w.shape[1:] == (n_tiles, d, tn) and seq % tm == 0 and tm % (BF16_SUBLANES * max(d2, s1)) == 0
    nsb = seq // tm
    g1_0, g2_0, u_0 = Z_TILES, Z_TILES + G_TILES, Z_TILES + 2 * G_TILES

    def residue_major(dil, first_tile, tiles):
        return pl.BlockSpec((None, dil, tm // dil, tn),
                            lambda b, s, j, l, k: (b, 0, s, jnp.clip(j - 1 - first_tile, 0, tiles - 1)))

    return pl.pallas_call(
        functools.partial(_proj_in_body, tm=tm, s1=s1),
        grid_spec=pltpu.PrefetchScalarGridSpec(
            num_scalar_prefetch=2,
            grid=(batch, nsb, n_tiles + 1),
            in_specs=[pl.BlockSpec((tm, d), lambda b, s, j, l, k: (b * nsb + s, 0)),
                      pl.BlockSpec((None, 1, d), lambda b, s, j, l, k: (l[0], 0, 0)),
                      pl.BlockSpec((None, None, d, tn),
                                   lambda b, s, j, l, k: (l[0], jnp.minimum(j, n_tiles - 1), 0, 0)),
                      pl.BlockSpec((tm, HEAD_DIM), lambda b, s, j, l, k: (s, 0)),
                      pl.BlockSpec((tm, HEAD_DIM), lambda b, s, j, l, k: (s, 0))],
            out_specs=[pl.BlockSpec((tm, tn),
                                    lambda b, s, j, l, k: (b * nsb + s, jnp.clip(j - 1, 0, Z_TILES - 1))),
                       residue_major(d1, g1_0, G_TILES),
                       residue_major(d2, g2_0, G_TILES),
                       residue_major(s1, u_0, U_TILES)],
            scratch_shapes=[pltpu.VMEM((tm, d), BF), pltpu.VMEM((tm, tn), F32),
                            pltpu.VMEM((tn // HEAD_DIM, tm, HEAD_DIM), F32)]),
        out_shape=[jax.ShapeDtypeStruct((m, Z_WIDTH), BF),
                   jax.ShapeDtypeStruct((batch, d1, seq // d1, G_TILES * tn), BF),
                   jax.ShapeDtypeStruct((batch, d2, seq // d2, G_TILES * tn), BF),
                   jax.ShapeDtypeStruct((batch, s1, seq // s1, U_TILES * tn), BF)],
        compiler_params=_cparams(3),
    )(lidx, jnp.asarray(TILE_KINDS, jnp.int32), x, g, w, cos_t, sin_t)


def _band_attn_body(l_ref, q_ref, kp_ref, kc_ref, kn_ref, vp_ref, vc_ref, vn_ref, *rest,
                    tq, rps, length, radius, n_heads, rep, has_sink, want_lse):
    rest = list(rest)
    sink_ref = rest.pop(0) if has_sink else None
    o_ref = rest.pop(0)
    lse_ref = rest.pop(0) if want_lse else None
    s_ref, p_ref, inv_ref = rest
    per_sequence = [[ref.at[ri] for ref in (q_ref, kp_ref, kc_ref, kn_ref, vp_ref, vc_ref, vn_ref,
                                            o_ref, lse_ref) if ref is not None] for ri in range(rps)]
    i = pl.program_id(2)
    rc = ROW_CHUNK
    nc = tq // rc
    win = rc + 2 * radius
    groups = n_heads // rep
    assert radius <= rc and radius % 16 == 0

    def window(prev_ref, cur_ref, next_ref, c, cols):
        lo, hi = c * rc - radius, (c + 1) * rc + radius
        parts = []
        if lo < 0:
            parts.append(prev_ref[rc - radius:rc, cols])
        parts.append(cur_ref[max(lo, 0):min(hi, tq), cols])
        if hi > tq:
            parts.append(next_ref[0:radius, cols])
        return parts[0] if len(parts) == 1 else jnp.concatenate(parts, axis=0)

    row = lax.broadcasted_iota(jnp.int32, (rc, win), 0)
    col = lax.broadcasted_iota(jnp.int32, (rc, win), 1)
    band_bias = jnp.where(jnp.abs(col - radius - row) <= radius, 0.0, NEG_BIG)
    key_col = lax.broadcasted_iota(jnp.int32, (1, win), 1)
    rb = SOFTMAX_ROWS
    for step, (ri, c) in enumerate((ri, c) for ri in range(rps) for c in range(nc)):
        q_ref, kp_ref, kc_ref, kn_ref, vp_ref, vc_ref, vn_ref, o_ref = per_sequence[ri][:8]
        lse_ref = per_sequence[ri][8] if want_lse else None
        slot = step % 2
        kpos = i * tq + c * rc - radius + key_col
        bias = band_bias + jnp.where(kpos >= 0, jnp.where(kpos < length, 0.0, NEG_BIG), NEG_BIG)
        rows = slice(c * rc, (c + 1) * rc)
        for g in range(groups):
            k = window(kp_ref, kc_ref, kn_ref, c, slice(g * HEAD_DIM, (g + 1) * HEAD_DIM))
            q = q_ref[rows, g * rep * HEAD_DIM:(g + 1) * rep * HEAD_DIM]
            if rep > 1:
                q = jnp.concatenate([q[:, r * HEAD_DIM:(r + 1) * HEAD_DIM] for r in range(rep)], axis=0)
            s_ref[slot, g * rep * rc:(g + 1) * rep * rc, :] = _dot_t(q, k)
        for blk in range(n_heads * rc // rb):
            h, r0 = divmod(blk * rb, rc)
            srows = slice(blk * rb, (blk + 1) * rb)
            s = s_ref[slot, srows, :] + bias[r0:r0 + rb]
            m = jnp.max(s, axis=-1, keepdims=True)
            if has_sink:
                sk = sink_ref[l_ref[0], h]
                m = jnp.maximum(m, sk)
            p = jnp.exp(s - m)
            den = jnp.sum(p, axis=-1, keepdims=True)
            if has_sink:
                den = den + jnp.exp(sk - m)
            p_ref[slot, srows, :] = p.astype(BF)
            inv_ref[slot, srows, :] = jnp.broadcast_to(1.0 / den, (rb, HEAD_DIM))
            if want_lse:
                lse_ref[c * rc + r0:c * rc + r0 + rb, h * HEAD_DIM:(h + 1) * HEAD_DIM] = (
                    jnp.broadcast_to(m + jnp.log(den), (rb, HEAD_DIM)))
        for g in range(groups):
            v = window(vp_ref, vc_ref, vn_ref, c, slice(g * HEAD_DIM, (g + 1) * HEAD_DIM))
            grows = slice(g * rep * rc, (g + 1) * rep * rc)
            o = _dot(p_ref[slot, grows, :], v) * inv_ref[slot, grows, :]
            for r in range(rep):
                h = g * rep + r
                o_ref[rows, h * HEAD_DIM:(h + 1) * HEAD_DIM] = o[r * rc:(r + 1) * rc].astype(o_ref.dtype)


def _band_attention(lidx, src, q_col, k_col, v_col, n_heads, rep, radius, sink=None, want_lse=False):
    batch, dil, length, width = src.shape
    tq = min(ATTN_ROWS, length)
    nc = tq // ROW_CHUNK
    qw, kw = n_heads * HEAD_DIM, (n_heads // rep) * HEAD_DIM
    assert length % tq == 0 and tq % ROW_CHUNK == 0
    assert q_col % qw == 0 and k_col % kw == 0 and v_col % kw == 0
    last_chunk = length // ROW_CHUNK - 1
    rps = min(dil, ATTN_ROWS // tq)
    assert dil % rps == 0

    def kv_specs(col):
        cb = col // kw
        return [
            pl.BlockSpec((None, rps, ROW_CHUNK, kw),
                         lambda b, r, i, l: (b, r, jnp.maximum(i * nc - 1, 0), cb)),
            pl.BlockSpec((None, rps, tq, kw), lambda b, r, i, l: (b, r, i, cb)),
            pl.BlockSpec((None, rps, ROW_CHUNK, kw),
                         lambda b, r, i, l: (b, r, jnp.minimum((i + 1) * nc, last_chunk), cb)),
        ]

    in_specs = [pl.BlockSpec((None, rps, tq, qw), lambda b, r, i, l: (b, r, i, q_col // qw))]
    in_specs += kv_specs(k_col) + kv_specs(v_col)
    args = [lidx] + [src] * 7
    if sink is not None:
        in_specs.append(pl.BlockSpec(memory_space=pltpu.SMEM))
        args.append(sink)
    out_spec = pl.BlockSpec((None, rps, tq, qw), lambda b, r, i, l: (b, r, i, 0))
    out_shape = [jax.ShapeDtypeStruct((batch, dil, length, qw), BF)]
    out_specs = [out_spec]
    if want_lse:
        out_shape.append(jax.ShapeDtypeStruct((batch, dil, length, qw), F32))
        out_specs.append(out_spec)
    return pl.pallas_call(
        functools.partial(_band_attn_body, tq=tq, rps=rps, length=length, radius=radius, n_heads=n_heads,
                          rep=rep, has_sink=sink is not None, want_lse=want_lse),
        grid_spec=_grid_spec((batch, dil // rps, length // tq), in_specs, out_specs,
                             [pltpu.VMEM((2, n_heads * ROW_CHUNK, ROW_CHUNK + 2 * radius), F32),
                              pltpu.VMEM((2, n_heads * ROW_CHUNK, ROW_CHUNK + 2 * radius), BF),
                              pltpu.VMEM((2, n_heads * ROW_CHUNK, HEAD_DIM), F32)]),
        out_shape=out_shape,
        compiler_params=_cparams(3),
    )(*args)


def _group_merge_body(l_ref, o0_ref, l0_ref, o1_ref, l1_ref, o2_ref, l2_ref, y_ref, nat_ref, *, tm):
    heads = A_HEADS_PER_GROUP
    for slot, (src_ref, dil) in enumerate(((o1_ref, A_DILATIONS[1]), (l1_ref, A_DILATIONS[1]),
                                           (o2_ref, A_DILATIONS[2]), (l2_ref, A_DILATIONS[2]))):
        rows = tm // dil
        for r in range(dil):
            for h in range(heads):
                nat_ref[slot * heads + h, pl.ds(r, rows, stride=dil), :] = (
                    src_ref[r, :, h * HEAD_DIM:(h + 1) * HEAD_DIM].astype(F32))
    for h in range(heads):
        hc = slice(h * HEAD_DIM, (h + 1) * HEAD_DIM)
        o1, l1, o2, l2 = (nat_ref[slot * heads + h] for slot in range(4))
        l0 = l0_ref[:, hc]
        m = jnp.maximum(jnp.maximum(l0, l1), l2)
        e0, e1, e2 = jnp.exp(l0 - m), jnp.exp(l1 - m), jnp.exp(l2 - m)
        num = e0 * o0_ref[:, hc].astype(F32) + e1 * o1 + e2 * o2
        y_ref[:, hc] = (num / (e0 + e1 + e2)).astype(y_ref.dtype)


def _group_merge(lidx, outs, lses, seq):
    batch, _, _, w = outs[0].shape
    m = batch * seq
    tm = min(WIDE_ROWS, seq)
    nsb = seq // tm
    assert tm % (BF16_SUBLANES * A_DILATIONS[2]) == 0

    def spec(dil):
        return pl.BlockSpec((None, dil, tm // dil, w), lambda b, s, l: (b, 0, s, 0))

    args = []
    in_specs = []
    for o, lse, dil in zip(outs, lses, A_DILATIONS):
        args += [o, lse]
        in_specs += [spec(dil)] * 2
    in_specs[0] = in_specs[1] = pl.BlockSpec((None, None, tm, w), lambda b, s, l: (b, 0, s, 0))
    return pl.pallas_call(
        functools.partial(_group_merge_body, tm=tm),
        grid_spec=_grid_spec((batch, nsb), in_specs, pl.BlockSpec((tm, w), lambda b, s, l: (b * nsb + s, 0)),
                             [pltpu.VMEM((4 * A_HEADS_PER_GROUP, tm, HEAD_DIM), F32)]),
        out_shape=jax.ShapeDtypeStruct((m, w), BF),
        compiler_params=_cparams(2),
    )(lidx, *args)


def _conv_body(l_ref, bg_ref, cg_ref, xb_ref, cgp_ref, xbp_ref, cgn_ref, xbn_ref, w_ref, o_ref,
               *, tm, seq):
    i = pl.program_id(0)
    u = cg_ref[...].astype(F32) * xb_ref[...].astype(F32)
    not_first = ((i * tm) % seq != 0).astype(F32)
    not_last = (((i + 1) * tm) % seq != 0).astype(F32)
    last = SUBLANES - 1
    u_prev = cgp_ref[last:, :].astype(F32) * xbp_ref[last:, :].astype(F32) * not_first
    u_next = cgn_ref[0:1, :].astype(F32) * xbn_ref[0:1, :].astype(F32) * not_last
    row = lax.broadcasted_iota(jnp.int32, u.shape, 0)
    below = jnp.where(row == 0, u_prev, pltpu.roll(u, 1, 0))
    above = jnp.where(row == tm - 1, u_next, pltpu.roll(u, tm - 1, 0))
    y = below * w_ref[0:1, :] + u * w_ref[1:2, :] + above * w_ref[2:3, :]
    o_ref[...] = (bg_ref[...].astype(F32) * y).astype(o_ref.dtype)


def _short_conv(lidx, z, w_conv, seq):
    m = z.shape[0]
    tm, tc = min(WIDE_ROWS, m), B_WIDTH
    halo = SUBLANES
    assert seq % tm == 0 and B_WIDTH % tc == 0
    last_halo = m // halo - 1

    def tile(col):
        return pl.BlockSpec((tm, tc), lambda i, j, l: (i, col // tc + j))

    def prev(col):
        return pl.BlockSpec((halo, tc), lambda i, j, l: (jnp.maximum(i * (tm // halo) - 1, 0), col // tc + j))

    def nxt(col):
        return pl.BlockSpec((halo, tc),
                            lambda i, j, l: (jnp.minimum((i + 1) * (tm // halo), last_halo), col // tc + j))

    return pl.pallas_call(
        functools.partial(_conv_body, tm=tm, seq=seq),
        grid_spec=_grid_spec(
            (m // tm, B_WIDTH // tc),
            [tile(BG_COL), tile(CG_COL), tile(XB_COL), prev(CG_COL), prev(XB_COL), nxt(CG_COL), nxt(XB_COL),
             pl.BlockSpec((None, 3, tc), lambda i, j, l: (l[0], 0, j))],
            pl.BlockSpec((tm, tc), lambda i, j, l: (i, j))),
        out_shape=jax.ShapeDtypeStruct((m, B_WIDTH), BF),
        compiler_params=_cparams(2),
    )(lidx, z, z, z, z, z, z, z, w_conv)


def _dft1_body(l_ref, u_ref, g_ref, ar_ref, ai_ref, *, tn1, tk2, s2, groups):
    for jj in range(tn1):
        r = _dot(g_ref[jj], u_ref[jj])
        for g in range(groups):
            gc = slice(g * C_GROUP_DIM, (g + 1) * C_GROUP_DIM)
            for t in range(s2 // tk2):
                dst = slice(jj * tk2, (jj + 1) * tk2)
                ar_ref[g, t, dst, :] = r[t * tk2:(t + 1) * tk2, gc]
                ai_ref[g, t, dst, :] = r[s2 + t * tk2:s2 + (t + 1) * tk2, gc]


def _dft2_body(l_ref, ar_ref, ai_ref, m2_ref, csc_ref, y_ref, xr_ref, xi_ref,
               *, tk2, s1, groups, scale):
    for j in range(tk2):
        rows = pl.ds(j, s1, stride=tk2)
        a_re = jnp.concatenate([ar_ref[g, rows, :] for g in range(groups)], axis=1)
        a_im = jnp.concatenate([ai_ref[g, rows, :] for g in range(groups)], axis=1)
        x = _dot(m2_ref[...], jnp.concatenate([a_re, a_im], axis=0).astype(BF))
        for g in range(groups):
            gc = slice(g * C_GROUP_DIM, (g + 1) * C_GROUP_DIM)
            xr_ref[g, rows, :] = x[:s1, gc]
            xi_ref[g, rows, :] = x[s1:, gc]
    for g in range(groups):
        gc = slice(g * C_GROUP_DIM, (g + 1) * C_GROUP_DIM)
        x_g = jnp.concatenate([xr_ref[g], xi_ref[g]], axis=1).astype(BF)
        yg = _dot(x_g, csc_ref[...]) * scale
        for k1 in range(s1):
            y_ref[k1, :, gc] = yg[k1 * tk2:(k1 + 1) * tk2].astype(y_ref.dtype)


def _fourier_mix(lidx, u, tables):
    g1, m2, csc = tables
    batch, s1, s2, width = u.shape
    groups = width // C_GROUP_DIM
    tn1, tk2 = DFT_RESIDUES, DFT_K2
    a_shape = jax.ShapeDtypeStruct((batch, groups, s2 // tk2, s1 * tk2, C_GROUP_DIM), F32)
    a_out = pl.BlockSpec((None, groups, s2 // tk2, tn1 * tk2, C_GROUP_DIM), lambda b, t, l: (b, 0, 0, t, 0))
    ar, ai = pl.pallas_call(
        functools.partial(_dft1_body, tn1=tn1, tk2=tk2, s2=s2, groups=groups),
        grid_spec=_grid_spec(
            (batch, s1 // tn1),
            [pl.BlockSpec((None, tn1, s2, width), lambda b, t, l: (b, t, 0, 0)),
             pl.BlockSpec((tn1, 2 * s2, s2), lambda b, t, l: (t, 0, 0))],
            [a_out, a_out]),
        out_shape=[a_shape, a_shape],
        compiler_params=_cparams(2),
    )(lidx, u, g1)
    a_in = pl.BlockSpec((None, groups, None, s1 * tk2, C_GROUP_DIM), lambda b, t, l: (b, 0, t, 0, 0))
    y = pl.pallas_call(
        functools.partial(_dft2_body, tk2=tk2, s1=s1, groups=groups,
                          scale=float((s1 * s2 * C_GROUP_DIM) ** -0.5)),
        grid_spec=_grid_spec(
            (batch, s2 // tk2),
            [a_in, a_in,
             pl.BlockSpec((2 * s1, 2 * s1), lambda b, t, l: (0, 0)),
             pl.BlockSpec((2 * C_GROUP_DIM, C_GROUP_DIM), lambda b, t, l: (0, 0))],
            pl.BlockSpec((None, s1, tk2, width), lambda b, t, l: (b, 0, t, 0)),
            [pltpu.VMEM((groups, tk2 * s1, C_GROUP_DIM), F32)] * 2),
        out_shape=jax.ShapeDtypeStruct((batch, s1, s2, width), BF),
        compiler_params=_cparams(2),
    )(lidx, ar, ai, m2, csc)
    return y.reshape(batch * s1 * s2, width)


def _dft_split(seq):
    s2 = 256 if seq >= 8192 else 128
    return seq // s2, s2


def _dft_tables(seq):
    s1, s2 = _dft_split(seq)
    two_pi = 2.0 * jnp.pi

    def cs(idx, period):
        ang = (idx % period).astype(F32) * (two_pi / period)
        return jnp.cos(ang), jnp.sin(ang)

    n1 = jnp.arange(s1, dtype=jnp.int32)[:, None, None]
    k2 = jnp.arange(s2, dtype=jnp.int32)[None, :, None]
    n2 = jnp.arange(s2, dtype=jnp.int32)[None, None, :]
    ca, sa = cs(k2 * (n1 + s1 * n2), seq)
    g1 = jnp.concatenate([ca, -sa], axis=1).astype(BF)
    a = jnp.arange(s1, dtype=jnp.int32)
    cb, sb = cs(a[:, None] * a[None, :], s1)
    m2 = jnp.concatenate([jnp.concatenate([cb, sb], axis=1),
                          jnp.concatenate([-sb, cb], axis=1)], axis=0).astype(BF)
    c = jnp.arange(C_GROUP_DIM, dtype=jnp.int32)
    cc, sc = cs(c[:, None] * c[None, :], C_GROUP_DIM)
    return g1, m2, jnp.concatenate([cc, sc], axis=0).astype(BF)


def _memory_update(x, g_ref, wq_ref, kv_ref, wo_ref, att_ref):
    h = _rms(x, g_ref[...]).astype(BF)
    q = (_dot(h, wq_ref[...]) * HEAD_DIM ** -0.5).astype(BF)
    mw = M_HEADS * HEAD_DIM
    for hh in range(M_HEADS):
        cols = slice(hh * HEAD_DIM, (hh + 1) * HEAD_DIM)
        k = kv_ref[:, cols]
        v = kv_ref[:, mw + hh * HEAD_DIM:mw + (hh + 1) * HEAD_DIM]
        s = _dot_t(q[:, cols], k)
        p = jnp.exp(s - jnp.max(s, axis=-1, keepdims=True))
        o = _dot(p.astype(BF), v) / jnp.sum(p, axis=-1, keepdims=True)
        att_ref[:, cols] = o.astype(BF)
    return x + _dot(att_ref[...], wo_ref[...])


def _merge_body(l_ref, x_ref, g_ref, wgate_ref, bgate_ref, ya_ref, yd_ref, yb_ref, yc_ref,
                wpa_ref, wpd_ref, wpb_ref, wpc_ref, wo_ref, gq_ref, wq_ref, kv_ref, wmo_ref,
                o_ref, h_ref, att_ref, *, tn):
    n = pl.program_id(1)

    def step(first):
        if first:
            h_ref[...] = _rms(x_ref[...], g_ref[...]).astype(BF)
        gates = jax.nn.sigmoid(_dot(h_ref[...], wgate_ref[...]) + bgate_ref[...])
        merged = None
        for i, (br_ref, wp_ref) in enumerate(((ya_ref, wpa_ref), (yd_ref, wpd_ref),
                                              (yb_ref, wpb_ref), (yc_ref, wpc_ref))):
            term = gates[:, i * tn:(i + 1) * tn] * _dot(br_ref[...], wp_ref[...])
            merged = term if merged is None else merged + term
        o_ref[...] = (x_ref[...] if first else o_ref[...]) + _dot(merged.astype(BF), wo_ref[...])

    pl.when(n == 0)(functools.partial(step, True))
    pl.when(n > 0)(functools.partial(step, False))

    @pl.when(n == pl.num_programs(1) - 1)
    def _():
        o_ref[...] = _memory_update(o_ref[...], gq_ref, wq_ref, kv_ref, wmo_ref, att_ref)


def _merge(lidx, x, g, w_gate, b_gate, branches, projs, w_o, g_memq, w_mq, kv, w_mo, seq):
    m, d = x.shape
    tm = min(MERGE_ROWS, m)
    tn = w_gate.shape[-1] // 4
    n_mem, kvw = kv.shape[1], kv.shape[2]
    mw = w_mq.shape[-1]
    assert seq % tm == 0

    def row_tile(a):
        return pl.BlockSpec((tm, a.shape[1]), lambda i, n, l: (i, 0))

    def proj_tile(w):
        return pl.BlockSpec((None, w.shape[1], tn), lambda i, n, l: (l[0], 0, n))

    return pl.pallas_call(
        functools.partial(_merge_body, tn=tn),
        grid_spec=_grid_spec(
            (m // tm, d // tn),
            [row_tile(x),
             pl.BlockSpec((None, 1, d), lambda i, n, l: (l[0], 0, 0)),
             pl.BlockSpec((None, None, d, 4 * tn), lambda i, n, l: (l[0], n, 0, 0)),
             pl.BlockSpec((None, None, 1, 4 * tn), lambda i, n, l: (l[0], n, 0, 0))]
            + [row_tile(b) for b in branches] + [proj_tile(w) for w in projs]
            + [pl.BlockSpec((None, tn, d), lambda i, n, l: (l[0], n, 0)),
               pl.BlockSpec((None, 1, d), lambda i, n, l: (l[0], 0, 0)),
               pl.BlockSpec((None, d, mw), lambda i, n, l: (l[0], 0, 0)),
               pl.BlockSpec((None, n_mem, kvw), lambda i, n, l: ((i * tm) // seq, 0, 0)),
               pl.BlockSpec((None, mw, d), lambda i, n, l: (l[0], 0, 0))],
            row_tile(x),
            [pltpu.VMEM((tm, d), BF), pltpu.VMEM((tm, mw), BF)]),
        out_shape=jax.ShapeDtypeStruct((m, d), F32),
        compiler_params=_cparams(2),
    )(lidx, x, g, w_gate, b_gate, *branches, *projs, w_o, g_memq, w_mq, kv, w_mo)


def _final_norm_body(l_ref, x_ref, g_ref, o_ref):
    o_ref[...] = _rms(x_ref[...], g_ref[...])


def _final_norm(lidx, x, g):
    m, d = x.shape
    tm = min(ROW_TILE, m)
    return pl.pallas_call(
        _final_norm_body,
        grid_spec=_grid_spec((m // tm,),
                             [pl.BlockSpec((tm, d), lambda i, l: (i, 0)),
                              pl.BlockSpec((1, d), lambda i, l: (0, 0))],
                             pl.BlockSpec((tm, d), lambda i, l: (i, 0))),
        out_shape=jax.ShapeDtypeStruct((m, d), F32),
        compiler_params=_cparams(1),
    )(lidx, x, g)


def _cast_body(*refs):
    x_ref, o_ref = refs[-2:]
    o_ref[...] = x_ref[...].astype(o_ref.dtype)


def _cast_w_in_tiles(w_in, source_tiles):
    depth, d, _ = w_in.shape
    n = len(source_tiles)
    return pl.pallas_call(
        _cast_body,
        grid_spec=_grid_spec((depth, n),
                             [pl.BlockSpec((None, d, PROJ_TILE), lambda l, j, t: (l, 0, t[j]))],
                             pl.BlockSpec((None, None, d, PROJ_TILE), lambda l, j, t: (l, j, 0, 0))),
        out_shape=jax.ShapeDtypeStruct((depth, n, d, PROJ_TILE), BF),
        compiler_params=_cparams(2),
    )(jnp.asarray(source_tiles, jnp.int32), w_in)


def _cast_gate_slabs(w_gate, tn):
    depth, n_gates, d, _ = w_gate.shape
    return pl.pallas_call(
        _cast_body,
        grid=(depth, d // tn, n_gates),
        in_specs=[pl.BlockSpec((None, None, d, tn), lambda l, n, i: (l, i, 0, n))],
        out_specs=pl.BlockSpec((None, None, d, tn), lambda l, n, i: (l, n, 0, i)),
        out_shape=jax.ShapeDtypeStruct((depth, d // tn, d, n_gates * tn), BF),
        compiler_params=_cparams(3),
    )(w_gate)


def _rope_tables(seq):
    half = HEAD_DIM // 2
    inv = ROPE_THETA ** (-jnp.arange(half, dtype=F32) / half)
    ang = jnp.arange(seq, dtype=jnp.int32).astype(F32)[:, None] * inv[None, :]
    cos, sin = jnp.cos(ang), jnp.sin(ang)
    return jnp.concatenate([cos, cos], axis=1), jnp.concatenate([-sin, sin], axis=1)


def _layer(lidx, x, mem, p, batch, seq, tabs):
    cos_t, sin_t, dft = tabs
    x = _ffn(lidx, x, p["g_ffn1"], p["w_ffn1_gate"], p["w_ffn1_up"], p["w_ffn1_down"])

    s1, _ = _dft_split(seq)
    z, zg1, zg2, uc = _proj_in(lidx, x, p["g_mix"], p["w_in"], cos_t, sin_t, batch, seq, s1)
    z4 = z.reshape(batch, 1, seq, Z_WIDTH)
    outs, lses = [], []
    for src, cols in ((z4, (QA0_COL, KA0_COL, VA0_COL)),
                      (zg1, (0, GROUP_WIDTH, 2 * GROUP_WIDTH)),
                      (zg2, (0, GROUP_WIDTH, 2 * GROUP_WIDTH))):
        o, lse = _band_attention(lidx, src, *cols, A_HEADS_PER_GROUP, 1, A_RADIUS, want_lse=True)
        outs.append(o)
        lses.append(lse)
    ya = _group_merge(lidx, outs, lses, seq)
    (yd,) = _band_attention(lidx, z4, QD_COL, KD_COL, VD_COL, D_Q_HEADS,
                            D_Q_HEADS // D_KV_HEADS, D_RADIUS, sink=p["sink"])
    yd = yd.reshape(batch * seq, D_Q_HEADS * HEAD_DIM)
    yb = _short_conv(lidx, z, p["w_conv"], seq)
    yc = _fourier_mix(lidx, uc, dft)
    kv = _proj(lidx, mem, p["g_memkv"], p["w_mkv"])
    kv = kv.reshape(batch, mem.shape[0] // batch, kv.shape[1])
    x = _merge(lidx, x, p["g_mix"], p["w_gate"], p["b_gate"], (ya, yd, yb, yc),
               (p["w_pa"], p["w_pd"], p["w_pb"], p["w_pc"]), p["w_o"],
               p["g_memq"], p["w_mq"], kv, p["w_mo"], seq)

    x = _ffn(lidx, x, p["g_ffn2"], p["w_ffn2_gate"], p["w_ffn2_up"], p["w_ffn2_down"])
    return x


def _forward(x_prompt, x_sample, mem_prompt, mem_sample, p, g_final):
    depth = p["w_o"].shape[0]
    d = x_prompt.shape[-1]
    trunks = []
    for x, mem in ((x_prompt, mem_prompt), (x_sample, mem_sample)):
        batch, seq = x.shape[0], x.shape[1]
        tabs = _rope_tables(seq) + (_dft_tables(seq),)
        trunks.append((batch, seq, tabs, mem.reshape(-1, d)))

    def body(xs, l):
        lidx = l.reshape(1)
        return tuple(_layer(lidx, x, mem, p, batch, seq, tabs)
                     for x, (batch, seq, tabs, mem) in zip(xs, trunks)), None

    xs = (x_prompt.reshape(-1, d), x_sample.reshape(-1, d))
    xs, _ = lax.scan(body, xs, jnp.arange(depth, dtype=jnp.int32))
    zero = jnp.zeros((1,), jnp.int32)
    gf = g_final.reshape(1, d)
    return tuple(_final_norm(zero, x, gf).reshape(orig.shape)
                 for x, orig in zip(xs, (x_prompt, x_sample)))


def kernel(x_prompt, x_sample, mem_prompt, mem_sample, g_ffn1, w_ffn1_gate, w_ffn1_up, w_ffn1_down, g_mix, w_in, w_conv, sink, w_gate, b_gate, w_pa, w_pd, w_pb, w_pc, w_o, g_memq, g_memkv, w_mq, w_mkv, w_mo, g_ffn2, w_ffn2_gate, w_ffn2_up, w_ffn2_down, g_final):
    depth, d = g_ffn1.shape

    def gain(g):
        return g.reshape(depth, 1, d)

    def tiles(lo, width):
        return tuple(range(lo // PROJ_TILE, (lo + width) // PROJ_TILE))

    qa, ka, va = (tiles(i * A_WIDTH, A_WIDTH) for i in range(3))
    d_start = 3 * A_WIDTH
    d_width = (D_Q_HEADS + 2 * D_KV_HEADS) * HEAD_DIM
    conv = tiles(d_start + d_width, 3 * B_WIDTH)
    fourier = tiles(d_start + d_width + 3 * B_WIDTH, C_WIDTH)
    source_tiles = ((qa[0], ka[0]) + tiles(d_start, d_width) + (va[0],) + conv
                    + (qa[1], ka[1], va[1], qa[2], ka[2], va[2]) + fourier)
    assert len(source_tiles) == len(TILE_KINDS)
    w_in_tiles = _cast_w_in_tiles(w_in, source_tiles)
    tn = min(MERGE_COLS, d)
    w_gate_slabs = _cast_gate_slabs(w_gate, tn)
    b_gate_slabs = (b_gate.reshape(depth, 4, d // tn, tn).transpose(0, 2, 1, 3)
                    .reshape(depth, d // tn, 1, 4 * tn))
    p = {
        "g_ffn1": gain(g_ffn1), "w_ffn1_gate": w_ffn1_gate.astype(BF), "w_ffn1_up": w_ffn1_up.astype(BF),
        "w_ffn1_down": w_ffn1_down.astype(BF),
        "g_mix": gain(g_mix), "w_in": w_in_tiles,
        "w_conv": w_conv, "sink": sink,
        "w_gate": w_gate_slabs, "b_gate": b_gate_slabs,
        "w_pa": w_pa.astype(BF), "w_pd": w_pd.astype(BF), "w_pb": w_pb.astype(BF), "w_pc": w_pc.astype(BF),
        "w_o": w_o.astype(BF),
        "g_memq": gain(g_memq), "g_memkv": gain(g_memkv), "w_mq": w_mq.astype(BF),
        "w_mkv": w_mkv.astype(BF), "w_mo": w_mo.astype(BF),
        "g_ffn2": gain(g_ffn2), "w_ffn2_gate": w_ffn2_gate.astype(BF), "w_ffn2_up": w_ffn2_up.astype(BF),
        "w_ffn2_down": w_ffn2_down.astype(BF),
    }
    return _forward(x_prompt, x_sample, mem_prompt, mem_sample, p, g_final)
```

```python
import functools

import jax
import jax.numpy as jnp
from jax import lax
from jax.experimental import pallas as pl
from jax.experimental.pallas import tpu as pltpu

BF = jnp.bfloat16
F32 = jnp.float32

HEAD_DIM = 128
EPS = 1e-6
ROPE_THETA = 10000.0
A_DILATIONS = (1, 4, 16)
A_RADIUS = 64
A_HEADS_PER_GROUP = 4
A_WIDTH = 1536
D_Q_HEADS = 8
D_KV_HEADS = 2
D_RADIUS = 128
B_WIDTH = 1024
C_WIDTH = 1024
C_GROUP_DIM = 128
M_HEADS = 4

QA0_COL, KA0_COL, QD_COL, KD_COL, VD_COL, VA0_COL, BG_COL, CG_COL, XB_COL, Z_WIDTH = (
    0, 512, 1024, 2048, 2304, 2560, 3072, 4096, 5120, 6144)
PROJ_TILE = 512
GROUP_WIDTH = 512
PLAIN, ROT_Q, ROT_K, ROT_HALF = 0, 1, 2, 3
TILE_KINDS = ((ROT_Q, ROT_K, ROT_Q, ROT_Q, ROT_HALF) + (PLAIN,) * 7
              + (ROT_Q, ROT_K, PLAIN) * 2
              + (PLAIN,) * 2)
Z_TILES, G_TILES, U_TILES = 12, 3, 2

VMEM_LIMIT_BYTES = 60 * 1024 * 1024
SUBLANES = 8
BF16_SUBLANES = 16

FFN_ROWS = 1024
FFN_COLS = 512
PROJ_ROWS = 1024
MERGE_ROWS = 512
MERGE_COLS = 256
WIDE_ROWS = 1024
ROW_TILE = 512
ATTN_ROWS = 1024
ROW_CHUNK = 128
SOFTMAX_ROWS = 32
DFT_RESIDUES = 4
DFT_K2 = 16
NEG_BIG = -1e30


def _cparams(ndim):
    return pltpu.CompilerParams(dimension_semantics=("arbitrary",) * ndim,
                                vmem_limit_bytes=VMEM_LIMIT_BYTES)


def _grid_spec(grid, in_specs, out_specs, scratch=()):
    return pltpu.PrefetchScalarGridSpec(num_scalar_prefetch=1, grid=grid, in_specs=in_specs,
                                        out_specs=out_specs, scratch_shapes=list(scratch))


def _rms(x, g):
    return x * lax.rsqrt(jnp.mean(x * x, axis=-1, keepdims=True) + EPS) * g


def _dot(a, b):
    return jnp.dot(a, b, preferred_element_type=F32)


def _dot_t(a, b):
    return lax.dot_general(a, b, (((1,), (1,)), ((), ())), preferred_element_type=F32)


def _ffn_body(l_ref, x_ref, g_ref, wgu_ref, wd_ref, *rest, n_steps, final):
    gf_ref = rest[0] if final else None
    o_ref, h_ref = rest[-2:]
    j = pl.program_id(1)
    tf = wd_ref.shape[0]

    def step(first, last):
        if first:
            h_ref[...] = _rms(x_ref[...], g_ref[...]).astype(BF)
        gate_up = _dot(h_ref[...], wgu_ref[...])
        gate, up = gate_up[:, :tf], gate_up[:, tf:]
        a = (0.5 * gate * jax.nn.sigmoid(gate) * up).astype(BF)
        o = (x_ref[...] if first else o_ref[...]) + _dot(a, wd_ref[...])
        o_ref[...] = _rms(o, gf_ref[...]) if final and last else o

    if n_steps == 1:
        step(True, True)
    else:
        pl.when(j == 0)(functools.partial(step, True, False))
        pl.when((j > 0) & (j < n_steps - 1))(functools.partial(step, False, False))
        pl.when(j == n_steps - 1)(functools.partial(step, False, True))


def _ffn(lidx, x, g, w_gate_up, wd, final_gain=None):
    m, d = x.shape
    tf = w_gate_up.shape[-1] // 2
    f = wd.shape[1]
    tm = min(FFN_ROWS, m)
    final = final_gain is not None
    in_specs = [pl.BlockSpec((tm, d), lambda i, j, l: (i, 0)),
                pl.BlockSpec((None, 1, d), lambda i, j, l: (l[0], 0, 0)),
                pl.BlockSpec((None, None, d, 2 * tf), lambda i, j, l: (l[0], j, 0, 0)),
                pl.BlockSpec((None, tf, d), lambda i, j, l: (l[0], j, 0))]
    args = [lidx, x, g, w_gate_up, wd]
    if final:
        in_specs.append(pl.BlockSpec((1, d), lambda i, j, l: (0, 0)))
        args.append(final_gain)
    return pl.pallas_call(
        functools.partial(_ffn_body, n_steps=f // tf, final=final),
        grid_spec=_grid_spec((m // tm, f // tf), in_specs,
                             pl.BlockSpec((tm, d), lambda i, j, l: (i, 0)),
                             [pltpu.VMEM((tm, d), BF)]),
        out_shape=jax.ShapeDtypeStruct((m, d), F32),
        compiler_params=_cparams(2),
    )(*args)


def _proj_body(l_ref, x_ref, g_ref, w_ref, o_ref, h_ref):
    @pl.when(pl.program_id(1) == 0)
    def _():
        h_ref[...] = _rms(x_ref[...], g_ref[...]).astype(BF)

    o_ref[...] = _dot(h_ref[...], w_ref[...]).astype(o_ref.dtype)


def _proj_in_body(l_ref, kind_ref, x_ref, g_ref, w_ref, cos_ref, sin_ref,
                  z_ref, g1_ref, g2_ref, uc_ref, h_ref, acc_ref, res_ref, *, tm, s1):
    j = pl.program_id(2)
    n_tiles = len(TILE_KINDS)
    chunks = PROJ_TILE // HEAD_DIM
    half_chunks = (VD_COL - KD_COL) // HEAD_DIM
    g1_0, g2_0, u_0 = Z_TILES, Z_TILES + G_TILES, Z_TILES + 2 * G_TILES
    n_rot = {PLAIN: 0, ROT_Q: chunks, ROT_K: chunks, ROT_HALF: half_chunks}

    def multiply_tile():
        acc_ref[...] = _dot(h_ref[...], w_ref[...])

    @pl.when(j == 0)
    def _():
        h_ref[...] = _rms(x_ref[...], g_ref[...]).astype(BF)
        multiply_tile()

    def lanes(c):
        return slice(c * HEAD_DIM, (c + 1) * HEAD_DIM)

    def finished(c, tile_kind):
        xc = acc_ref[:, lanes(c)]
        if c >= n_rot[tile_kind]:
            return xc
        rotated = xc * cos_ref[...] + pltpu.roll(xc, HEAD_DIM // 2, 1) * sin_ref[...]
        return rotated * HEAD_DIM ** -0.5 if tile_kind == ROT_Q else rotated

    def store_natural(tile_kind):
        for c in range(chunks):
            z_ref[:, lanes(c)] = finished(c, tile_kind).astype(BF)

    def store_residue_major(dst_ref, dil, tile_kind):
        for c in range(chunks):
            res_ref[c] = finished(c, tile_kind)
        rows = tm // dil
        for r in range(dil):
            for c in range(chunks):
                dst_ref[r, :, lanes(c)] = res_ref[c, pl.ds(r, rows, stride=dil), :].astype(BF)

    t = j - 1
    kind = kind_ref[jnp.maximum(t, 0)]
    destinations = (
        (0, g1_0, (PLAIN, ROT_Q, ROT_K, ROT_HALF), store_natural),
        (g1_0, g2_0, (PLAIN, ROT_Q, ROT_K), functools.partial(store_residue_major, g1_ref, A_DILATIONS[1])),
        (g2_0, u_0, (PLAIN, ROT_Q, ROT_K), functools.partial(store_residue_major, g2_ref, A_DILATIONS[2])),
        (u_0, n_tiles, (PLAIN,), functools.partial(store_residue_major, uc_ref, s1)),
    )
    for lo, hi, kinds, store in destinations:
        for tile_kind in kinds:
            in_range = (t >= lo) & (t < hi) & (kind == tile_kind)

            @pl.when(in_range & (j < n_tiles))
            def _(store=store, tile_kind=tile_kind):
                store(tile_kind)
                multiply_tile()

            if hi == n_tiles:
                @pl.when(in_range & (j == n_tiles))
                def _(store=store, tile_kind=tile_kind):
                    store(tile_kind)


def _proj(lidx, x, g, w, out_dtype=BF, tn=512):
    m, d = x.shape
    n = w.shape[-1]
    tm, tn = min(ROW_TILE, m), min(tn, n)
    return pl.pallas_call(
        _proj_body,
        grid_spec=_grid_spec(
            (m // tm, n // tn),
            [pl.BlockSpec((tm, d), lambda i, j, l: (i, 0)),
             pl.BlockSpec((None, 1, d), lambda i, j, l: (l[0], 0, 0)),
             pl.BlockSpec((None, d, tn), lambda i, j, l: (l[0], 0, j))],
            pl.BlockSpec((tm, tn), lambda i, j, l: (i, j)),
            [pltpu.VMEM((tm, d), BF)]),
        out_shape=jax.ShapeDtypeStruct((m, n), out_dtype),
        compiler_params=_cparams(2),
    )(lidx, x, g, w)


def _proj_in(lidx, x, g, w, cos_t, sin_t, batch, seq, s1):
    m, d = x.shape
    tm, tn = min(PROJ_ROWS, m), PROJ_TILE
    n_tiles = len(TILE_KINDS)
    d1, d2 = A_DILATIONS[1], A_DILATIONS[2]
    assert w.shape[1:] == (n_tiles, d, tn) and seq % tm == 0 and tm % (BF16_SUBLANES * max(d2, s1)) == 0
    nsb = seq // tm
    g1_0, g2_0, u_0 = Z_TILES, Z_TILES + G_TILES, Z_TILES + 2 * G_TILES

    def residue_major(dil, first_tile, tiles):
        return pl.BlockSpec((None, dil, tm // dil, tn),
                            lambda b, s, j, l, k: (b, 0, s, jnp.clip(j - 1 - first_tile, 0, tiles - 1)))

    return pl.pallas_call(
        functools.partial(_proj_in_body, tm=tm, s1=s1),
        grid_spec=pltpu.PrefetchScalarGridSpec(
            num_scalar_prefetch=2,
            grid=(batch, nsb, n_tiles + 1),
            in_specs=[pl.BlockSpec((tm, d), lambda b, s, j, l, k: (b * nsb + s, 0)),
                      pl.BlockSpec((None, 1, d), lambda b, s, j, l, k: (l[0], 0, 0)),
                      pl.BlockSpec((None, None, d, tn),
                                   lambda b, s, j, l, k: (l[0], jnp.minimum(j, n_tiles - 1), 0, 0)),
                      pl.BlockSpec((tm, HEAD_DIM), lambda b, s, j, l, k: (s, 0)),
                      pl.BlockSpec((tm, HEAD_DIM), lambda b, s, j, l, k: (s, 0))],
            out_specs=[pl.BlockSpec((tm, tn),
                                    lambda b, s, j, l, k: (b * nsb + s, jnp.clip(j - 1, 0, Z_TILES - 1))),
                       residue_major(d1, g1_0, G_TILES),
                       residue_major(d2, g2_0, G_TILES),
                       residue_major(s1, u_0, U_TILES)],
            scratch_shapes=[pltpu.VMEM((tm, d), BF), pltpu.VMEM((tm, tn), F32),
                            pltpu.VMEM((tn // HEAD_DIM, tm, HEAD_DIM), F32)]),
        out_shape=[jax.ShapeDtypeStruct((m, Z_WIDTH), BF),
                   jax.ShapeDtypeStruct((batch, d1, seq // d1, G_TILES * tn), BF),
                   jax.ShapeDtypeStruct((batch, d2, seq // d2, G_TILES * tn), BF),
                   jax.ShapeDtypeStruct((batch, s1, seq // s1, U_TILES * tn), BF)],
        compiler_params=_cparams(3),
    )(lidx, jnp.asarray(TILE_KINDS, jnp.int32), x, g, w, cos_t, sin_t)


def _band_attn_body(l_ref, q_ref, kp_ref, kc_ref, kn_ref, vp_ref, vc_ref, vn_ref, *rest,
                    tq, rps, length, radius, n_heads, rep, has_sink, want_lse):
    rest = list(rest)
    sink_ref = rest.pop(0) if has_sink else None
    o_ref = rest.pop(0)
    lse_ref = rest.pop(0) if want_lse else None
    s_ref, p_ref, inv_ref = rest
    per_sequence = [[ref.at[ri] for ref in (q_ref, kp_ref, kc_ref, kn_ref, vp_ref, vc_ref, vn_ref,
                                            o_ref, lse_ref) if ref is not None] for ri in range(rps)]
    i = pl.program_id(2)
    rc = ROW_CHUNK
    nc = tq // rc
    win = rc + 2 * radius
    groups = n_heads // rep
    assert radius <= rc and radius % 16 == 0

    def window(prev_ref, cur_ref, next_ref, c, cols):
        lo, hi = c * rc - radius, (c + 1) * rc + radius
        parts = []
        if lo < 0:
            parts.append(prev_ref[rc - radius:rc, cols])
        parts.append(cur_ref[max(lo, 0):min(hi, tq), cols])
        if hi > tq:
            parts.append(next_ref[0:radius, cols])
        return parts[0] if len(parts) == 1 else jnp.concatenate(parts, axis=0)

    row = lax.broadcasted_iota(jnp.int32, (rc, win), 0)
    col = lax.broadcasted_iota(jnp.int32, (rc, win), 1)
    band_bias = jnp.where(jnp.abs(col - radius - row) <= radius, 0.0, NEG_BIG)
    key_col = lax.broadcasted_iota(jnp.int32, (1, win), 1)
    rb = SOFTMAX_ROWS
    for step, (ri, c) in enumerate((ri, c) for ri in range(rps) for c in range(nc)):
        q_ref, kp_ref, kc_ref, kn_ref, vp_ref, vc_ref, vn_ref, o_ref = per_sequence[ri][:8]
        lse_ref = per_sequence[ri][8] if want_lse else None
        slot = step % 2
        kpos = i * tq + c * rc - radius + key_col
        bias = band_bias + jnp.where(kpos >= 0, jnp.where(kpos < length, 0.0, NEG_BIG), NEG_BIG)
        rows = slice(c * rc, (c + 1) * rc)
        for g in range(groups):
            k = window(kp_ref, kc_ref, kn_ref, c, slice(g * HEAD_DIM, (g + 1) * HEAD_DIM))
            q = q_ref[rows, g * rep * HEAD_DIM:(g + 1) * rep * HEAD_DIM]
            if rep > 1:
                q = jnp.concatenate([q[:, r * HEAD_DIM:(r + 1) * HEAD_DIM] for r in range(rep)], axis=0)
            s_ref[slot, g * rep * rc:(g + 1) * rep * rc, :] = _dot_t(q, k)
        for blk in range(n_heads * rc // rb):
            h, r0 = divmod(blk * rb, rc)
            srows = slice(blk * rb, (blk + 1) * rb)
            s = s_ref[slot, srows, :] + bias[r0:r0 + rb]
            m = jnp.max(s, axis=-1, keepdims=True)
            if has_sink:
                sk = sink_ref[l_ref[0], h]
                m = jnp.maximum(m, sk)
            p = jnp.exp(s - m)
            den = jnp.sum(p, axis=-1, keepdims=True)
            if has_sink:
                den = den + jnp.exp(sk - m)
            p_ref[slot, srows, :] = p.astype(BF)
            inv_ref[slot, srows, :] = jnp.broadcast_to(1.0 / den, (rb, HEAD_DIM))
            if want_lse:
                lse_ref[c * rc + r0:c * rc + r0 + rb, h * HEAD_DIM:(h + 1) * HEAD_DIM] = (
                    jnp.broadcast_to(m + jnp.log(den), (rb, HEAD_DIM)))
        for g in range(groups):
            v = window(vp_ref, vc_ref, vn_ref, c, slice(g * HEAD_DIM, (g + 1) * HEAD_DIM))
            grows = slice(g * rep * rc, (g + 1) * rep * rc)
            o = _dot(p_ref[slot, grows, :], v) * inv_ref[slot, grows, :]
            for r in range(rep):
                h = g * rep + r
                o_ref[rows, h * HEAD_DIM:(h + 1) * HEAD_DIM] = o[r * rc:(r + 1) * rc].astype(o_ref.dtype)


def _band_attention(lidx, src, q_col, k_col, v_col, n_heads, rep, radius, sink=None, want_lse=False):
    batch, dil, length, width = src.shape
    tq = min(ATTN_ROWS, length)
    nc = tq // ROW_CHUNK
    qw, kw = n_heads * HEAD_DIM, (n_heads // rep) * HEAD_DIM
    assert length % tq == 0 and tq % ROW_CHUNK == 0
    assert q_col % qw == 0 and k_col % kw == 0 and v_col % kw == 0
    last_chunk = length // ROW_CHUNK - 1
    rps = min(dil, ATTN_ROWS // tq)
    assert dil % rps == 0

    def kv_specs(col):
        cb = col // kw
        return [
            pl.BlockSpec((None, rps, ROW_CHUNK, kw),
                         lambda b, r, i, l: (b, r, jnp.maximum(i * nc - 1, 0), cb)),
            pl.BlockSpec((None, rps, tq, kw), lambda b, r, i, l: (b, r, i, cb)),
            pl.BlockSpec((None, rps, ROW_CHUNK, kw),
                         lambda b, r, i, l: (b, r, jnp.minimum((i + 1) * nc, last_chunk), cb)),
        ]

    in_specs = [pl.BlockSpec((None, rps, tq, qw), lambda b, r, i, l: (b, r, i, q_col // qw))]
    in_specs += kv_specs(k_col) + kv_specs(v_col)
    args = [lidx] + [src] * 7
    if sink is not None:
        in_specs.append(pl.BlockSpec(memory_space=pltpu.SMEM))
        args.append(sink)
    out_spec = pl.BlockSpec((None, rps, tq, qw), lambda b, r, i, l: (b, r, i, 0))
    out_shape = [jax.ShapeDtypeStruct((batch, dil, length, qw), BF)]
    out_specs = [out_spec]
    if want_lse:
        out_shape.append(jax.ShapeDtypeStruct((batch, dil, length, qw), F32))
        out_specs.append(out_spec)
    return pl.pallas_call(
        functools.partial(_band_attn_body, tq=tq, rps=rps, length=length, radius=radius, n_heads=n_heads,
                          rep=rep, has_sink=sink is not None, want_lse=want_lse),
        grid_spec=_grid_spec((batch, dil // rps, length // tq), in_specs, out_specs,
                             [pltpu.VMEM((2, n_heads * ROW_CHUNK, ROW_CHUNK + 2 * radius), F32),
                              pltpu.VMEM((2, n_heads * ROW_CHUNK, ROW_CHUNK + 2 * radius), BF),
                              pltpu.VMEM((2, n_heads * ROW_CHUNK, HEAD_DIM), F32)]),
        out_shape=out_shape,
        compiler_params=_cparams(3),
    )(*args)


def _group_merge_body(l_ref, o0_ref, l0_ref, o1_ref, l1_ref, o2_ref, l2_ref, y_ref, nat_ref, *, tm):
    heads = A_HEADS_PER_GROUP
    for slot, (src_ref, dil) in enumerate(((o1_ref, A_DILATIONS[1]), (l1_ref, A_DILATIONS[1]),
                                           (o2_ref, A_DILATIONS[2]), (l2_ref, A_DILATIONS[2]))):
        rows = tm // dil
        for r in range(dil):
            for h in range(heads):
                nat_ref[slot * heads + h, pl.ds(r, rows, stride=dil), :] = (
                    src_ref[r, :, h * HEAD_DIM:(h + 1) * HEAD_DIM].astype(F32))
    for h in range(heads):
        hc = slice(h * HEAD_DIM, (h + 1) * HEAD_DIM)
        o1, l1, o2, l2 = (nat_ref[slot * heads + h] for slot in range(4))
        l0 = l0_ref[:, hc]
        m = jnp.maximum(jnp.maximum(l0, l1), l2)
        e0, e1, e2 = jnp.exp(l0 - m), jnp.exp(l1 - m), jnp.exp(l2 - m)
        num = e0 * o0_ref[:, hc].astype(F32) + e1 * o1 + e2 * o2
        y_ref[:, hc] = (num / (e0 + e1 + e2)).astype(y_ref.dtype)


def _group_merge(lidx, outs, lses, seq):
    batch, _, _, w = outs[0].shape
    m = batch * seq
    tm = min(WIDE_ROWS, seq)
    nsb = seq // tm
    assert tm % (BF16_SUBLANES * A_DILATIONS[2]) == 0

    def spec(dil):
        return pl.BlockSpec((None, dil, tm // dil, w), lambda b, s, l: (b, 0, s, 0))

    args = []
    in_specs = []
    for o, lse, dil in zip(outs, lses, A_DILATIONS):
        args += [o, lse]
        in_specs += [spec(dil)] * 2
    in_specs[0] = in_specs[1] = pl.BlockSpec((None, None, tm, w), lambda b, s, l: (b, 0, s, 0))
    return pl.pallas_call(
        functools.partial(_group_merge_body, tm=tm),
        grid_spec=_grid_spec((batch, nsb), in_specs, pl.BlockSpec((tm, w), lambda b, s, l: (b * nsb + s, 0)),
                             [pltpu.VMEM((4 * A_HEADS_PER_GROUP, tm, HEAD_DIM), F32)]),
        out_shape=jax.ShapeDtypeStruct((m, w), BF),
        compiler_params=_cparams(2),
    )(lidx, *args)


def _conv_body(l_ref, bg_ref, cg_ref, xb_ref, cgp_ref, xbp_ref, cgn_ref, xbn_ref, w_ref, o_ref,
               *, tm, seq):
    i = pl.program_id(0)
    u = cg_ref[...].astype(F32) * xb_ref[...].astype(F32)
    not_first = ((i * tm) % seq != 0).astype(F32)
    not_last = (((i + 1) * tm) % seq != 0).astype(F32)
    last = SUBLANES - 1
    u_prev = cgp_ref[last:, :].astype(F32) * xbp_ref[last:, :].astype(F32) * not_first
    u_next = cgn_ref[0:1, :].astype(F32) * xbn_ref[0:1, :].astype(F32) * not_last
    row = lax.broadcasted_iota(jnp.int32, u.shape, 0)
    below = jnp.where(row == 0, u_prev, pltpu.roll(u, 1, 0))
    above = jnp.where(row == tm - 1, u_next, pltpu.roll(u, tm - 1, 0))
    y = below * w_ref[0:1, :] + u * w_ref[1:2, :] + above * w_ref[2:3, :]
    o_ref[...] = (bg_ref[...].astype(F32) * y).astype(o_ref.dtype)


def _short_conv(lidx, z, w_conv, seq):
    m = z.shape[0]
    tm, tc = min(WIDE_ROWS, m), B_WIDTH
    halo = SUBLANES
    assert seq % tm == 0 and B_WIDTH % tc == 0
    last_halo = m // halo - 1

    def tile(col):
        return pl.BlockSpec((tm, tc), lambda i, j, l: (i, col // tc + j))

    def prev(col):
        return pl.BlockSpec((halo, tc), lambda i, j, l: (jnp.maximum(i * (tm // halo) - 1, 0), col // tc + j))

    def nxt(col):
        return pl.BlockSpec((halo, tc),
                            lambda i, j, l: (jnp.minimum((i + 1) * (tm // halo), last_halo), col // tc + j))

    return pl.pallas_call(
        functools.partial(_conv_body, tm=tm, seq=seq),
        grid_spec=_grid_spec(
            (m // tm, B_WIDTH // tc),
            [tile(BG_COL), tile(CG_COL), tile(XB_COL), prev(CG_COL), prev(XB_COL), nxt(CG_COL), nxt(XB_COL),
             pl.BlockSpec((None, 3, tc), lambda i, j, l: (l[0], 0, j))],
            pl.BlockSpec((tm, tc), lambda i, j, l: (i, j))),
        out_shape=jax.ShapeDtypeStruct((m, B_WIDTH), BF),
        compiler_params=_cparams(2),
    )(lidx, z, z, z, z, z, z, z, w_conv)


def _dft1_body(l_ref, u_ref, g_ref, ar_ref, ai_ref, *, tn1, tk2, s2, groups):
    for jj in range(tn1):
        r = _dot(g_ref[jj], u_ref[jj])
        for g in range(groups):
            gc = slice(g * C_GROUP_DIM, (g + 1) * C_GROUP_DIM)
            for t in range(s2 // tk2):
                dst = slice(jj * tk2, (jj + 1) * tk2)
                ar_ref[g, t, dst, :] = r[t * tk2:(t + 1) * tk2, gc]
                ai_ref[g, t, dst, :] = r[s2 + t * tk2:s2 + (t + 1) * tk2, gc]


def _dft2_body(l_ref, ar_ref, ai_ref, m2_ref, csc_ref, y_ref, xr_ref, xi_ref,
               *, tk2, s1, groups, scale):
    for j in range(tk2):
        rows = pl.ds(j, s1, stride=tk2)
        a_re = jnp.concatenate([ar_ref[g, rows, :] for g in range(groups)], axis=1)
        a_im = jnp.concatenate([ai_ref[g, rows, :] for g in range(groups)], axis=1)
        x = _dot(m2_ref[...], jnp.concatenate([a_re, a_im], axis=0).astype(BF))
        for g in range(groups):
            gc = slice(g * C_GROUP_DIM, (g + 1) * C_GROUP_DIM)
            xr_ref[g, rows, :] = x[:s1, gc]
            xi_ref[g, rows, :] = x[s1:, gc]
    for g in range(groups):
        gc = slice(g * C_GROUP_DIM, (g + 1) * C_GROUP_DIM)
        x_g = jnp.concatenate([xr_ref[g], xi_ref[g]], axis=1).astype(BF)
        yg = _dot(x_g, csc_ref[...]) * scale
        for k1 in range(s1):
            y_ref[k1, :, gc] = yg[k1 * tk2:(k1 + 1) * tk2].astype(y_ref.dtype)


def _fourier_mix(lidx, u, tables):
    g1, m2, csc = tables
    batch, s1, s2, width = u.shape
    groups = width // C_GROUP_DIM
    tn1, tk2 = DFT_RESIDUES, DFT_K2
    a_shape = jax.ShapeDtypeStruct((batch, groups, s2 // tk2, s1 * tk2, C_GROUP_DIM), F32)
    a_out = pl.BlockSpec((None, groups, s2 // tk2, tn1 * tk2, C_GROUP_DIM), lambda b, t, l: (b, 0, 0, t, 0))
    ar, ai = pl.pallas_call(
        functools.partial(_dft1_body, tn1=tn1, tk2=tk2, s2=s2, groups=groups),
        grid_spec=_grid_spec(
            (batch, s1 // tn1),
            [pl.BlockSpec((None, tn1, s2, width), lambda b, t, l: (b, t, 0, 0)),
             pl.BlockSpec((tn1, 2 * s2, s2), lambda b, t, l: (t, 0, 0))],
            [a_out, a_out]),
        out_shape=[a_shape, a_shape],
        compiler_params=_cparams(2),
    )(lidx, u, g1)
    a_in = pl.BlockSpec((None, groups, None, s1 * tk2, C_GROUP_DIM), lambda b, t, l: (b, 0, t, 0, 0))
    y = pl.pallas_call(
        functools.partial(_dft2_body, tk2=tk2, s1=s1, groups=groups,
                          scale=float((s1 * s2 * C_GROUP_DIM) ** -0.5)),
        grid_spec=_grid_spec(
            (batch, s2 // tk2),
            [a_in, a_in,
             pl.BlockSpec((2 * s1, 2 * s1), lambda b, t, l: (0, 0)),
             pl.BlockSpec((2 * C_GROUP_DIM, C_GROUP_DIM), lambda b, t, l: (0, 0))],
            pl.BlockSpec((None, s1, tk2, width), lambda b, t, l: (b, 0, t, 0)),
            [pltpu.VMEM((groups, tk2 * s1, C_GROUP_DIM), F32)] * 2),
        out_shape=jax.ShapeDtypeStruct((batch, s1, s2, width), BF),
        compiler_params=_cparams(2),
    )(lidx, ar, ai, m2, csc)
    return y.reshape(batch * s1 * s2, width)


def _dft_split(seq):
    s2 = 256 if seq >= 8192 else 128
    return seq // s2, s2


def _dft_tables(seq):
    s1, s2 = _dft_split(seq)
    two_pi = 2.0 * jnp.pi

    def cs(idx, period):
        ang = (idx % period).astype(F32) * (two_pi / period)
        return jnp.cos(ang), jnp.sin(ang)

    n1 = jnp.arange(s1, dtype=jnp.int32)[:, None, None]
    k2 = jnp.arange(s2, dtype=jnp.int32)[None, :, None]
    n2 = jnp.arange(s2, dtype=jnp.int32)[None, None, :]
    ca, sa = cs(k2 * (n1 + s1 * n2), seq)
    g1 = jnp.concatenate([ca, -sa], axis=1).astype(BF)
    a = jnp.arange(s1, dtype=jnp.int32)
    cb, sb = cs(a[:, None] * a[None, :], s1)
    m2 = jnp.concatenate([jnp.concatenate([cb, sb], axis=1),
                          jnp.concatenate([-sb, cb], axis=1)], axis=0).astype(BF)
    c = jnp.arange(C_GROUP_DIM, dtype=jnp.int32)
    cc, sc = cs(c[:, None] * c[None, :], C_GROUP_DIM)
    return g1, m2, jnp.concatenate([cc, sc], axis=0).astype(BF)


def _memory_update(x, g_ref, wq_ref, kv_ref, wo_ref, att_ref):
    h = _rms(x, g_ref[...]).astype(BF)
    q = (_dot(h, wq_ref[...]) * HEAD_DIM ** -0.5).astype(BF)
    mw = M_HEADS * HEAD_DIM
    for hh in range(M_HEADS):
        cols = slice(hh * HEAD_DIM, (hh + 1) * HEAD_DIM)
        k = kv_ref[:, cols]
        v = kv_ref[:, mw + hh * HEAD_DIM:mw + (hh + 1) * HEAD_DIM]
        s = _dot_t(q[:, cols], k)
        p = jnp.exp(s - jnp.max(s, axis=-1, keepdims=True))
        o = _dot(p.astype(BF), v) / jnp.sum(p, axis=-1, keepdims=True)
        att_ref[:, cols] = o.astype(BF)
    return x + _dot(att_ref[...], wo_ref[...])


def _merge_body(l_ref, x_ref, g_ref, wgate_ref, bgate_ref, ya_ref, yd_ref, yb_ref, yc_ref,
                wpa_ref, wpd_ref, wpb_ref, wpc_ref, wo_ref, gq_ref, wq_ref, kv_ref, wmo_ref,
                o_ref, h_ref, att_ref, *, tn):
    n = pl.program_id(1)

    def step(first):
        if first:
            h_ref[...] = _rms(x_ref[...], g_ref[...]).astype(BF)
        gates = jax.nn.sigmoid(_dot(h_ref[...], wgate_ref[...]) + bgate_ref[...])
        merged = None
        for i, (br_ref, wp_ref) in enumerate(((ya_ref, wpa_ref), (yd_ref, wpd_ref),
                                              (yb_ref, wpb_ref), (yc_ref, wpc_ref))):
            term = gates[:, i * tn:(i + 1) * tn] * _dot(br_ref[...], wp_ref[...])
            merged = term if merged is None else merged + term
        o_ref[...] = (x_ref[...] if first else o_ref[...]) + _dot(merged.astype(BF), wo_ref[...])

    pl.when(n == 0)(functools.partial(step, True))
    pl.when(n > 0)(functools.partial(step, False))

    @pl.when(n == pl.num_programs(1) - 1)
    def _():
        o_ref[...] = _memory_update(o_ref[...], gq_ref, wq_ref, kv_ref, wmo_ref, att_ref)


def _merge(lidx, x, g, w_gate, b_gate, branches, projs, w_o, g_memq, w_mq, kv, w_mo, seq):
    m, d = x.shape
    tm = min(MERGE_ROWS, m)
    tn = w_gate.shape[-1] // 4
    n_mem, kvw = kv.shape[1], kv.shape[2]
    mw = w_mq.shape[-1]
    assert seq % tm == 0

    def row_tile(a):
        return pl.BlockSpec((tm, a.shape[1]), lambda i, n, l: (i, 0))

    def proj_tile(w):
        return pl.BlockSpec((None, w.shape[1], tn), lambda i, n, l: (l[0], 0, n))

    return pl.pallas_call(
        functools.partial(_merge_body, tn=tn),
        grid_spec=_grid_spec(
            (m // tm, d // tn),
            [row_tile(x),
             pl.BlockSpec((None, 1, d), lambda i, n, l: (l[0], 0, 0)),
             pl.BlockSpec((None, None, d, 4 * tn), lambda i, n, l: (l[0], n, 0, 0)),
             pl.BlockSpec((None, None, 1, 4 * tn), lambda i, n, l: (l[0], n, 0, 0))]
            + [row_tile(b) for b in branches] + [proj_tile(w) for w in projs]
            + [pl.BlockSpec((None, tn, d), lambda i, n, l: (l[0], n, 0)),
               pl.BlockSpec((None, 1, d), lambda i, n, l: (l[0], 0, 0)),
               pl.BlockSpec((None, d, mw), lambda i, n, l: (l[0], 0, 0)),
               pl.BlockSpec((None, n_mem, kvw), lambda i, n, l: ((i * tm) // seq, 0, 0)),
               pl.BlockSpec((None, mw, d), lambda i, n, l: (l[0], 0, 0))],
            row_tile(x),
            [pltpu.VMEM((tm, d), BF), pltpu.VMEM((tm, mw), BF)]),
        out_shape=jax.ShapeDtypeStruct((m, d), F32),
        compiler_params=_cparams(2),
    )(lidx, x, g, w_gate, b_gate, *branches, *projs, w_o, g_memq, w_mq, kv, w_mo)


def _cast_body(*refs):
    x_ref, o_ref = refs[-2:]
    o_ref[...] = x_ref[...].astype(o_ref.dtype)


def _cast_w_in_tiles(w_in, source_tiles):
    depth, d, _ = w_in.shape
    n = len(source_tiles)
    return pl.pallas_call(
        _cast_body,
        grid_spec=_grid_spec((depth, n),
                             [pl.BlockSpec((None, d, PROJ_TILE), lambda l, j, t: (l, 0, t[j]))],
                             pl.BlockSpec((None, None, d, PROJ_TILE), lambda l, j, t: (l, j, 0, 0))),
        out_shape=jax.ShapeDtypeStruct((depth, n, d, PROJ_TILE), BF),
        compiler_params=_cparams(2),
    )(jnp.asarray(source_tiles, jnp.int32), w_in)


def _cast_pair_body(a_ref, b_ref, o_ref):
    half = a_ref.shape[-1]
    o_ref[:, :half] = a_ref[...].astype(o_ref.dtype)
    o_ref[:, half:] = b_ref[...].astype(o_ref.dtype)


def _cast_gate_up_slabs(wg, wu, tf):
    depth, d, f = wg.shape
    spec = pl.BlockSpec((None, d, tf), lambda l, j: (l, 0, j))
    return pl.pallas_call(
        _cast_pair_body,
        grid=(depth, f // tf),
        in_specs=[spec, spec],
        out_specs=pl.BlockSpec((None, None, d, 2 * tf), lambda l, j: (l, j, 0, 0)),
        out_shape=jax.ShapeDtypeStruct((depth, f // tf, d, 2 * tf), BF),
        compiler_params=_cparams(2),
    )(wg, wu)


def _cast_gate_slabs(w_gate, tn):
    depth, n_gates, d, _ = w_gate.shape
    return pl.pallas_call(
        _cast_body,
        grid=(depth, d // tn, n_gates),
        in_specs=[pl.BlockSpec((None, None, d, tn), lambda l, n, i: (l, i, 0, n))],
        out_specs=pl.BlockSpec((None, None, d, tn), lambda l, n, i: (l, n, 0, i)),
        out_shape=jax.ShapeDtypeStruct((depth, d // tn, d, n_gates * tn), BF),
        compiler_params=_cparams(3),
    )(w_gate)


def _rope_tables(seq):
    half = HEAD_DIM // 2
    inv = ROPE_THETA ** (-jnp.arange(half, dtype=F32) / half)
    ang = jnp.arange(seq, dtype=jnp.int32).astype(F32)[:, None] * inv[None, :]
    cos, sin = jnp.cos(ang), jnp.sin(ang)
    return jnp.concatenate([cos, cos], axis=1), jnp.concatenate([-sin, sin], axis=1)


def _layer(lidx, x, mem, p, batch, seq, tabs, final_gain=None):
    cos_t, sin_t, dft = tabs
    x = _ffn(lidx, x, p["g_ffn1"], p["w_ffn1_gate_up"], p["w_ffn1_down"])

    s1, _ = _dft_split(seq)
    z, zg1, zg2, uc = _proj_in(lidx, x, p["g_mix"], p["w_in"], cos_t, sin_t, batch, seq, s1)
    z4 = z.reshape(batch, 1, seq, Z_WIDTH)
    outs, lses = [], []
    for src, cols in ((z4, (QA0_COL, KA0_COL, VA0_COL)),
                      (zg1, (0, GROUP_WIDTH, 2 * GROUP_WIDTH)),
                      (zg2, (0, GROUP_WIDTH, 2 * GROUP_WIDTH))):
        o, lse = _band_attention(lidx, src, *cols, A_HEADS_PER_GROUP, 1, A_RADIUS, want_lse=True)
        outs.append(o)
        lses.append(lse)
    ya = _group_merge(lidx, outs, lses, seq)
    (yd,) = _band_attention(lidx, z4, QD_COL, KD_COL, VD_COL, D_Q_HEADS,
                            D_Q_HEADS // D_KV_HEADS, D_RADIUS, sink=p["sink"])
    yd = yd.reshape(batch * seq, D_Q_HEADS * HEAD_DIM)
    yb = _short_conv(lidx, z, p["w_conv"], seq)
    yc = _fourier_mix(lidx, uc, dft)
    kv = _proj(lidx, mem, p["g_memkv"], p["w_mkv"])
    kv = kv.reshape(batch, mem.shape[0] // batch, kv.shape[1])
    x = _merge(lidx, x, p["g_mix"], p["w_gate"], p["b_gate"], (ya, yd, yb, yc),
               (p["w_pa"], p["w_pd"], p["w_pb"], p["w_pc"]), p["w_o"],
               p["g_memq"], p["w_mq"], kv, p["w_mo"], seq)

    return _ffn(lidx, x, p["g_ffn2"], p["w_ffn2_gate_up"], p["w_ffn2_down"], final_gain)


def _forward(x_prompt, x_sample, mem_prompt, mem_sample, p, g_final):
    depth = p["w_o"].shape[0]
    d = x_prompt.shape[-1]
    trunks = []
    for x, mem in ((x_prompt, mem_prompt), (x_sample, mem_sample)):
        batch, seq = x.shape[0], x.shape[1]
        tabs = _rope_tables(seq) + (_dft_tables(seq),)
        trunks.append((batch, seq, tabs, mem.reshape(-1, d)))

    gf = g_final.reshape(1, d)
    xs = [x_prompt.reshape(-1, d), x_sample.reshape(-1, d)]
    for layer in range(depth):
        lidx = jnp.full((1,), layer, jnp.int32)
        xs = [_layer(lidx, x, mem, p, batch, seq, tabs, gf if layer == depth - 1 else None)
              for x, (batch, seq, tabs, mem) in zip(xs, trunks)]
    return tuple(x.reshape(orig.shape) for x, orig in zip(xs, (x_prompt, x_sample)))


def kernel(x_prompt, x_sample, mem_prompt, mem_sample, g_ffn1, w_ffn1_gate, w_ffn1_up, w_ffn1_down, g_mix, w_in, w_conv, sink, w_gate, b_gate, w_pa, w_pd, w_pb, w_pc, w_o, g_memq, g_memkv, w_mq, w_mkv, w_mo, g_ffn2, w_ffn2_gate, w_ffn2_up, w_ffn2_down, g_final):
    depth, d = g_ffn1.shape

    def gain(g):
        return g.reshape(depth, 1, d)

    def tiles(lo, width):
        return tuple(range(lo // PROJ_TILE, (lo + width) // PROJ_TILE))

    qa, ka, va = (tiles(i * A_WIDTH, A_WIDTH) for i in range(3))
    d_start = 3 * A_WIDTH
    d_width = (D_Q_HEADS + 2 * D_KV_HEADS) * HEAD_DIM
    conv = tiles(d_start + d_width, 3 * B_WIDTH)
    fourier = tiles(d_start + d_width + 3 * B_WIDTH, C_WIDTH)
    source_tiles = ((qa[0], ka[0]) + tiles(d_start, d_width) + (va[0],) + conv
                    + (qa[1], ka[1], va[1], qa[2], ka[2], va[2]) + fourier)
    assert len(source_tiles) == len(TILE_KINDS)
    w_in_tiles = _cast_w_in_tiles(w_in, source_tiles)
    tn = min(MERGE_COLS, d)
    w_gate_slabs = _cast_gate_slabs(w_gate, tn)
    b_gate_slabs = (b_gate.reshape(depth, 4, d // tn, tn).transpose(0, 2, 1, 3)
                    .reshape(depth, d // tn, 1, 4 * tn))
    tf = min(FFN_COLS, w_ffn1_gate.shape[-1])
    p = {
        "g_ffn1": gain(g_ffn1), "w_ffn1_gate_up": _cast_gate_up_slabs(w_ffn1_gate, w_ffn1_up, tf),
        "w_ffn1_down": w_ffn1_down.astype(BF),
        "g_mix": gain(g_mix), "w_in": w_in_tiles,
        "w_conv": w_conv, "sink": sink,
        "w_gate": w_gate_slabs, "b_gate": b_gate_slabs,
        "w_pa": w_pa.astype(BF), "w_pd": w_pd.astype(BF), "w_pb": w_pb.astype(BF), "w_pc": w_pc.astype(BF),
        "w_o": w_o.astype(BF),
        "g_memq": gain(g_memq), "g_memkv": gain(g_memkv), "w_mq": w_mq.astype(BF),
        "w_mkv": w_mkv.astype(BF), "w_mo": w_mo.astype(BF),
        "g_ffn2": gain(g_ffn2), "w_ffn2_gate_up": _cast_gate_up_slabs(w_ffn2_gate, w_ffn2_up, tf),
        "w_ffn2_down": w_ffn2_down.astype(BF),
    }
    return _forward(x_prompt, x_sample, mem_prompt, mem_sample, p, g_final)
```

```python
import functools

import jax
import jax.numpy as jnp
from jax import lax
from jax.experimental import pallas as pl
from jax.experimental.pallas import tpu as pltpu

BF = jnp.bfloat16
F32 = jnp.float32

HEAD_DIM = 128
EPS = 1e-6
ROPE_THETA = 10000.0
A_DILATIONS = (1, 4, 16)
A_RADIUS = 64
A_HEADS_PER_GROUP = 4
A_WIDTH = 1536
D_Q_HEADS = 8
D_KV_HEADS = 2
D_RADIUS = 128
B_WIDTH = 1024
C_WIDTH = 1024
C_GROUP_DIM = 128
M_HEADS = 4

QA0_COL, KA0_COL, QD_COL, KD_COL, VD_COL, VA0_COL, BG_COL, CG_COL, XB_COL, Z_WIDTH = (
    0, 512, 1024, 2048, 2304, 2560, 3072, 4096, 5120, 6144)
PROJ_TILE = 512
GROUP_WIDTH = 512
PLAIN, ROT_Q, ROT_K, ROT_HALF = 0, 1, 2, 3
TILE_KINDS = ((ROT_Q, ROT_K, ROT_Q, ROT_Q, ROT_HALF) + (PLAIN,) * 7
              + (ROT_Q, ROT_K, PLAIN) * 2
              + (PLAIN,) * 2)
Z_TILES, G_TILES, U_TILES = 12, 3, 2

VMEM_LIMIT_BYTES = 60 * 1024 * 1024
SUBLANES = 8
BF16_SUBLANES = 16

FFN_ROWS = 1024
FFN_COLS = 512
PROJ_ROWS = 1024
MERGE_ROWS = 512
MERGE_COLS = 256
WIDE_ROWS = 1024
ROW_TILE = 512
ATTN_ROWS = 1024
ATTN_ROWS_WIDE = 2048
ROW_CHUNK = 128
SOFTMAX_ROWS = 32
DFT_RESIDUES = 4
DFT_K2 = 16
NEG_BIG = -1e30


def _cparams(ndim):
    return pltpu.CompilerParams(dimension_semantics=("arbitrary",) * ndim,
                                vmem_limit_bytes=VMEM_LIMIT_BYTES)


def _grid_spec(grid, in_specs, out_specs, scratch=()):
    return pltpu.PrefetchScalarGridSpec(num_scalar_prefetch=1, grid=grid, in_specs=in_specs,
                                        out_specs=out_specs, scratch_shapes=list(scratch))


def _rms(x, g):
    return x * lax.rsqrt(jnp.mean(x * x, axis=-1, keepdims=True) + EPS) * g


def _dot(a, b):
    return jnp.dot(a, b, preferred_element_type=F32)


def _dot_t(a, b):
    return lax.dot_general(a, b, (((1,), (1,)), ((), ())), preferred_element_type=F32)


def _ffn_body(l_ref, x_ref, g_ref, wgu_ref, wd_ref, *rest, n_steps, final):
    gf_ref = rest[0] if final else None
    o_ref, h_ref = rest[-2:]
    j = pl.program_id(1)
    tf = wd_ref.shape[0]

    def step(first, last):
        if first:
            h_ref[...] = _rms(x_ref[...], g_ref[...]).astype(BF)
        gate_up = _dot(h_ref[...], wgu_ref[...])
        gate, up = gate_up[:, :tf], gate_up[:, tf:]
        a = (0.5 * gate * jax.nn.sigmoid(gate) * up).astype(BF)
        o = (x_ref[...] if first else o_ref[...]) + _dot(a, wd_ref[...])
        o_ref[...] = _rms(o, gf_ref[...]) if final and last else o

    if n_steps == 1:
        step(True, True)
    else:
        pl.when(j == 0)(functools.partial(step, True, False))
        pl.when((j > 0) & (j < n_steps - 1))(functools.partial(step, False, False))
        pl.when(j == n_steps - 1)(functools.partial(step, False, True))


def _ffn(lidx, x, g, w_gate_up, wd, final_gain=None):
    m, d = x.shape
    tf = w_gate_up.shape[-1] // 2
    f = wd.shape[1]
    tm = min(FFN_ROWS, m)
    final = final_gain is not None
    in_specs = [pl.BlockSpec((tm, d), lambda i, j, l: (i, 0)),
                pl.BlockSpec((None, 1, d), lambda i, j, l: (l[0], 0, 0)),
                pl.BlockSpec((None, None, d, 2 * tf), lambda i, j, l: (l[0], j, 0, 0)),
                pl.BlockSpec((None, tf, d), lambda i, j, l: (l[0], j, 0))]
    args = [lidx, x, g, w_gate_up, wd]
    if final:
        in_specs.append(pl.BlockSpec((1, d), lambda i, j, l: (0, 0)))
        args.append(final_gain)
    return pl.pallas_call(
        functools.partial(_ffn_body, n_steps=f // tf, final=final),
        grid_spec=_grid_spec((m // tm, f // tf), in_specs,
                             pl.BlockSpec((tm, d), lambda i, j, l: (i, 0)),
                             [pltpu.VMEM((tm, d), BF)]),
        out_shape=jax.ShapeDtypeStruct((m, d), F32),
        compiler_params=_cparams(2),
    )(*args)


def _proj_body(l_ref, x_ref, g_ref, w_ref, o_ref, h_ref):
    @pl.when(pl.program_id(1) == 0)
    def _():
        h_ref[...] = _rms(x_ref[...], g_ref[...]).astype(BF)

    o_ref[...] = _dot(h_ref[...], w_ref[...]).astype(o_ref.dtype)


def _proj_in_body(l_ref, kind_ref, x_ref, g_ref, w_ref, cos_ref, sin_ref,
                  z_ref, g1_ref, g2_ref, uc_ref, h_ref, acc_ref, res_ref, *, tm, s1):
    j = pl.program_id(2)
    n_tiles = len(TILE_KINDS)
    chunks = PROJ_TILE // HEAD_DIM
    half_chunks = (VD_COL - KD_COL) // HEAD_DIM
    g1_0, g2_0, u_0 = Z_TILES, Z_TILES + G_TILES, Z_TILES + 2 * G_TILES
    n_rot = {PLAIN: 0, ROT_Q: chunks, ROT_K: chunks, ROT_HALF: half_chunks}

    def multiply_tile():
        acc_ref[...] = _dot(h_ref[...], w_ref[...])

    @pl.when(j == 0)
    def _():
        h_ref[...] = _rms(x_ref[...], g_ref[...]).astype(BF)
        multiply_tile()

    def lanes(c):
        return slice(c * HEAD_DIM, (c + 1) * HEAD_DIM)

    def finished(c, tile_kind):
        xc = acc_ref[:, lanes(c)]
        if c >= n_rot[tile_kind]:
            return xc
        rotated = xc * cos_ref[...] + pltpu.roll(xc, HEAD_DIM // 2, 1) * sin_ref[...]
        return rotated * HEAD_DIM ** -0.5 if tile_kind == ROT_Q else rotated

    def store_natural(tile_kind):
        for c in range(chunks):
            z_ref[:, lanes(c)] = finished(c, tile_kind).astype(BF)

    def store_residue_major(dst_ref, dil, tile_kind):
        for c in range(chunks):
            res_ref[c] = finished(c, tile_kind)
        rows = tm // dil
        for r in range(dil):
            for c in range(chunks):
                dst_ref[r, :, lanes(c)] = res_ref[c, pl.ds(r, rows, stride=dil), :].astype(BF)

    t = j - 1
    kind = kind_ref[jnp.maximum(t, 0)]
    destinations = (
        (0, g1_0, (PLAIN, ROT_Q, ROT_K, ROT_HALF), store_natural),
        (g1_0, g2_0, (PLAIN, ROT_Q, ROT_K), functools.partial(store_residue_major, g1_ref, A_DILATIONS[1])),
        (g2_0, u_0, (PLAIN, ROT_Q, ROT_K), functools.partial(store_residue_major, g2_ref, A_DILATIONS[2])),
        (u_0, n_tiles, (PLAIN,), functools.partial(store_residue_major, uc_ref, s1)),
    )
    for lo, hi, kinds, store in destinations:
        for tile_kind in kinds:
            in_range = (t >= lo) & (t < hi) & (kind == tile_kind)

            @pl.when(in_range & (j < n_tiles))
            def _(store=store, tile_kind=tile_kind):
                store(tile_kind)
                multiply_tile()

            if hi == n_tiles:
                @pl.when(in_range & (j == n_tiles))
                def _(store=store, tile_kind=tile_kind):
                    store(tile_kind)


def _proj(lidx, x, g, w, out_dtype=BF, tn=512):
    m, d = x.shape
    n = w.shape[-1]
    tm, tn = min(ROW_TILE, m), min(tn, n)
    return pl.pallas_call(
        _proj_body,
        grid_spec=_grid_spec(
            (m // tm, n // tn),
            [pl.BlockSpec((tm, d), lambda i, j, l: (i, 0)),
             pl.BlockSpec((None, 1, d), lambda i, j, l: (l[0], 0, 0)),
             pl.BlockSpec((None, d, tn), lambda i, j, l: (l[0], 0, j))],
            pl.BlockSpec((tm, tn), lambda i, j, l: (i, j)),
            [pltpu.VMEM((tm, d), BF)]),
        out_shape=jax.ShapeDtypeStruct((m, n), out_dtype),
        compiler_params=_cparams(2),
    )(lidx, x, g, w)


def _proj_in(lidx, x, g, w, cos_t, sin_t, batch, seq, s1):
    m, d = x.shape
    tm, tn = min(PROJ_ROWS, m), PROJ_TILE
    n_tiles = len(TILE_KINDS)
    d1, d2 = A_DILATIONS[1], A_DILATIONS[2]
    assert w.shape[1:] == (n_tiles, d, tn) and seq % tm == 0 and tm % (BF16_SUBLANES * max(d2, s1)) == 0
    nsb = seq // tm
    g1_0, g2_0, u_0 = Z_TILES, Z_TILES + G_TILES, Z_TILES + 2 * G_TILES

    def residue_major(dil, first_tile, tiles):
        return pl.BlockSpec((None, dil, tm // dil, tn),
                            lambda b, s, j, l, k: (b, 0, s, jnp.clip(j - 1 - first_tile, 0, tiles - 1)))

    return pl.pallas_call(
        functools.partial(_proj_in_body, tm=tm, s1=s1),
        grid_spec=pltpu.PrefetchScalarGridSpec(
            num_scalar_prefetch=2,
            grid=(batch, nsb, n_tiles + 1),
            in_specs=[pl.BlockSpec((tm, d), lambda b, s, j, l, k: (b * nsb + s, 0)),
                      pl.BlockSpec((None, 1, d), lambda b, s, j, l, k: (l[0], 0, 0)),
                      pl.BlockSpec((None, None, d, tn),
                                   lambda b, s, j, l, k: (l[0], jnp.minimum(j, n_tiles - 1), 0, 0)),
                      pl.BlockSpec((tm, HEAD_DIM), lambda b, s, j, l, k: (s, 0)),
                      pl.BlockSpec((tm, HEAD_DIM), lambda b, s, j, l, k: (s, 0))],
            out_specs=[pl.BlockSpec((tm, tn),
                                    lambda b, s, j, l, k: (b * nsb + s, jnp.clip(j - 1, 0, Z_TILES - 1))),
                       residue_major(d1, g1_0, G_TILES),
                       residue_major(d2, g2_0, G_TILES),
                       residue_major(s1, u_0, U_TILES)],
            scratch_shapes=[pltpu.VMEM((tm, d), BF), pltpu.VMEM((tm, tn), F32),
                            pltpu.VMEM((tn // HEAD_DIM, tm, HEAD_DIM), F32)]),
        out_shape=[jax.ShapeDtypeStruct((m, Z_WIDTH), BF),
                   jax.ShapeDtypeStruct((batch, d1, seq // d1, G_TILES * tn), BF),
                   jax.ShapeDtypeStruct((batch, d2, seq // d2, G_TILES * tn), BF),
                   jax.ShapeDtypeStruct((batch, s1, seq // s1, U_TILES * tn), BF)],
        compiler_params=_cparams(3),
    )(lidx, jnp.asarray(TILE_KINDS, jnp.int32), x, g, w, cos_t, sin_t)


def _band_attn_body(l_ref, q_ref, kp_ref, kc_ref, kn_ref, vp_ref, vc_ref, vn_ref, *rest,
                    tq, rps, length, radius, n_heads, rep, has_sink, want_lse, merge_dils):
    rest = list(rest)
    sink_ref = rest.pop(0) if has_sink else None
    others = [rest.pop(0) for _ in range(2 * len(merge_dils))]
    o_ref = rest.pop(0)
    lse_ref = rest.pop(0) if want_lse else None
    s_ref, p_ref, inv_ref = rest[:3]
    if merge_dils:
        lse0_ref, nat_ref = rest[3:]
        for a, src_ref in enumerate(others):
            dil = merge_dils[a // 2]
            for r in range(dil):
                for h in range(n_heads):
                    nat_ref[a * n_heads + h, pl.ds(r, tq // dil, stride=dil), :] = (
                        src_ref[r, :, h * HEAD_DIM:(h + 1) * HEAD_DIM].astype(F32))
    per_sequence = [[ref.at[ri] for ref in (q_ref, kp_ref, kc_ref, kn_ref, vp_ref, vc_ref, vn_ref,
                                            o_ref, lse_ref) if ref is not None] for ri in range(rps)]
    i = pl.program_id(2)
    rc = ROW_CHUNK
    nc = tq // rc
    win = rc + 2 * radius
    groups = n_heads // rep
    assert radius <= rc and radius % 16 == 0

    def window(prev_ref, cur_ref, next_ref, c, cols):
        lo, hi = c * rc - radius, (c + 1) * rc + radius
        parts = []
        if lo < 0:
            parts.append(prev_ref[rc - radius:rc, cols])
        parts.append(cur_ref[max(lo, 0):min(hi, tq), cols])
        if hi > tq:
            parts.append(next_ref[0:radius, cols])
        return parts[0] if len(parts) == 1 else jnp.concatenate(parts, axis=0)

    row = lax.broadcasted_iota(jnp.int32, (rc, win), 0)
    col = lax.broadcasted_iota(jnp.int32, (rc, win), 1)
    band_bias = jnp.where(jnp.abs(col - radius - row) <= radius, 0.0, NEG_BIG)
    key_col = lax.broadcasted_iota(jnp.int32, (1, win), 1)
    rb = SOFTMAX_ROWS
    for step, (ri, c) in enumerate((ri, c) for ri in range(rps) for c in range(nc)):
        q_ref, kp_ref, kc_ref, kn_ref, vp_ref, vc_ref, vn_ref, o_ref = per_sequence[ri][:8]
        lse_ref = per_sequence[ri][8] if want_lse else None
        slot = step % 2
        kpos = i * tq + c * rc - radius + key_col
        bias = band_bias + jnp.where(kpos >= 0, jnp.where(kpos < length, 0.0, NEG_BIG), NEG_BIG)
        rows = slice(c * rc, (c + 1) * rc)
        for g in range(groups):
            k = window(kp_ref, kc_ref, kn_ref, c, slice(g * HEAD_DIM, (g + 1) * HEAD_DIM))
            q = q_ref[rows, g * rep * HEAD_DIM:(g + 1) * rep * HEAD_DIM]
            if rep > 1:
                q = jnp.concatenate([q[:, r * HEAD_DIM:(r + 1) * HEAD_DIM] for r in range(rep)], axis=0)
            s_ref[slot, g * rep * rc:(g + 1) * rep * rc, :] = _dot_t(q, k)
        for blk in range(n_heads * rc // rb):
            h, r0 = divmod(blk * rb, rc)
            srows = slice(blk * rb, (blk + 1) * rb)
            s = s_ref[slot, srows, :] + bias[r0:r0 + rb]
            m = jnp.max(s, axis=-1, keepdims=True)
            if has_sink:
                sk = sink_ref[l_ref[0], h]
                m = jnp.maximum(m, sk)
            p = jnp.exp(s - m)
            den = jnp.sum(p, axis=-1, keepdims=True)
            if has_sink:
                den = den + jnp.exp(sk - m)
            p_ref[slot, srows, :] = p.astype(BF)
            inv_ref[slot, srows, :] = jnp.broadcast_to(1.0 / den, (rb, HEAD_DIM))
            if want_lse:
                lse_ref[c * rc + r0:c * rc + r0 + rb, h * HEAD_DIM:(h + 1) * HEAD_DIM] = (
                    jnp.broadcast_to(m + jnp.log(den), (rb, HEAD_DIM)))
            if merge_dils:
                lse0_ref[slot, srows, :] = jnp.broadcast_to(m + jnp.log(den), (rb, HEAD_DIM))
        for g in range(groups):
            v = window(vp_ref, vc_ref, vn_ref, c, slice(g * HEAD_DIM, (g + 1) * HEAD_DIM))
            grows = slice(g * rep * rc, (g + 1) * rep * rc)
            o = _dot(p_ref[slot, grows, :], v) * inv_ref[slot, grows, :]
            for r in range(rep):
                h = g * rep + r
                o_h = o[r * rc:(r + 1) * rc]
                if merge_dils:
                    lses = [lse0_ref[slot, h * rc:(h + 1) * rc, :]]
                    outs = [o_h]
                    for a in range(len(merge_dils)):
                        outs.append(nat_ref[2 * a * n_heads + h, rows, :])
                        lses.append(nat_ref[(2 * a + 1) * n_heads + h, rows, :])
                    top = functools.reduce(jnp.maximum, lses)
                    weights = [jnp.exp(l - top) for l in lses]
                    o_h = sum(w * x for w, x in zip(weights, outs)) / sum(weights)
                o_ref[rows, h * HEAD_DIM:(h + 1) * HEAD_DIM] = o_h.astype(o_ref.dtype)


def _band_attention(lidx, src, q_col, k_col, v_col, n_heads, rep, radius, sink=None, want_lse=False,
                    merge_with=()):
    batch, dil, length, width = src.shape
    tq = min(ATTN_ROWS if merge_with else ATTN_ROWS_WIDE, length)
    nc = tq // ROW_CHUNK
    qw, kw = n_heads * HEAD_DIM, (n_heads // rep) * HEAD_DIM
    assert length % tq == 0 and tq % ROW_CHUNK == 0
    assert q_col % qw == 0 and k_col % kw == 0 and v_col % kw == 0
    last_chunk = length // ROW_CHUNK - 1
    rps = max(1, min(dil, ATTN_ROWS // tq))
    assert dil % rps == 0

    def kv_specs(col):
        cb = col // kw
        return [
            pl.BlockSpec((None, rps, ROW_CHUNK, kw),
                         lambda b, r, i, l: (b, r, jnp.maximum(i * nc - 1, 0), cb)),
            pl.BlockSpec((None, rps, tq, kw), lambda b, r, i, l: (b, r, i, cb)),
            pl.BlockSpec((None, rps, ROW_CHUNK, kw),
                         lambda b, r, i, l: (b, r, jnp.minimum((i + 1) * nc, last_chunk), cb)),
        ]

    in_specs = [pl.BlockSpec((None, rps, tq, qw), lambda b, r, i, l: (b, r, i, q_col // qw))]
    in_specs += kv_specs(k_col) + kv_specs(v_col)
    args = [lidx] + [src] * 7
    if sink is not None:
        in_specs.append(pl.BlockSpec(memory_space=pltpu.SMEM))
        args.append(sink)
    scratch = [pltpu.VMEM((2, n_heads * ROW_CHUNK, ROW_CHUNK + 2 * radius), F32),
               pltpu.VMEM((2, n_heads * ROW_CHUNK, ROW_CHUNK + 2 * radius), BF),
               pltpu.VMEM((2, n_heads * ROW_CHUNK, HEAD_DIM), F32)]
    merge_dils = tuple(o.shape[1] for o, _ in merge_with)
    if merge_with:
        assert dil == 1 and not want_lse and all(tq % (BF16_SUBLANES * g) == 0 for g in merge_dils)
        for pair in merge_with:
            for arr in pair:
                g = arr.shape[1]
                in_specs.append(pl.BlockSpec((None, g, tq // g, qw), lambda b, r, i, l: (b, 0, i, 0)))
                args.append(arr)
        scratch += [pltpu.VMEM((2, n_heads * ROW_CHUNK, HEAD_DIM), F32),
                    pltpu.VMEM((2 * len(merge_with) * n_heads, tq, HEAD_DIM), F32)]
    out_spec = pl.BlockSpec((None, rps, tq, qw), lambda b, r, i, l: (b, r, i, 0))
    out_shape = [jax.ShapeDtypeStruct((batch, dil, length, qw), BF)]
    out_specs = [out_spec]
    if want_lse:
        out_shape.append(jax.ShapeDtypeStruct((batch, dil, length, qw), F32))
        out_specs.append(out_spec)
    return pl.pallas_call(
        functools.partial(_band_attn_body, tq=tq, rps=rps, length=length, radius=radius, n_heads=n_heads,
                          rep=rep, has_sink=sink is not None, want_lse=want_lse, merge_dils=merge_dils),
        grid_spec=_grid_spec((batch, dil // rps, length // tq), in_specs, out_specs, scratch),
        out_shape=out_shape,
        compiler_params=_cparams(3),
    )(*args)


def _conv_body(l_ref, bg_ref, cg_ref, xb_ref, cgp_ref, xbp_ref, cgn_ref, xbn_ref, w_ref, o_ref,
               *, tm, seq):
    i = pl.program_id(0)
    u = cg_ref[...].astype(F32) * xb_ref[...].astype(F32)
    not_first = ((i * tm) % seq != 0).astype(F32)
    not_last = (((i + 1) * tm) % seq != 0).astype(F32)
    last = SUBLANES - 1
    u_prev = cgp_ref[last:, :].astype(F32) * xbp_ref[last:, :].astype(F32) * not_first
    u_next = cgn_ref[0:1, :].astype(F32) * xbn_ref[0:1, :].astype(F32) * not_last
    row = lax.broadcasted_iota(jnp.int32, u.shape, 0)
    below = jnp.where(row == 0, u_prev, pltpu.roll(u, 1, 0))
    above = jnp.where(row == tm - 1, u_next, pltpu.roll(u, tm - 1, 0))
    y = below * w_ref[0:1, :] + u * w_ref[1:2, :] + above * w_ref[2:3, :]
    o_ref[...] = (bg_ref[...].astype(F32) * y).astype(o_ref.dtype)


def _short_conv(lidx, z, w_conv, seq):
    m = z.shape[0]
    tm, tc = min(WIDE_ROWS, m), B_WIDTH
    halo = SUBLANES
    assert seq % tm == 0 and B_WIDTH % tc == 0
    last_halo = m // halo - 1

    def tile(col):
        return pl.BlockSpec((tm, tc), lambda i, j, l: (i, col // tc + j))

    def prev(col):
        return pl.BlockSpec((halo, tc), lambda i, j, l: (jnp.maximum(i * (tm // halo) - 1, 0), col // tc + j))

    def nxt(col):
        return pl.BlockSpec((halo, tc),
                            lambda i, j, l: (jnp.minimum((i + 1) * (tm // halo), last_halo), col // tc + j))

    return pl.pallas_call(
        functools.partial(_conv_body, tm=tm, seq=seq),
        grid_spec=_grid_spec(
            (m // tm, B_WIDTH // tc),
            [tile(BG_COL), tile(CG_COL), tile(XB_COL), prev(CG_COL), prev(XB_COL), nxt(CG_COL), nxt(XB_COL),
             pl.BlockSpec((None, 3, tc), lambda i, j, l: (l[0], 0, j))],
            pl.BlockSpec((tm, tc), lambda i, j, l: (i, j))),
        out_shape=jax.ShapeDtypeStruct((m, B_WIDTH), BF),
        compiler_params=_cparams(2),
    )(lidx, z, z, z, z, z, z, z, w_conv)


def _dft1_body(l_ref, u_ref, g_ref, ar_ref, ai_ref, *, tn1, tk2, s2, groups):
    for jj in range(tn1):
        r = _dot(g_ref[jj], u_ref[jj])
        for g in range(groups):
            gc = slice(g * C_GROUP_DIM, (g + 1) * C_GROUP_DIM)
            for t in range(s2 // tk2):
                dst = slice(jj * tk2, (jj + 1) * tk2)
                ar_ref[g, t, dst, :] = r[t * tk2:(t + 1) * tk2, gc]
                ai_ref[g, t, dst, :] = r[s2 + t * tk2:s2 + (t + 1) * tk2, gc]


def _dft2_body(l_ref, ar_ref, ai_ref, m2_ref, csc_ref, y_ref, xr_ref, xi_ref,
               *, tk2, s1, groups, scale):
    for j in range(tk2):
        rows = pl.ds(j, s1, stride=tk2)
        a_re = jnp.concatenate([ar_ref[g, rows, :] for g in range(groups)], axis=1)
        a_im = jnp.concatenate([ai_ref[g, rows, :] for g in range(groups)], axis=1)
        x = _dot(m2_ref[...], jnp.concatenate([a_re, a_im], axis=0).astype(BF))
        for g in range(groups):
            gc = slice(g * C_GROUP_DIM, (g + 1) * C_GROUP_DIM)
            xr_ref[g, rows, :] = x[:s1, gc]
            xi_ref[g, rows, :] = x[s1:, gc]
    for g in range(groups):
        gc = slice(g * C_GROUP_DIM, (g + 1) * C_GROUP_DIM)
        x_g = jnp.concatenate([xr_ref[g], xi_ref[g]], axis=1).astype(BF)
        yg = _dot(x_g, csc_ref[...]) * scale
        for k1 in range(s1):
            y_ref[k1, :, gc] = yg[k1 * tk2:(k1 + 1) * tk2].astype(y_ref.dtype)


def _fourier_mix(lidx, u, tables):
    g1, m2, csc = tables
    batch, s1, s2, width = u.shape
    groups = width // C_GROUP_DIM
    tn1, tk2 = DFT_RESIDUES, DFT_K2
    a_shape = jax.ShapeDtypeStruct((batch, groups, s2 // tk2, s1 * tk2, C_GROUP_DIM), F32)
    a_out = pl.BlockSpec((None, groups, s2 // tk2, tn1 * tk2, C_GROUP_DIM), lambda b, t, l: (b, 0, 0, t, 0))
    ar, ai = pl.pallas_call(
        functools.partial(_dft1_body, tn1=tn1, tk2=tk2, s2=s2, groups=groups),
        grid_spec=_grid_spec(
            (batch, s1 // tn1),
            [pl.BlockSpec((None, tn1, s2, width), lambda b, t, l: (b, t, 0, 0)),
             pl.BlockSpec((tn1, 2 * s2, s2), lambda b, t, l: (t, 0, 0))],
            [a_out, a_out]),
        out_shape=[a_shape, a_shape],
        compiler_params=_cparams(2),
    )(lidx, u, g1)
    a_in = pl.BlockSpec((None, groups, None, s1 * tk2, C_GROUP_DIM), lambda b, t, l: (b, 0, t, 0, 0))
    y = pl.pallas_call(
        functools.partial(_dft2_body, tk2=tk2, s1=s1, groups=groups,
                          scale=float((s1 * s2 * C_GROUP_DIM) ** -0.5)),
        grid_spec=_grid_spec(
            (batch, s2 // tk2),
            [a_in, a_in,
             pl.BlockSpec((2 * s1, 2 * s1), lambda b, t, l: (0, 0)),
             pl.BlockSpec((2 * C_GROUP_DIM, C_GROUP_DIM), lambda b, t, l: (0, 0))],
            pl.BlockSpec((None, s1, tk2, width), lambda b, t, l: (b, 0, t, 0)),
            [pltpu.VMEM((groups, tk2 * s1, C_GROUP_DIM), F32)] * 2),
        out_shape=jax.ShapeDtypeStruct((batch, s1, s2, width), BF),
        compiler_params=_cparams(2),
    )(lidx, ar, ai, m2, csc)
    return y.reshape(batch * s1 * s2, width)


def _dft_split(seq):
    s2 = 256 if seq >= 8192 else 128
    return seq // s2, s2


def _dft_tables(seq):
    s1, s2 = _dft_split(seq)
    two_pi = 2.0 * jnp.pi

    def cs(idx, period):
        ang = (idx % period).astype(F32) * (two_pi / period)
        return jnp.cos(ang), jnp.sin(ang)

    n1 = jnp.arange(s1, dtype=jnp.int32)[:, None, None]
    k2 = jnp.arange(s2, dtype=jnp.int32)[None, :, None]
    n2 = jnp.arange(s2, dtype=jnp.int32)[None, None, :]
    ca, sa = cs(k2 * (n1 + s1 * n2), seq)
    g1 = jnp.concatenate([ca, -sa], axis=1).astype(BF)
    a = jnp.arange(s1, dtype=jnp.int32)
    cb, sb = cs(a[:, None] * a[None, :], s1)
    m2 = jnp.concatenate([jnp.concatenate([cb, sb], axis=1),
                          jnp.concatenate([-sb, cb], axis=1)], axis=0).astype(BF)
    c = jnp.arange(C_GROUP_DIM, dtype=jnp.int32)
    cc, sc = cs(c[:, None] * c[None, :], C_GROUP_DIM)
    return g1, m2, jnp.concatenate([cc, sc], axis=0).astype(BF)


def _memory_update(x, g_ref, wq_ref, kv_ref, wo_ref, att_ref):
    h = _rms(x, g_ref[...]).astype(BF)
    q = (_dot(h, wq_ref[...]) * HEAD_DIM ** -0.5).astype(BF)
    mw = M_HEADS * HEAD_DIM
    for hh in range(M_HEADS):
        cols = slice(hh * HEAD_DIM, (hh + 1) * HEAD_DIM)
        k = kv_ref[:, cols]
        v = kv_ref[:, mw + hh * HEAD_DIM:mw + (hh + 1) * HEAD_DIM]
        s = _dot_t(q[:, cols], k)
        p = jnp.exp(s - jnp.max(s, axis=-1, keepdims=True))
        o = _dot(p.astype(BF), v) / jnp.sum(p, axis=-1, keepdims=True)
        att_ref[:, cols] = o.astype(BF)
    return x + _dot(att_ref[...], wo_ref[...])


def _merge_body(l_ref, x_ref, g_ref, wgate_ref, bgate_ref, ya_ref, yd_ref, yb_ref, yc_ref,
                wpa_ref, wpd_ref, wpb_ref, wpc_ref, wo_ref, gq_ref, wq_ref, kv_ref, wmo_ref,
                o_ref, h_ref, att_ref, *, tn):
    n = pl.program_id(1)

    def step(first):
        if first:
            h_ref[...] = _rms(x_ref[...], g_ref[...]).astype(BF)
        gates = jax.nn.sigmoid(_dot(h_ref[...], wgate_ref[...]) + bgate_ref[...])
        merged = None
        for i, (br_ref, wp_ref) in enumerate(((ya_ref, wpa_ref), (yd_ref, wpd_ref),
                                              (yb_ref, wpb_ref), (yc_ref, wpc_ref))):
            term = gates[:, i * tn:(i + 1) * tn] * _dot(br_ref[...], wp_ref[...])
            merged = term if merged is None else merged + term
        o_ref[...] = (x_ref[...] if first else o_ref[...]) + _dot(merged.astype(BF), wo_ref[...])

    pl.when(n == 0)(functools.partial(step, True))
    pl.when(n > 0)(functools.partial(step, False))

    @pl.when(n == pl.num_programs(1) - 1)
    def _():
        o_ref[...] = _memory_update(o_ref[...], gq_ref, wq_ref, kv_ref, wmo_ref, att_ref)


def _merge(lidx, x, g, w_gate, b_gate, branches, projs, w_o, g_memq, w_mq, kv, w_mo, seq):
    m, d = x.shape
    tm = min(MERGE_ROWS, m)
    tn = w_gate.shape[-1] // 4
    n_mem, kvw = kv.shape[1], kv.shape[2]
    mw = w_mq.shape[-1]
    assert seq % tm == 0

    def row_tile(a):
        return pl.BlockSpec((tm, a.shape[1]), lambda i, n, l: (i, 0))

    def proj_tile(w):
        return pl.BlockSpec((None, w.shape[1], tn), lambda i, n, l: (l[0], 0, n))

    return pl.pallas_call(
        functools.partial(_merge_body, tn=tn),
        grid_spec=_grid_spec(
            (m // tm, d // tn),
            [row_tile(x),
             pl.BlockSpec((None, 1, d), lambda i, n, l: (l[0], 0, 0)),
             pl.BlockSpec((None, None, d, 4 * tn), lambda i, n, l: (l[0], n, 0, 0)),
             pl.BlockSpec((None, None, 1, 4 * tn), lambda i, n, l: (l[0], n, 0, 0))]
            + [row_tile(b) for b in branches] + [proj_tile(w) for w in projs]
            + [pl.BlockSpec((None, tn, d), lambda i, n, l: (l[0], n, 0)),
               pl.BlockSpec((None, 1, d), lambda i, n, l: (l[0], 0, 0)),
               pl.BlockSpec((None, d, mw), lambda i, n, l: (l[0], 0, 0)),
               pl.BlockSpec((None, n_mem, kvw), lambda i, n, l: ((i * tm) // seq, 0, 0)),
               pl.BlockSpec((None, mw, d), lambda i, n, l: (l[0], 0, 0))],
            row_tile(x),
            [pltpu.VMEM((tm, d), BF), pltpu.VMEM((tm, mw), BF)]),
        out_shape=jax.ShapeDtypeStruct((m, d), F32),
        compiler_params=_cparams(2),
    )(lidx, x, g, w_gate, b_gate, *branches, *projs, w_o, g_memq, w_mq, kv, w_mo)


def _cast_body(*refs):
    x_ref, o_ref = refs[-2:]
    o_ref[...] = x_ref[...].astype(o_ref.dtype)


def _cast_w_in_tiles(w_in, source_tiles):
    depth, d, _ = w_in.shape
    n = len(source_tiles)
    return pl.pallas_call(
        _cast_body,
        grid_spec=_grid_spec((depth, n),
                             [pl.BlockSpec((None, d, PROJ_TILE), lambda l, j, t: (l, 0, t[j]))],
                             pl.BlockSpec((None, None, d, PROJ_TILE), lambda l, j, t: (l, j, 0, 0))),
        out_shape=jax.ShapeDtypeStruct((depth, n, d, PROJ_TILE), BF),
        compiler_params=_cparams(2),
    )(jnp.asarray(source_tiles, jnp.int32), w_in)


def _cast_pair_body(a_ref, b_ref, o_ref):
    half = a_ref.shape[-1]
    o_ref[:, :half] = a_ref[...].astype(o_ref.dtype)
    o_ref[:, half:] = b_ref[...].astype(o_ref.dtype)


def _cast_gate_up_slabs(wg, wu, tf):
    depth, d, f = wg.shape
    spec = pl.BlockSpec((None, d, tf), lambda l, j: (l, 0, j))
    return pl.pallas_call(
        _cast_pair_body,
        grid=(depth, f // tf),
        in_specs=[spec, spec],
        out_specs=pl.BlockSpec((None, None, d, 2 * tf), lambda l, j: (l, j, 0, 0)),
        out_shape=jax.ShapeDtypeStruct((depth, f // tf, d, 2 * tf), BF),
        compiler_params=_cparams(2),
    )(wg, wu)


def _cast_gate_slabs(w_gate, tn):
    depth, n_gates, d, _ = w_gate.shape
    return pl.pallas_call(
        _cast_body,
        grid=(depth, d // tn, n_gates),
        in_specs=[pl.BlockSpec((None, None, d, tn), lambda l, n, i: (l, i, 0, n))],
        out_specs=pl.BlockSpec((None, None, d, tn), lambda l, n, i: (l, n, 0, i)),
        out_shape=jax.ShapeDtypeStruct((depth, d // tn, d, n_gates * tn), BF),
        compiler_params=_cparams(3),
    )(w_gate)


def _rope_tables(seq):
    half = HEAD_DIM // 2
    inv = ROPE_THETA ** (-jnp.arange(half, dtype=F32) / half)
    ang = jnp.arange(seq, dtype=jnp.int32).astype(F32)[:, None] * inv[None, :]
    cos, sin = jnp.cos(ang), jnp.sin(ang)
    return jnp.concatenate([cos, cos], axis=1), jnp.concatenate([-sin, sin], axis=1)


def _layer(lidx, x, mem, p, batch, seq, tabs, final_gain=None):
    cos_t, sin_t, dft = tabs
    x = _ffn(lidx, x, p["g_ffn1"], p["w_ffn1_gate_up"], p["w_ffn1_down"])

    s1, _ = _dft_split(seq)
    z, zg1, zg2, uc = _proj_in(lidx, x, p["g_mix"], p["w_in"], cos_t, sin_t, batch, seq, s1)
    z4 = z.reshape(batch, 1, seq, Z_WIDTH)
    dilated = [_band_attention(lidx, src, 0, GROUP_WIDTH, 2 * GROUP_WIDTH, A_HEADS_PER_GROUP, 1,
                               A_RADIUS, want_lse=True) for src in (zg1, zg2)]
    (ya,) = _band_attention(lidx, z4, QA0_COL, KA0_COL, VA0_COL, A_HEADS_PER_GROUP, 1, A_RADIUS,
                            merge_with=dilated)
    ya = ya.reshape(batch * seq, A_HEADS_PER_GROUP * HEAD_DIM)
    (yd,) = _band_attention(lidx, z4, QD_COL, KD_COL, VD_COL, D_Q_HEADS,
                            D_Q_HEADS // D_KV_HEADS, D_RADIUS, sink=p["sink"])
    yd = yd.reshape(batch * seq, D_Q_HEADS * HEAD_DIM)
    yb = _short_conv(lidx, z, p["w_conv"], seq)
    yc = _fourier_mix(lidx, uc, dft)
    kv = _proj(lidx, mem, p["g_memkv"], p["w_mkv"])
    kv = kv.reshape(batch, mem.shape[0] // batch, kv.shape[1])
    x = _merge(lidx, x, p["g_mix"], p["w_gate"], p["b_gate"], (ya, yd, yb, yc),
               (p["w_pa"], p["w_pd"], p["w_pb"], p["w_pc"]), p["w_o"],
               p["g_memq"], p["w_mq"], kv, p["w_mo"], seq)

    return _ffn(lidx, x, p["g_ffn2"], p["w_ffn2_gate_up"], p["w_ffn2_down"], final_gain)


def _forward(x_prompt, x_sample, mem_prompt, mem_sample, p, g_final):
    depth = p["w_o"].shape[0]
    d = x_prompt.shape[-1]
    trunks = []
    for x, mem in ((x_prompt, mem_prompt), (x_sample, mem_sample)):
        batch, seq = x.shape[0], x.shape[1]
        tabs = _rope_tables(seq) + (_dft_tables(seq),)
        trunks.append((batch, seq, tabs, mem.reshape(-1, d)))

    gf = g_final.reshape(1, d)
    xs = [x_prompt.reshape(-1, d), x_sample.reshape(-1, d)]
    for layer in range(depth):
        lidx = jnp.full((1,), layer, jnp.int32)
        xs = [_layer(lidx, x, mem, p, batch, seq, tabs, gf if layer == depth - 1 else None)
              for x, (batch, seq, tabs, mem) in zip(xs, trunks)]
    return tuple(x.reshape(orig.shape) for x, orig in zip(xs, (x_prompt, x_sample)))


def kernel(x_prompt, x_sample, mem_prompt, mem_sample, g_ffn1, w_ffn1_gate, w_ffn1_up, w_ffn1_down, g_mix, w_in, w_conv, sink, w_gate, b_gate, w_pa, w_pd, w_pb, w_pc, w_o, g_memq, g_memkv, w_mq, w_mkv, w_mo, g_ffn2, w_ffn2_gate, w_ffn2_up, w_ffn2_down, g_final):
    depth, d = g_ffn1.shape

    def gain(g):
        return g.reshape(depth, 1, d)

    def tiles(lo, width):
        return tuple(range(lo // PROJ_TILE, (lo + width) // PROJ_TILE))

    qa, ka, va = (tiles(i * A_WIDTH, A_WIDTH) for i in range(3))
    d_start = 3 * A_WIDTH
    d_width = (D_Q_HEADS + 2 * D_KV_HEADS) * HEAD_DIM
    conv = tiles(d_start + d_width, 3 * B_WIDTH)
    fourier = tiles(d_start + d_width + 3 * B_WIDTH, C_WIDTH)
    source_tiles = ((qa[0], ka[0]) + tiles(d_start, d_width) + (va[0],) + conv
                    + (qa[1], ka[1], va[1], qa[2], ka[2], va[2]) + fourier)
    assert len(source_tiles) == len(TILE_KINDS)
    w_in_tiles = _cast_w_in_tiles(w_in, source_tiles)
    tn = min(MERGE_COLS, d)
    w_gate_slabs = _cast_gate_slabs(w_gate, tn)
    b_gate_slabs = (b_gate.reshape(depth, 4, d // tn, tn).transpose(0, 2, 1, 3)
                    .reshape(depth, d // tn, 1, 4 * tn))
    tf = min(FFN_COLS, w_ffn1_gate.shape[-1])
    p = {
        "g_ffn1": gain(g_ffn1), "w_ffn1_gate_up": _cast_gate_up_slabs(w_ffn1_gate, w_ffn1_up, tf),
        "w_ffn1_down": w_ffn1_down.astype(BF),
        "g_mix": gain(g_mix), "w_in": w_in_tiles,
        "w_conv": w_conv, "sink": sink,
        "w_gate": w_gate_slabs, "b_gate": b_gate_slabs,
        "w_pa": w_pa.astype(BF), "w_pd": w_pd.astype(BF), "w_pb": w_pb.astype(BF), "w_pc": w_pc.astype(BF),
        "w_o": w_o.astype(BF),
        "g_memq": gain(g_memq), "g_memkv": gain(g_memkv), "w_mq": w_mq.astype(BF),
        "w_mkv": w_mkv.astype(BF), "w_mo": w_mo.astype(BF),
        "g_ffn2": gain(g_ffn2), "w_ffn2_gate_up": _cast_gate_up_slabs(w_ffn2_gate, w_ffn2_up, tf),
        "w_ffn2_down": w_ffn2_down.astype(BF),
    }
    return _forward(x_prompt, x_sample, mem_prompt, mem_sample, p, g_final)
```

```python
import functools

import jax
import jax.numpy as jnp
from jax import lax
from jax.experimental import pallas as pl
from jax.experimental.pallas import tpu as pltpu

BF = jnp.bfloat16
F32 = jnp.float32

HEAD_DIM = 128
EPS = 1e-6
ROPE_THETA = 10000.0
A_DILATIONS = (1, 4, 16)
A_RADIUS = 64
A_HEADS_PER_GROUP = 4
A_WIDTH = 1536
D_Q_HEADS = 8
D_KV_HEADS = 2
D_RADIUS = 128
B_WIDTH = 1024
C_WIDTH = 1024
C_GROUP_DIM = 128
M_HEADS = 4

QA0_COL, KA0_COL, QD_COL, KD_COL, VD_COL, VA0_COL, BG_COL, CG_COL, XB_COL, Z_WIDTH = (
    0, 512, 1024, 2048, 2304, 2560, 3072, 4096, 5120, 6144)
PROJ_TILE = 512
GROUP_WIDTH = 512
PLAIN, ROT_Q, ROT_K, ROT_HALF = 0, 1, 2, 3
TILE_KINDS = ((ROT_Q, ROT_K, ROT_Q, ROT_Q, ROT_HALF) + (PLAIN,) * 7
              + (ROT_Q, ROT_K, PLAIN) * 2
              + (PLAIN,) * 2)
Z_TILES, G_TILES, U_TILES = 12, 3, 2

VMEM_LIMIT_BYTES = 60 * 1024 * 1024
SUBLANES = 8
BF16_SUBLANES = 16

FFN_ROWS = 1024
FFN_COLS = 512
PROJ_ROWS = 1024
MERGE_ROWS = 512
MERGE_COLS = 256
WIDE_ROWS = 1024
ROW_TILE = 512
ATTN_ROWS = 1024
ROW_CHUNK = 128
SOFTMAX_ROWS = 32
DFT_RESIDUES = 4
DFT_K2 = 16
NEG_BIG = -1e30


def _cparams(ndim):
    return pltpu.CompilerParams(dimension_semantics=("arbitrary",) * ndim,
                                vmem_limit_bytes=VMEM_LIMIT_BYTES)


def _grid_spec(grid, in_specs, out_specs, scratch=()):
    return pltpu.PrefetchScalarGridSpec(num_scalar_prefetch=1, grid=grid, in_specs=in_specs,
                                        out_specs=out_specs, scratch_shapes=list(scratch))


def _rms(x, g):
    return x * lax.rsqrt(jnp.mean(x * x, axis=-1, keepdims=True) + EPS) * g


def _dot(a, b):
    return jnp.dot(a, b, preferred_element_type=F32)


def _dot_t(a, b):
    return lax.dot_general(a, b, (((1,), (1,)), ((), ())), preferred_element_type=F32)


def _ffn_body(l_ref, x_ref, g_ref, wgu_ref, wd_ref, *rest, n_steps, final):
    gf_ref = rest[0] if final else None
    o_ref, h_ref = rest[-2:]
    j = pl.program_id(1)
    tf = wd_ref.shape[0]

    def step(first, last):
        if first:
            h_ref[...] = _rms(x_ref[...], g_ref[...]).astype(BF)
        gate_up = _dot(h_ref[...], wgu_ref[...])
        gate, up = gate_up[:, :tf], gate_up[:, tf:]
        a = (0.5 * gate * jax.nn.sigmoid(gate) * up).astype(BF)
        o = (x_ref[...] if first else o_ref[...]) + _dot(a, wd_ref[...])
        o_ref[...] = _rms(o, gf_ref[...]) if final and last else o

    if n_steps == 1:
        step(True, True)
    else:
        pl.when(j == 0)(functools.partial(step, True, False))
        pl.when((j > 0) & (j < n_steps - 1))(functools.partial(step, False, False))
        pl.when(j == n_steps - 1)(functools.partial(step, False, True))


def _ffn(lidx, x, g, w_gate_up, wd, final_gain=None):
    m, d = x.shape
    tf = w_gate_up.shape[-1] // 2
    f = wd.shape[1]
    tm = min(FFN_ROWS, m)
    final = final_gain is not None
    in_specs = [pl.BlockSpec((tm, d), lambda i, j, l: (i, 0)),
                pl.BlockSpec((None, 1, d), lambda i, j, l: (l[0], 0, 0)),
                pl.BlockSpec((None, None, d, 2 * tf), lambda i, j, l: (l[0], j, 0, 0)),
                pl.BlockSpec((None, tf, d), lambda i, j, l: (l[0], j, 0))]
    args = [lidx, x, g, w_gate_up, wd]
    if final:
        in_specs.append(pl.BlockSpec((1, d), lambda i, j, l: (0, 0)))
        args.append(final_gain)
    return pl.pallas_call(
        functools.partial(_ffn_body, n_steps=f // tf, final=final),
        grid_spec=_grid_spec((m // tm, f // tf), in_specs,
                             pl.BlockSpec((tm, d), lambda i, j, l: (i, 0)),
                             [pltpu.VMEM((tm, d), BF)]),
        out_shape=jax.ShapeDtypeStruct((m, d), F32),
        compiler_params=_cparams(2),
    )(*args)


def _proj_body(l_ref, x_ref, g_ref, w_ref, o_ref, h_ref):
    @pl.when(pl.program_id(1) == 0)
    def _():
        h_ref[...] = _rms(x_ref[...], g_ref[...]).astype(BF)

    o_ref[...] = _dot(h_ref[...], w_ref[...]).astype(o_ref.dtype)


def _strided_pitch(dil):
    return dil if dil % (2 * SUBLANES) else dil + SUBLANES


def _proj_in_body(l_ref, kind_ref, x_ref, g_ref, w_ref, cos_ref, sin_ref,
                  z_ref, g1_ref, g2_ref, uc_ref, h_ref, acc_ref, res_ref, *, tm, s1):
    j = pl.program_id(2)
    n_tiles = len(TILE_KINDS)
    chunks = PROJ_TILE // HEAD_DIM
    half_chunks = (VD_COL - KD_COL) // HEAD_DIM
    g1_0, g2_0, u_0 = Z_TILES, Z_TILES + G_TILES, Z_TILES + 2 * G_TILES
    n_rot = {PLAIN: 0, ROT_Q: chunks, ROT_K: chunks, ROT_HALF: half_chunks}

    def multiply_tile():
        acc_ref[...] = _dot(h_ref[...], w_ref[...])

    @pl.when(j == 0)
    def _():
        h_ref[...] = _rms(x_ref[...], g_ref[...]).astype(BF)
        multiply_tile()

    def lanes(c):
        return slice(c * HEAD_DIM, (c + 1) * HEAD_DIM)

    def finished(c, tile_kind):
        xc = acc_ref[:, lanes(c)]
        if c >= n_rot[tile_kind]:
            return xc
        rotated = xc * cos_ref[...] + pltpu.roll(xc, HEAD_DIM // 2, 1) * sin_ref[...]
        return rotated * HEAD_DIM ** -0.5 if tile_kind == ROT_Q else rotated

    def store_natural(tile_kind):
        for c in range(chunks):
            z_ref[:, lanes(c)] = finished(c, tile_kind).astype(BF)

    def store_residue_major(dst_ref, dil, tile_kind):
        pitch = _strided_pitch(dil)
        rows = tm // dil
        for c in range(chunks):
            val = finished(c, tile_kind)
            if pitch == dil:
                res_ref[c, 0:tm, :] = val
            else:
                for i in range(rows):
                    res_ref[c, i * pitch:i * pitch + dil, :] = val[i * dil:(i + 1) * dil]
        for r in range(dil):
            for c in range(chunks):
                dst_ref[r, :, lanes(c)] = res_ref[c, pl.ds(r, rows, stride=pitch), :].astype(BF)

    t = j - 1
    kind = kind_ref[jnp.maximum(t, 0)]
    destinations = (
        (0, g1_0, (PLAIN, ROT_Q, ROT_K, ROT_HALF), store_natural),
        (g1_0, g2_0, (PLAIN, ROT_Q, ROT_K), functools.partial(store_residue_major, g1_ref, A_DILATIONS[1])),
        (g2_0, u_0, (PLAIN, ROT_Q, ROT_K), functools.partial(store_residue_major, g2_ref, A_DILATIONS[2])),
        (u_0, n_tiles, (PLAIN,), functools.partial(store_residue_major, uc_ref, s1)),
    )
    for lo, hi, kinds, store in destinations:
        for tile_kind in kinds:
            in_range = (t >= lo) & (t < hi) & (kind == tile_kind)

            @pl.when(in_range & (j < n_tiles))
            def _(store=store, tile_kind=tile_kind):
                store(tile_kind)
                multiply_tile()

            if hi == n_tiles:
                @pl.when(in_range & (j == n_tiles))
                def _(store=store, tile_kind=tile_kind):
                    store(tile_kind)


def _proj(lidx, x, g, w, out_dtype=BF, tn=512):
    m, d = x.shape
    n = w.shape[-1]
    tm, tn = min(ROW_TILE, m), min(tn, n)
    return pl.pallas_call(
        _proj_body,
        grid_spec=_grid_spec(
            (m // tm, n // tn),
            [pl.BlockSpec((tm, d), lambda i, j, l: (i, 0)),
             pl.BlockSpec((None, 1, d), lambda i, j, l: (l[0], 0, 0)),
             pl.BlockSpec((None, d, tn), lambda i, j, l: (l[0], 0, j))],
            pl.BlockSpec((tm, tn), lambda i, j, l: (i, j)),
            [pltpu.VMEM((tm, d), BF)]),
        out_shape=jax.ShapeDtypeStruct((m, n), out_dtype),
        compiler_params=_cparams(2),
    )(lidx, x, g, w)


def _proj_in(lidx, x, g, w, cos_t, sin_t, batch, seq, s1):
    m, d = x.shape
    tm, tn = min(PROJ_ROWS, m), PROJ_TILE
    n_tiles = len(TILE_KINDS)
    d1, d2 = A_DILATIONS[1], A_DILATIONS[2]
    assert w.shape[1:] == (n_tiles, d, tn) and seq % tm == 0 and tm % (BF16_SUBLANES * max(d2, s1)) == 0
    nsb = seq // tm
    g1_0, g2_0, u_0 = Z_TILES, Z_TILES + G_TILES, Z_TILES + 2 * G_TILES

    def residue_major(dil, first_tile, tiles):
        return pl.BlockSpec((None, dil, tm // dil, tn),
                            lambda b, s, j, l, k: (b, 0, s, jnp.clip(j - 1 - first_tile, 0, tiles - 1)))

    return pl.pallas_call(
        functools.partial(_proj_in_body, tm=tm, s1=s1),
        grid_spec=pltpu.PrefetchScalarGridSpec(
            num_scalar_prefetch=2,
            grid=(batch, nsb, n_tiles + 1),
            in_specs=[pl.BlockSpec((tm, d), lambda b, s, j, l, k: (b * nsb + s, 0)),
                      pl.BlockSpec((None, 1, d), lambda b, s, j, l, k: (l[0], 0, 0)),
                      pl.BlockSpec((None, None, d, tn),
                                   lambda b, s, j, l, k: (l[0], jnp.minimum(j, n_tiles - 1), 0, 0)),
                      pl.BlockSpec((tm, HEAD_DIM), lambda b, s, j, l, k: (s, 0)),
                      pl.BlockSpec((tm, HEAD_DIM), lambda b, s, j, l, k: (s, 0))],
            out_specs=[pl.BlockSpec((tm, tn),
                                    lambda b, s, j, l, k: (b * nsb + s, jnp.clip(j - 1, 0, Z_TILES - 1))),
                       residue_major(d1, g1_0, G_TILES),
                       residue_major(d2, g2_0, G_TILES),
                       residue_major(s1, u_0, U_TILES)],
            scratch_shapes=[pltpu.VMEM((tm, d), BF), pltpu.VMEM((tm, tn), F32),
                            pltpu.VMEM((tn // HEAD_DIM, max(tm // g * _strided_pitch(g) for g in (d1, d2, s1)),
                                        HEAD_DIM), F32)]),
        out_shape=[jax.ShapeDtypeStruct((m, Z_WIDTH), BF),
                   jax.ShapeDtypeStruct((batch, d1, seq // d1, G_TILES * tn), BF),
                   jax.ShapeDtypeStruct((batch, d2, seq // d2, G_TILES * tn), BF),
                   jax.ShapeDtypeStruct((batch, s1, seq // s1, U_TILES * tn), BF)],
        compiler_params=_cparams(3),
    )(lidx, jnp.asarray(TILE_KINDS, jnp.int32), x, g, w, cos_t, sin_t)


def _band_attn_body(l_ref, q_ref, kp_ref, kc_ref, kn_ref, vp_ref, vc_ref, vn_ref, *rest,
                    tq, rps, length, radius, n_heads, rep, has_sink, want_lse, merge_dils):
    rest = list(rest)
    sink_ref = rest.pop(0) if has_sink else None
    others = [rest.pop(0) for _ in range(2 * len(merge_dils))]
    o_ref = rest.pop(0)
    lse_ref = rest.pop(0) if want_lse else None
    s_ref, p_ref, inv_ref = rest[:3]
    if merge_dils:
        lse0_ref, nat_ref = rest[3:]
        for a, src_ref in enumerate(others):
            dil = merge_dils[a // 2]
            for r in range(dil):
                for h in range(n_heads):
                    nat_ref[a * n_heads + h, pl.ds(r, tq // dil, stride=dil), :] = (
                        src_ref[r, :, h * HEAD_DIM:(h + 1) * HEAD_DIM].astype(F32))
    per_sequence = [[ref.at[ri] for ref in (q_ref, kp_ref, kc_ref, kn_ref, vp_ref, vc_ref, vn_ref,
                                            o_ref, lse_ref) if ref is not None] for ri in range(rps)]
    i = pl.program_id(2)
    rc = ROW_CHUNK
    nc = tq // rc
    win = rc + 2 * radius
    groups = n_heads // rep
    assert radius <= rc and radius % 16 == 0

    def window(prev_ref, cur_ref, next_ref, c, cols):
        lo, hi = c * rc - radius, (c + 1) * rc + radius
        parts = []
        if lo < 0:
            parts.append(prev_ref[rc - radius:rc, cols])
        parts.append(cur_ref[max(lo, 0):min(hi, tq), cols])
        if hi > tq:
            parts.append(next_ref[0:radius, cols])
        return parts[0] if len(parts) == 1 else jnp.concatenate(parts, axis=0)

    row = lax.broadcasted_iota(jnp.int32, (rc, win), 0)
    col = lax.broadcasted_iota(jnp.int32, (rc, win), 1)
    band_bias = jnp.where(jnp.abs(col - radius - row) <= radius, 0.0, NEG_BIG)
    key_col = lax.broadcasted_iota(jnp.int32, (1, win), 1)
    rb = SOFTMAX_ROWS
    for step, (ri, c) in enumerate((ri, c) for ri in range(rps) for c in range(nc)):
        q_ref, kp_ref, kc_ref, kn_ref, vp_ref, vc_ref, vn_ref, o_ref = per_sequence[ri][:8]
        lse_ref = per_sequence[ri][8] if want_lse else None
        slot = step % 2
        kpos = i * tq + c * rc - radius + key_col
        bias = band_bias + jnp.where(kpos >= 0, jnp.where(kpos < length, 0.0, NEG_BIG), NEG_BIG)
        rows = slice(c * rc, (c + 1) * rc)
        for g in range(groups):
            k = window(kp_ref, kc_ref, kn_ref, c, slice(g * HEAD_DIM, (g + 1) * HEAD_DIM))
            q = q_ref[rows, g * rep * HEAD_DIM:(g + 1) * rep * HEAD_DIM]
            if rep > 1:
                q = jnp.concatenate([q[:, r * HEAD_DIM:(r + 1) * HEAD_DIM] for r in range(rep)], axis=0)
            s_ref[slot, g * rep * rc:(g + 1) * rep * rc, :] = _dot_t(q, k)
        for blk in range(n_heads * rc // rb):
            h, r0 = divmod(blk * rb, rc)
            srows = slice(blk * rb, (blk + 1) * rb)
            s = s_ref[slot, srows, :] + bias[r0:r0 + rb]
            m = jnp.max(s, axis=-1, keepdims=True)
            if has_sink:
                sk = sink_ref[l_ref[0], h]
                m = jnp.maximum(m, sk)
            p = jnp.exp(s - m)
            den = jnp.sum(p, axis=-1, keepdims=True)
            if has_sink:
                den = den + jnp.exp(sk - m)
            p_ref[slot, srows, :] = p.astype(BF)
            inv_ref[slot, srows, :] = jnp.broadcast_to(1.0 / den, (rb, HEAD_DIM))
            if want_lse:
                lse_ref[c * rc + r0:c * rc + r0 + rb, h * HEAD_DIM:(h + 1) * HEAD_DIM] = (
                    jnp.broadcast_to(m + jnp.log(den), (rb, HEAD_DIM)))
            if merge_dils:
                lse0_ref[slot, srows, :] = jnp.broadcast_to(m + jnp.log(den), (rb, HEAD_DIM))
        for g in range(groups):
            v = window(vp_ref, vc_ref, vn_ref, c, slice(g * HEAD_DIM, (g + 1) * HEAD_DIM))
            grows = slice(g * rep * rc, (g + 1) * rep * rc)
            o = _dot(p_ref[slot, grows, :], v) * inv_ref[slot, grows, :]
            for r in range(rep):
                h = g * rep + r
                o_h = o[r * rc:(r + 1) * rc]
                if merge_dils:
                    lses = [lse0_ref[slot, h * rc:(h + 1) * rc, :]]
                    outs = [o_h]
                    for a in range(len(merge_dils)):
                        outs.append(nat_ref[2 * a * n_heads + h, rows, :])
                        lses.append(nat_ref[(2 * a + 1) * n_heads + h, rows, :])
                    top = functools.reduce(jnp.maximum, lses)
                    weights = [jnp.exp(l - top) for l in lses]
                    o_h = sum(w * x for w, x in zip(weights, outs)) / sum(weights)
                o_ref[rows, h * HEAD_DIM:(h + 1) * HEAD_DIM] = o_h.astype(o_ref.dtype)


def _band_attention(lidx, src, q_col, k_col, v_col, n_heads, rep, radius, sink=None, want_lse=False,
                    merge_with=()):
    batch, dil, length, width = src.shape
    tq = min(ATTN_ROWS, length)
    nc = tq // ROW_CHUNK
    qw, kw = n_heads * HEAD_DIM, (n_heads // rep) * HEAD_DIM
    assert length % tq == 0 and tq % ROW_CHUNK == 0
    assert q_col % qw == 0 and k_col % kw == 0 and v_col % kw == 0
    last_chunk = length // ROW_CHUNK - 1
    rps = min(dil, ATTN_ROWS // tq)
    assert dil % rps == 0

    def kv_specs(col):
        cb = col // kw
        return [
            pl.BlockSpec((None, rps, ROW_CHUNK, kw),
                         lambda b, r, i, l: (b, r, jnp.maximum(i * nc - 1, 0), cb)),
            pl.BlockSpec((None, rps, tq, kw), lambda b, r, i, l: (b, r, i, cb)),
            pl.BlockSpec((None, rps, ROW_CHUNK, kw),
                         lambda b, r, i, l: (b, r, jnp.minimum((i + 1) * nc, last_chunk), cb)),
        ]

    in_specs = [pl.BlockSpec((None, rps, tq, qw), lambda b, r, i, l: (b, r, i, q_col // qw))]
    in_specs += kv_specs(k_col) + kv_specs(v_col)
    args = [lidx] + [src] * 7
    if sink is not None:
        in_specs.append(pl.BlockSpec(memory_space=pltpu.SMEM))
        args.append(sink)
    scratch = [pltpu.VMEM((2, n_heads * ROW_CHUNK, ROW_CHUNK + 2 * radius), F32),
               pltpu.VMEM((2, n_heads * ROW_CHUNK, ROW_CHUNK + 2 * radius), BF),
               pltpu.VMEM((2, n_heads * ROW_CHUNK, HEAD_DIM), F32)]
    merge_dils = tuple(o.shape[1] for o, _ in merge_with)
    if merge_with:
        assert dil == 1 and not want_lse and all(tq % (BF16_SUBLANES * g) == 0 for g in merge_dils)
        for pair in merge_with:
            for arr in pair:
                g = arr.shape[1]
                in_specs.append(pl.BlockSpec((None, g, tq // g, qw), lambda b, r, i, l: (b, 0, i, 0)))
                args.append(arr)
        scratch += [pltpu.VMEM((2, n_heads * ROW_CHUNK, HEAD_DIM), F32),
                    pltpu.VMEM((2 * len(merge_with) * n_heads, tq, HEAD_DIM), F32)]
    out_spec = pl.BlockSpec((None, rps, tq, qw), lambda b, r, i, l: (b, r, i, 0))
    out_shape = [jax.ShapeDtypeStruct((batch, dil, length, qw), BF)]
    out_specs = [out_spec]
    if want_lse:
        out_shape.append(jax.ShapeDtypeStruct((batch, dil, length, qw), F32))
        out_specs.append(out_spec)
    return pl.pallas_call(
        functools.partial(_band_attn_body, tq=tq, rps=rps, length=length, radius=radius, n_heads=n_heads,
                          rep=rep, has_sink=sink is not None, want_lse=want_lse, merge_dils=merge_dils),
        grid_spec=_grid_spec((batch, dil // rps, length // tq), in_specs, out_specs, scratch),
        out_shape=out_shape,
        compiler_params=_cparams(3),
    )(*args)


def _conv_body(l_ref, bg_ref, cg_ref, xb_ref, cgp_ref, xbp_ref, cgn_ref, xbn_ref, w_ref, o_ref,
               *, tm, seq):
    i = pl.program_id(0)
    u = cg_ref[...].astype(F32) * xb_ref[...].astype(F32)
    not_first = ((i * tm) % seq != 0).astype(F32)
    not_last = (((i + 1) * tm) % seq != 0).astype(F32)
    last = SUBLANES - 1
    u_prev = cgp_ref[last:, :].astype(F32) * xbp_ref[last:, :].astype(F32) * not_first
    u_next = cgn_ref[0:1, :].astype(F32) * xbn_ref[0:1, :].astype(F32) * not_last
    row = lax.broadcasted_iota(jnp.int32, u.shape, 0)
    below = jnp.where(row == 0, u_prev, pltpu.roll(u, 1, 0))
    above = jnp.where(row == tm - 1, u_next, pltpu.roll(u, tm - 1, 0))
    y = below * w_ref[0:1, :] + u * w_ref[1:2, :] + above * w_ref[2:3, :]
    o_ref[...] = (bg_ref[...].astype(F32) * y).astype(o_ref.dtype)


def _short_conv(lidx, z, w_conv, seq):
    m = z.shape[0]
    tm, tc = min(WIDE_ROWS, m), B_WIDTH
    halo = SUBLANES
    assert seq % tm == 0 and B_WIDTH % tc == 0
    last_halo = m // halo - 1

    def tile(col):
        return pl.BlockSpec((tm, tc), lambda i, j, l: (i, col // tc + j))

    def prev(col):
        return pl.BlockSpec((halo, tc), lambda i, j, l: (jnp.maximum(i * (tm // halo) - 1, 0), col // tc + j))

    def nxt(col):
        return pl.BlockSpec((halo, tc),
                            lambda i, j, l: (jnp.minimum((i + 1) * (tm // halo), last_halo), col // tc + j))

    return pl.pallas_call(
        functools.partial(_conv_body, tm=tm, seq=seq),
        grid_spec=_grid_spec(
            (m // tm, B_WIDTH // tc),
            [tile(BG_COL), tile(CG_COL), tile(XB_COL), prev(CG_COL), prev(XB_COL), nxt(CG_COL), nxt(XB_COL),
             pl.BlockSpec((None, 3, tc), lambda i, j, l: (l[0], 0, j))],
            pl.BlockSpec((tm, tc), lambda i, j, l: (i, j))),
        out_shape=jax.ShapeDtypeStruct((m, B_WIDTH), BF),
        compiler_params=_cparams(2),
    )(lidx, z, z, z, z, z, z, z, w_conv)


def _dft1_body(l_ref, u_ref, g_ref, ar_ref, ai_ref, *, tn1, tk2, s2, groups):
    for jj in range(tn1):
        r = _dot(g_ref[jj], u_ref[jj])
        for g in range(groups):
            gc = slice(g * C_GROUP_DIM, (g + 1) * C_GROUP_DIM)
            for t in range(s2 // tk2):
                dst = slice(jj * tk2, (jj + 1) * tk2)
                ar_ref[g, t, dst, :] = r[t * tk2:(t + 1) * tk2, gc]
                ai_ref[g, t, dst, :] = r[s2 + t * tk2:s2 + (t + 1) * tk2, gc]


def _dft2_body(l_ref, ar_ref, ai_ref, m2_ref, csc_ref, y_ref, xr_ref, xi_ref,
               *, tk2, s1, groups, scale):
    for j in range(tk2):
        rows = pl.ds(j, s1, stride=tk2)
        a_re = jnp.concatenate([ar_ref[g, rows, :] for g in range(groups)], axis=1)
        a_im = jnp.concatenate([ai_ref[g, rows, :] for g in range(groups)], axis=1)
        x = _dot(m2_ref[...], jnp.concatenate([a_re, a_im], axis=0).astype(BF))
        for g in range(groups):
            gc = slice(g * C_GROUP_DIM, (g + 1) * C_GROUP_DIM)
            xr_ref[g, rows, :] = x[:s1, gc]
            xi_ref[g, rows, :] = x[s1:, gc]
    for g in range(groups):
        gc = slice(g * C_GROUP_DIM, (g + 1) * C_GROUP_DIM)
        x_g = jnp.concatenate([xr_ref[g], xi_ref[g]], axis=1).astype(BF)
        yg = _dot(x_g, csc_ref[...]) * scale
        for k1 in range(s1):
            y_ref[k1, :, gc] = yg[k1 * tk2:(k1 + 1) * tk2].astype(y_ref.dtype)


def _fourier_mix(lidx, u, tables):
    g1, m2, csc = tables
    batch, s1, s2, width = u.shape
    groups = width // C_GROUP_DIM
    tn1, tk2 = DFT_RESIDUES, DFT_K2
    a_shape = jax.ShapeDtypeStruct((batch, groups, s2 // tk2, s1 * tk2, C_GROUP_DIM), F32)
    a_out = pl.BlockSpec((None, groups, s2 // tk2, tn1 * tk2, C_GROUP_DIM), lambda b, t, l: (b, 0, 0, t, 0))
    ar, ai = pl.pallas_call(
        functools.partial(_dft1_body, tn1=tn1, tk2=tk2, s2=s2, groups=groups),
        grid_spec=_grid_spec(
            (batch, s1 // tn1),
            [pl.BlockSpec((None, tn1, s2, width), lambda b, t, l: (b, t, 0, 0)),
             pl.BlockSpec((tn1, 2 * s2, s2), lambda b, t, l: (t, 0, 0))],
            [a_out, a_out]),
        out_shape=[a_shape, a_shape],
        compiler_params=_cparams(2),
    )(lidx, u, g1)
    a_in = pl.BlockSpec((None, groups, None, s1 * tk2, C_GROUP_DIM), lambda b, t, l: (b, 0, t, 0, 0))
    y = pl.pallas_call(
        functools.partial(_dft2_body, tk2=tk2, s1=s1, groups=groups,
                          scale=float((s1 * s2 * C_GROUP_DIM) ** -0.5)),
        grid_spec=_grid_spec(
            (batch, s2 // tk2),
            [a_in, a_in,
             pl.BlockSpec((2 * s1, 2 * s1), lambda b, t, l: (0, 0)),
             pl.BlockSpec((2 * C_GROUP_DIM, C_GROUP_DIM), lambda b, t, l: (0, 0))],
            pl.BlockSpec((None, s1, tk2, width), lambda b, t, l: (b, 0, t, 0)),
            [pltpu.VMEM((groups, tk2 * s1, C_GROUP_DIM), F32)] * 2),
        out_shape=jax.ShapeDtypeStruct((batch, s1, s2, width), BF),
        compiler_params=_cparams(2),
    )(lidx, ar, ai, m2, csc)
    return y.reshape(batch * s1 * s2, width)


def _dft_split(seq):
    s2 = 256 if seq >= 8192 else 128
    return seq // s2, s2


def _dft_tables(seq):
    s1, s2 = _dft_split(seq)
    two_pi = 2.0 * jnp.pi

    def cs(idx, period):
        ang = (idx % period).astype(F32) * (two_pi / period)
        return jnp.cos(ang), jnp.sin(ang)

    n1 = jnp.arange(s1, dtype=jnp.int32)[:, None, None]
    k2 = jnp.arange(s2, dtype=jnp.int32)[None, :, None]
    n2 = jnp.arange(s2, dtype=jnp.int32)[None, None, :]
    ca, sa = cs(k2 * (n1 + s1 * n2), seq)
    g1 = jnp.concatenate([ca, -sa], axis=1).astype(BF)
    a = jnp.arange(s1, dtype=jnp.int32)
    cb, sb = cs(a[:, None] * a[None, :], s1)
    m2 = jnp.concatenate([jnp.concatenate([cb, sb], axis=1),
                          jnp.concatenate([-sb, cb], axis=1)], axis=0).astype(BF)
    c = jnp.arange(C_GROUP_DIM, dtype=jnp.int32)
    cc, sc = cs(c[:, None] * c[None, :], C_GROUP_DIM)
    return g1, m2, jnp.concatenate([cc, sc], axis=0).astype(BF)


def _memory_update(x, g_ref, wq_ref, kv_ref, wo_ref, att_ref):
    h = _rms(x, g_ref[...]).astype(BF)
    q = (_dot(h, wq_ref[...]) * HEAD_DIM ** -0.5).astype(BF)
    mw = M_HEADS * HEAD_DIM
    for hh in range(M_HEADS):
        cols = slice(hh * HEAD_DIM, (hh + 1) * HEAD_DIM)
        k = kv_ref[:, cols]
        v = kv_ref[:, mw + hh * HEAD_DIM:mw + (hh + 1) * HEAD_DIM]
        s = _dot_t(q[:, cols], k)
        p = jnp.exp(s - jnp.max(s, axis=-1, keepdims=True))
        o = _dot(p.astype(BF), v) / jnp.sum(p, axis=-1, keepdims=True)
        att_ref[:, cols] = o.astype(BF)
    return x + _dot(att_ref[...], wo_ref[...])


def _merge_body(l_ref, x_ref, g_ref, wgate_ref, bgate_ref, ya_ref, yd_ref, yb_ref, yc_ref,
                wpa_ref, wpd_ref, wpb_ref, wpc_ref, wo_ref, gq_ref, wq_ref, kv_ref, wmo_ref,
                o_ref, h_ref, att_ref, *, tn):
    n = pl.program_id(1)

    def step(first):
        if first:
            h_ref[...] = _rms(x_ref[...], g_ref[...]).astype(BF)
        gates = jax.nn.sigmoid(_dot(h_ref[...], wgate_ref[...]) + bgate_ref[...])
        merged = None
        for i, (br_ref, wp_ref) in enumerate(((ya_ref, wpa_ref), (yd_ref, wpd_ref),
                                              (yb_ref, wpb_ref), (yc_ref, wpc_ref))):
            term = gates[:, i * tn:(i + 1) * tn] * _dot(br_ref[...], wp_ref[...])
            merged = term if merged is None else merged + term
        o_ref[...] = (x_ref[...] if first else o_ref[...]) + _dot(merged.astype(BF), wo_ref[...])

    pl.when(n == 0)(functools.partial(step, True))
    pl.when(n > 0)(functools.partial(step, False))

    @pl.when(n == pl.num_programs(1) - 1)
    def _():
        o_ref[...] = _memory_update(o_ref[...], gq_ref, wq_ref, kv_ref, wmo_ref, att_ref)


def _merge(lidx, x, g, w_gate, b_gate, branches, projs, w_o, g_memq, w_mq, kv, w_mo, seq):
    m, d = x.shape
    tm = min(MERGE_ROWS, m)
    tn = w_gate.shape[-1] // 4
    n_mem, kvw = kv.shape[1], kv.shape[2]
    mw = w_mq.shape[-1]
    assert seq % tm == 0

    def row_tile(a):
        return pl.BlockSpec((tm, a.shape[1]), lambda i, n, l: (i, 0))

    def proj_tile(w):
        return pl.BlockSpec((None, w.shape[1], tn), lambda i, n, l: (l[0], 0, n))

    return pl.pallas_call(
        functools.partial(_merge_body, tn=tn),
        grid_spec=_grid_spec(
            (m // tm, d // tn),
            [row_tile(x),
             pl.BlockSpec((None, 1, d), lambda i, n, l: (l[0], 0, 0)),
             pl.BlockSpec((None, None, d, 4 * tn), lambda i, n, l: (l[0], n, 0, 0)),
             pl.BlockSpec((None, None, 1, 4 * tn), lambda i, n, l: (l[0], n, 0, 0))]
            + [row_tile(b) for b in branches] + [proj_tile(w) for w in projs]
            + [pl.BlockSpec((None, tn, d), lambda i, n, l: (l[0], n, 0)),
               pl.BlockSpec((None, 1, d), lambda i, n, l: (l[0], 0, 0)),
               pl.BlockSpec((None, d, mw), lambda i, n, l: (l[0], 0, 0)),
               pl.BlockSpec((None, n_mem, kvw), lambda i, n, l: ((i * tm) // seq, 0, 0)),
               pl.BlockSpec((None, mw, d), lambda i, n, l: (l[0], 0, 0))],
            row_tile(x),
            [pltpu.VMEM((tm, d), BF), pltpu.VMEM((tm, mw), BF)]),
        out_shape=jax.ShapeDtypeStruct((m, d), F32),
        compiler_params=_cparams(2),
    )(lidx, x, g, w_gate, b_gate, *branches, *projs, w_o, g_memq, w_mq, kv, w_mo)


def _cast_body(*refs):
    x_ref, o_ref = refs[-2:]
    o_ref[...] = x_ref[...].astype(o_ref.dtype)


def _cast_w_in_tiles(w_in, source_tiles):
    depth, d, _ = w_in.shape
    n = len(source_tiles)
    return pl.pallas_call(
        _cast_body,
        grid_spec=_grid_spec((depth, n),
                             [pl.BlockSpec((None, d, PROJ_TILE), lambda l, j, t: (l, 0, t[j]))],
                             pl.BlockSpec((None, None, d, PROJ_TILE), lambda l, j, t: (l, j, 0, 0))),
        out_shape=jax.ShapeDtypeStruct((depth, n, d, PROJ_TILE), BF),
        compiler_params=_cparams(2),
    )(jnp.asarray(source_tiles, jnp.int32), w_in)


def _cast_pair_body(a_ref, b_ref, o_ref):
    half = a_ref.shape[-1]
    o_ref[:, :half] = a_ref[...].astype(o_ref.dtype)
    o_ref[:, half:] = b_ref[...].astype(o_ref.dtype)


def _cast_gate_up_slabs(wg, wu, tf):
    depth, d, f = wg.shape
    spec = pl.BlockSpec((None, d, tf), lambda l, j: (l, 0, j))
    return pl.pallas_call(
        _cast_pair_body,
        grid=(depth, f // tf),
        in_specs=[spec, spec],
        out_specs=pl.BlockSpec((None, None, d, 2 * tf), lambda l, j: (l, j, 0, 0)),
        out_shape=jax.ShapeDtypeStruct((depth, f // tf, d, 2 * tf), BF),
        compiler_params=_cparams(2),
    )(wg, wu)


def _cast_gate_slabs(w_gate, tn):
    depth, n_gates, d, _ = w_gate.shape
    return pl.pallas_call(
        _cast_body,
        grid=(depth, d // tn, n_gates),
        in_specs=[pl.BlockSpec((None, None, d, tn), lambda l, n, i: (l, i, 0, n))],
        out_specs=pl.BlockSpec((None, None, d, tn), lambda l, n, i: (l, n, 0, i)),
        out_shape=jax.ShapeDtypeStruct((depth, d // tn, d, n_gates * tn), BF),
        compiler_params=_cparams(3),
    )(w_gate)


def _rope_tables(seq):
    half = HEAD_DIM // 2
    inv = ROPE_THETA ** (-jnp.arange(half, dtype=F32) / half)
    ang = jnp.arange(seq, dtype=jnp.int32).astype(F32)[:, None] * inv[None, :]
    cos, sin = jnp.cos(ang), jnp.sin(ang)
    return jnp.concatenate([cos, cos], axis=1), jnp.concatenate([-sin, sin], axis=1)


def _layer(lidx, x, mem, p, batch, seq, tabs, final_gain=None):
    cos_t, sin_t, dft = tabs
    x = _ffn(lidx, x, p["g_ffn1"], p["w_ffn1_gate_up"], p["w_ffn1_down"])

    s1, _ = _dft_split(seq)
    z, zg1, zg2, uc = _proj_in(lidx, x, p["g_mix"], p["w_in"], cos_t, sin_t, batch, seq, s1)
    z4 = z.reshape(batch, 1, seq, Z_WIDTH)
    dilated = [_band_attention(lidx, src, 0, GROUP_WIDTH, 2 * GROUP_WIDTH, A_HEADS_PER_GROUP, 1,
                               A_RADIUS, want_lse=True) for src in (zg1, zg2)]
    (ya,) = _band_attention(lidx, z4, QA0_COL, KA0_COL, VA0_COL, A_HEADS_PER_GROUP, 1, A_RADIUS,
                            merge_with=dilated)
    ya = ya.reshape(batch * seq, A_HEADS_PER_GROUP * HEAD_DIM)
    (yd,) = _band_attention(lidx, z4, QD_COL, KD_COL, VD_COL, D_Q_HEADS,
                            D_Q_HEADS // D_KV_HEADS, D_RADIUS, sink=p["sink"])
    yd = yd.reshape(batch * seq, D_Q_HEADS * HEAD_DIM)
    yb = _short_conv(lidx, z, p["w_conv"], seq)
    yc = _fourier_mix(lidx, uc, dft)
    kv = _proj(lidx, mem, p["g_memkv"], p["w_mkv"])
    kv = kv.reshape(batch, mem.shape[0] // batch, kv.shape[1])
    x = _merge(lidx, x, p["g_mix"], p["w_gate"], p["b_gate"], (ya, yd, yb, yc),
               (p["w_pa"], p["w_pd"], p["w_pb"], p["w_pc"]), p["w_o"],
               p["g_memq"], p["w_mq"], kv, p["w_mo"], seq)

    return _ffn(lidx, x, p["g_ffn2"], p["w_ffn2_gate_up"], p["w_ffn2_down"], final_gain)


def _forward(x_prompt, x_sample, mem_prompt, mem_sample, p, g_final):
    depth = p["w_o"].shape[0]
    d = x_prompt.shape[-1]
    trunks = []
    for x, mem in ((x_prompt, mem_prompt), (x_sample, mem_sample)):
        batch, seq = x.shape[0], x.shape[1]
        tabs = _rope_tables(seq) + (_dft_tables(seq),)
        trunks.append((batch, seq, tabs, mem.reshape(-1, d)))

    gf = g_final.reshape(1, d)
    xs = [x_prompt.reshape(-1, d), x_sample.reshape(-1, d)]
    for layer in range(depth):
        lidx = jnp.full((1,), layer, jnp.int32)
        xs = [_layer(lidx, x, mem, p, batch, seq, tabs, gf if layer == depth - 1 else None)
              for x, (batch, seq, tabs, mem) in zip(xs, trunks)]
    return tuple(x.reshape(orig.shape) for x, orig in zip(xs, (x_prompt, x_sample)))


def kernel(x_prompt, x_sample, mem_prompt, mem_sample, g_ffn1, w_ffn1_gate, w_ffn1_up, w_ffn1_down, g_mix, w_in, w_conv, sink, w_gate, b_gate, w_pa, w_pd, w_pb, w_pc, w_o, g_memq, g_memkv, w_mq, w_mkv, w_mo, g_ffn2, w_ffn2_gate, w_ffn2_up, w_ffn2_down, g_final):
    depth, d = g_ffn1.shape

    def gain(g):
        return g.reshape(depth, 1, d)

    def tiles(lo, width):
        return tuple(range(lo // PROJ_TILE, (lo + width) // PROJ_TILE))

    qa, ka, va = (tiles(i * A_WIDTH, A_WIDTH) for i in range(3))
    d_start = 3 * A_WIDTH
    d_width = (D_Q_HEADS + 2 * D_KV_HEADS) * HEAD_DIM
    conv = tiles(d_start + d_width, 3 * B_WIDTH)
    fourier = tiles(d_start + d_width + 3 * B_WIDTH, C_WIDTH)
    source_tiles = ((qa[0], ka[0]) + tiles(d_start, d_width) + (va[0],) + conv
                    + (qa[1], ka[1], va[1], qa[2], ka[2], va[2]) + fourier)
    assert len(source_tiles) == len(TILE_KINDS)
    w_in_tiles = _cast_w_in_tiles(w_in, source_tiles)
    tn = min(MERGE_COLS, d)
    w_gate_slabs = _cast_gate_slabs(w_gate, tn)
    b_gate_slabs = (b_gate.reshape(depth, 4, d // tn, tn).transpose(0, 2, 1, 3)
                    .reshape(depth, d // tn, 1, 4 * tn))
    tf = min(FFN_COLS, w_ffn1_gate.shape[-1])
    p = {
        "g_ffn1": gain(g_ffn1), "w_ffn1_gate_up": _cast_gate_up_slabs(w_ffn1_gate, w_ffn1_up, tf),
        "w_ffn1_down": w_ffn1_down.astype(BF),
        "g_mix": gain(g_mix), "w_in": w_in_tiles,
        "w_conv": w_conv, "sink": sink,
        "w_gate": w_gate_slabs, "b_gate": b_gate_slabs,
        "w_pa": w_pa.astype(BF), "w_pd": w_pd.astype(BF), "w_pb": w_pb.astype(BF), "w_pc": w_pc.astype(BF),
        "w_o": w_o.astype(BF),
        "g_memq": gain(g_memq), "g_memkv": gain(g_memkv), "w_mq": w_mq.astype(BF),
        "w_mkv": w_mkv.astype(BF), "w_mo": w_mo.astype(BF),
        "g_ffn2": gain(g_ffn2), "w_ffn2_gate_up": _cast_gate_up_slabs(w_ffn2_gate, w_ffn2_up, tf),
        "w_ffn2_down": w_ffn2_down.astype(BF),
    }
    return _forward(x_prompt, x_sample, mem_prompt, mem_sample, p, g_final)
```
